```python
import math
import jax
import jax.numpy as jnp
from jax import lax
import numpy as np

D_MODEL = 1024
BATCH = 2
SEQ = 16384
DEPTH = 2

GRID_W = 64
CTX_LEN = 256

HY_WIDTH = D_MODEL // 4
NA_HEAD_DIM = 64
NA_WIDTH = D_MODEL // 2
NA_HEADS = NA_WIDTH // NA_HEAD_DIM
LRU_WIDTH = D_MODEL // 4
LRU_BLOCKS = 4
MIX_WIDTH = HY_WIDTH + NA_WIDTH + LRU_WIDTH
IN_WIDTH = 3 * HY_WIDTH + 3 * NA_WIDTH + 2 * LRU_WIDTH

HY_ORDER = 2
HY_SHORT_CONV = 3
HY_BANDS = 16
HY_EMB = 2 * HY_BANDS + 1
HY_FILTER_HIDDEN = 64
HY_FAST_DECAY = 0.3
HY_SLOW_DECAY = 1.5
HY_DECAY_TARGET = 1e-2

NA_WIN_ROWS = 8
NA_WIN_COLS = 16
NA_COL_BLOCK = 16
NA_KEY_COLS = 32

LRU_CONV = 4
LRU_C = 8.0

D_FF = -(-8 * D_MODEL // (3 * 256)) * 256
RMS_EPS = 1e-6
NEG_INF = -1e30

kernel_name = 'hybrid_hyena_natten_rglru_diffusion_block'


def rmsnorm(x, g):
    xf = x.astype(jnp.float32)
    y = xf * lax.rsqrt(jnp.mean(xf * xf, axis=-1, keepdims=True) + RMS_EPS)
    return (y * g.astype(jnp.float32)).astype(x.dtype)


def modulation(cond, w, b):
    m = jax.nn.silu(cond) @ w + b
    return jnp.split(m, 6, axis=-1)


def conv_centred(x, w, b):
    K = w.shape[0]
    L = x.shape[1]
    pl = K // 2
    pr = K - 1 - pl
    xp = jnp.pad(x, ((0, 0), (pl, pr), (0, 0)))
    y = b
    for k in range(K):
        y = y + w[k] * xp[:, k:k + L]
    return y


def hyena_filters(L, f_w1, f_b1, f_w2, f_b2, f_w3, f_freq):
    t = jnp.arange(L, dtype=jnp.float32) / L
    bands = jnp.arange(1, HY_BANDS + 1, dtype=jnp.float32)
    ang = 2.0 * math.pi * t[:, None] * bands[None, :]
    z = jnp.concatenate([t[:, None], jnp.cos(ang), jnp.sin(ang)], axis=-1)
    h = jnp.sin(f_freq[0] * (z @ f_w1 + f_b1))
    h = jnp.sin(f_freq[1] * (h @ f_w2 + f_b2))
    k = (h @ f_w3).astype(jnp.float32).reshape(L, HY_ORDER, 2, HY_WIDTH)
    deltas = jnp.abs(jnp.linspace(math.log(HY_DECAY_TARGET) / HY_FAST_DECAY,
                                  math.log(HY_DECAY_TARGET) / HY_SLOW_DECAY,
                                  HY_WIDTH, dtype=jnp.float32))
    k = k * jnp.exp(-t[:, None, None, None] * deltas)
    kf = k[:, :, 0]
    kb = k[:, :, 1]
    k_circ = jnp.concatenate([kf, jnp.zeros_like(kf[:1]), kb[:0:-1]], axis=0)
    k_circ = k_circ / jnp.sum(jnp.abs(k_circ), axis=0, keepdims=True)
    return jnp.fft.rfft(k_circ, axis=0)


def fft_conv(u, k_f):
    L = u.shape[1]
    U = jnp.fft.rfft(u, n=2 * L, axis=1)
    return jnp.fft.irfft(U * k_f[None], n=2 * L, axis=1)[:, :L]


def hyena_mixer(p, conv_w, conv_b, filt, hy_bias):
    L = p.shape[1]
    u = conv_centred(p, conv_w, conv_b).astype(jnp.float32)
    v, x1, x2 = jnp.split(u, 3, axis=-1)
    k_f = hyena_filters(L, *filt)
    z = v
    for n, gate in enumerate((x1, x2)):
        z = gate * (fft_conv(z, k_f[:, n]) + hy_bias[n].astype(jnp.float32) * z)
    return z


def na_static(grid_w):
    n_cb = grid_w // NA_COL_BLOCK
    qc = np.arange(grid_w).reshape(n_cb, NA_COL_BLOCK)
    kc0 = np.clip(np.arange(n_cb) * NA_COL_BLOCK - NA_WIN_COLS // 2, 0, grid_w - NA_KEY_COLS)
    kc = kc0[:, None] + np.arange(NA_KEY_COLS)[None, :]
    start = np.clip(qc - NA_WIN_COLS // 2, 0, grid_w - NA_WIN_COLS)
    valid = (kc[:, None, :] >= start[:, :, None]) & (kc[:, None, :] < start[:, :, None] + NA_WIN_COLS)
    dc = np.clip(kc[:, None, :] - qc[:, :, None], -(NA_WIN_COLS - 1), NA_WIN_COLS - 1) + NA_WIN_COLS - 1
    return kc, valid, dc


def na_latent(q, k, v, k_ctx, v_ctx, rpb, rows):
    B, L, H, dh = q.shape
    kr = min(NA_WIN_ROWS, rows)
    kc, valid, dc = na_static(GRID_W)
    n_cb = kc.shape[0]
    n_loc = kr * NA_KEY_COLS
    scale = dh ** -0.5
    valid = jnp.asarray(valid)[:, :, None, :]
    qg = q.reshape(B, rows, n_cb, NA_COL_BLOCK, H, dh).transpose(1, 0, 4, 2, 3, 5)
    kg = k.reshape(B, rows, GRID_W, H, dh).transpose(0, 3, 1, 2, 4)
    vg = v.reshape(B, rows, GRID_W, H, dh).transpose(0, 3, 1, 2, 4)
    kcx = k_ctx.transpose(0, 2, 1, 3)
    vcx = v_ctx.transpose(0, 2, 1, 3)

    def row_step(args):
        q_r, r = args
        rs = jnp.clip(r - kr // 2, 0, rows - kr)
        k_blk = lax.dynamic_slice_in_dim(kg, rs, kr, axis=2)[:, :, :, kc]
        v_blk = lax.dynamic_slice_in_dim(vg, rs, kr, axis=2)[:, :, :, kc]
        s_loc = jnp.einsum('bhnqd,bhrnkd->bhnqrk', q_r, k_blk).astype(jnp.float32) * scale
        dr = rs + jnp.arange(kr) - r + NA_WIN_ROWS - 1
        bias = rpb[:, dr][:, :, dc].transpose(0, 2, 3, 1, 4)
        s_loc = jnp.where(valid, s_loc + bias[None].astype(jnp.float32), NEG_INF)
        s_ctx = jnp.einsum('bhnqd,bhcd->bhnqc', q_r, kcx).astype(jnp.float32) * scale
        s = jnp.concatenate([s_loc.reshape(B, H, n_cb, NA_COL_BLOCK, n_loc), s_ctx], axis=-1)
        p = jax.nn.softmax(s, axis=-1)
        p_loc = p[..., :n_loc].reshape(B, H, n_cb, NA_COL_BLOCK, kr, NA_KEY_COLS).astype(v.dtype)
        p_ctx = p[..., n_loc:].astype(v.dtype)
        return (jnp.einsum('bhnqrk,bhrnkd->bhnqd', p_loc, v_blk)
                + jnp.einsum('bhnqc,bhcd->bhnqd', p_ctx, vcx))

    o = lax.map(row_step, (qg, jnp.arange(rows)))
    return o.transpose(1, 0, 3, 4, 2, 5).reshape(B, L, H * dh)


def attn_ctx(q, k, v):
    B, Lc, H, dh = q.shape
    s = jnp.einsum('bqhd,bkhd->bhqk', q, k).astype(jnp.float32) * (dh ** -0.5)
    p = jax.nn.softmax(s, axis=-1).astype(v.dtype)
    return jnp.einsum('bhqk,bkhd->bqhd', p, v).reshape(B, Lc, H * dh)


def linear_scan(a, b, h0, reverse):
    idx = -1 if reverse else 0
    b = b.at[:, idx].add(a[:, idx] * h0)

    def combine(e1, e2):
        a1, b1 = e1
        a2, b2 = e2
        return a1 * a2, a2 * b1 + b2

    _, h = lax.associative_scan(combine, (a, b), reverse=reverse, axis=1)
    return h


def rglru_coeffs(u, wa, ba, wi, bi, lam):
    B, L, C = u.shape
    ub = u.reshape(B, L, LRU_BLOCKS, C // LRU_BLOCKS)
    r = jax.nn.sigmoid(jnp.einsum('blnc,ncd->blnd', ub, wa.astype(jnp.float32)).reshape(B, L, C) + ba)
    i = jax.nn.sigmoid(jnp.einsum('blnc,ncd->blnd', ub, wi.astype(jnp.float32)).reshape(B, L, C) + bi)
    log_a = -LRU_C * r * jax.nn.softplus(-lam.astype(jnp.float32))
    a = jnp.exp(log_a)
    b = jnp.sqrt(-jnp.expm1(2.0 * log_a)) * (i * u)
    return a, b


def lru_mixer(p, pc, conv_w, conv_b, wa, ba, wi, bi, lam, with_ctx_out):
    xr, xg = jnp.split(p, 2, axis=-1)
    xr_c, xg_c = jnp.split(pc, 2, axis=-1)
    u = conv_centred(xr, conv_w, conv_b).astype(jnp.float32)
    u_c = conv_centred(xr_c, conv_w, conv_b).astype(jnp.float32)
    h_lat = 0.0
    h_ctx = 0.0
    for d, rev in enumerate((False, True)):
        a_c, b_c = rglru_coeffs(u_c, wa[d], ba[d], wi[d], bi[d], lam[d])
        hc = linear_scan(a_c, b_c, jnp.zeros_like(b_c[:, 0]), rev)
        h_end = hc[:, 0] if rev else hc[:, -1]
        a_l, b_l = rglru_coeffs(u, wa[d], ba[d], wi[d], bi[d], lam[d])
        h_lat = h_lat + linear_scan(a_l, b_l, h_end, rev)
        h_ctx = h_ctx + hc
    y = (h_lat * jax.nn.gelu(xg.astype(jnp.float32))).astype(p.dtype)
    if not with_ctx_out:
        return y, None
    yc = (h_ctx * jax.nn.gelu(xg_c.astype(jnp.float32))).astype(p.dtype)
    return y, yc


def mixer(h, hc, rows, with_ctx_out, w_in, w_out, hy_conv_w, hy_conv_b, hy_filt, hy_bias, na_rpb, lru_params):
    B, L, _ = h.shape
    Lc = hc.shape[1]
    s1 = 3 * HY_WIDTH
    s2 = s1 + 3 * NA_WIDTH
    p = h @ w_in
    pc = hc @ w_in
    y_hy = hyena_mixer(p[..., :s1], hy_conv_w, hy_conv_b, hy_filt, hy_bias).astype(h.dtype)
    qkv = p[..., s1:s2].reshape(B, L, 3, NA_HEADS, NA_HEAD_DIM)
    qkv_c = pc[..., s1:s2].reshape(B, Lc, 3, NA_HEADS, NA_HEAD_DIM)
    y_na = na_latent(qkv[:, :, 0], qkv[:, :, 1], qkv[:, :, 2], qkv_c[:, :, 1], qkv_c[:, :, 2], na_rpb, rows)
    y_lru, yc_lru = lru_mixer(p[..., s2:], pc[..., s2:], *lru_params, with_ctx_out)
    y = jnp.concatenate([y_hy, y_na, y_lru], axis=-1) @ w_out
    if not with_ctx_out:
        return y, None
    yc_hy = hyena_mixer(pc[..., :s1], hy_conv_w, hy_conv_b, hy_filt, hy_bias).astype(hc.dtype)
    yc_na = attn_ctx(qkv_c[:, :, 0], qkv_c[:, :, 1], qkv_c[:, :, 2])
    yc = jnp.concatenate([yc_hy, yc_na, yc_lru], axis=-1) @ w_out
    return y, yc


def swiglu(h, w_gu, w_down):
    g, u = jnp.split(h @ w_gu, 2, axis=-1)
    return (jax.nn.silu(g) * u) @ w_down


def setup_inputs(seed: int = 0) -> dict:
    key = jax.random.key(seed)
    ks = jax.random.split(key, 32)
    f32 = jnp.float32
    D = D_MODEL
    blk = LRU_WIDTH // LRU_BLOCKS

    def nrm(i, shape, scale):
        return scale * jax.random.normal(ks[i], shape, f32)

    a8 = jax.random.uniform(ks[28], (DEPTH, 2, LRU_WIDTH), f32, 0.9, 0.999)
    a_base = a8 ** (1.0 / LRU_C)
    lru_lam = jnp.log(a_base) - jnp.log1p(-a_base)
    return {
        'x': nrm(0, (BATCH, SEQ, D), 1.0),
        'c': nrm(1, (BATCH, D), 1.0),
        'ctx': nrm(2, (BATCH, CTX_LEN, D), 1.0),
        'c_ctx': nrm(3, (D,), 1.0),
        'ada_w': nrm(4, (DEPTH, D, 6 * D), D ** -0.5),
        'ada_b': nrm(5, (DEPTH, 6 * D), 0.02),
        'g_mix_pre': 1.0 + nrm(6, (DEPTH, D), 0.02),
        'g_mix_post': 1.0 + nrm(7, (DEPTH, D), 0.02),
        'g_ffn_pre': 1.0 + nrm(8, (DEPTH, D), 0.02),
        'g_ffn_post': 1.0 + nrm(9, (DEPTH, D), 0.02),
        'w_in': nrm(10, (DEPTH, D, IN_WIDTH), D ** -0.5),
        'w_out': nrm(11, (DEPTH, MIX_WIDTH, D), MIX_WIDTH ** -0.5),
        'hy_conv_w': nrm(12, (DEPTH, HY_SHORT_CONV, 3 * HY_WIDTH), HY_SHORT_CONV ** -0.5),
        'hy_conv_b': nrm(13, (DEPTH, 3 * HY_WIDTH), 0.02),
        'hy_f_w1': nrm(14, (DEPTH, HY_EMB, HY_FILTER_HIDDEN), HY_EMB ** -0.5),
        'hy_f_b1': nrm(15, (DEPTH, HY_FILTER_HIDDEN), 0.02),
        'hy_f_w2': nrm(16, (DEPTH, HY_FILTER_HIDDEN, HY_FILTER_HIDDEN), HY_FILTER_HIDDEN ** -0.5),
        'hy_f_b2': nrm(17, (DEPTH, HY_FILTER_HIDDEN), 0.02),
        'hy_f_w3': nrm(18, (DEPTH, HY_FILTER_HIDDEN, HY_ORDER * 2 * HY_WIDTH), HY_FILTER_HIDDEN ** -0.5),
        'hy_f_freq': 1.0 + nrm(19, (DEPTH, 2, HY_FILTER_HIDDEN), 0.1),
        'hy_bias': nrm(20, (DEPTH, HY_ORDER, HY_WIDTH), 1.0),
        'na_rpb': nrm(21, (DEPTH, NA_HEADS, 2 * NA_WIN_ROWS - 1, 2 * NA_WIN_COLS - 1), 0.02),
        'lru_conv_w': nrm(22, (DEPTH, LRU_CONV, LRU_WIDTH), LRU_CONV ** -0.5),
        'lru_conv_b': nrm(23, (DEPTH, LRU_WIDTH), 0.02),
        'lru_wa': nrm(24, (DEPTH, 2, LRU_BLOCKS, blk, blk), blk ** -0.5),
        'lru_ba': nrm(25, (DEPTH, 2, LRU_WIDTH), 0.02),
        'lru_wi': nrm(26, (DEPTH, 2, LRU_BLOCKS, blk, blk), blk ** -0.5),
        'lru_bi': nrm(27, (DEPTH, 2, LRU_WIDTH), 0.02),
        'lru_lam': lru_lam,
        'ffn_w_gu': nrm(29, (DEPTH, D, 2 * D_FF), D ** -0.5),
        'ffn_w_down': nrm(30, (DEPTH, D_FF, D), D_FF ** -0.5),
    }


def reference(x, c, ctx, c_ctx, ada_w, ada_b, g_mix_pre, g_mix_post, g_ffn_pre, g_ffn_post,
              w_in, w_out, hy_conv_w, hy_conv_b, hy_f_w1, hy_f_b1, hy_f_w2, hy_f_b2, hy_f_w3,
              hy_f_freq, hy_bias, na_rpb, lru_conv_w, lru_conv_b, lru_wa, lru_ba, lru_wi, lru_bi,
              lru_lam, ffn_w_gu, ffn_w_down):
    rows = x.shape[1] // GRID_W
    xc = ctx
    for l in range(DEPTH):
        with_ctx_out = l < DEPTH - 1
        sh_m, sc_m, gt_m, sh_f, sc_f, gt_f = [t[:, None, :] for t in modulation(c, ada_w[l], ada_b[l])]
        csh_m, csc_m, cgt_m, csh_f, csc_f, cgt_f = modulation(c_ctx, ada_w[l], ada_b[l])

        h = rmsnorm(x, g_mix_pre[l]) * (1.0 + sc_m) + sh_m
        hc = rmsnorm(xc, g_mix_pre[l]) * (1.0 + csc_m) + csh_m
        hy_filt = (hy_f_w1[l], hy_f_b1[l], hy_f_w2[l], hy_f_b2[l], hy_f_w3[l], hy_f_freq[l])
        lru_params = (lru_conv_w[l], lru_conv_b[l], lru_wa[l], lru_ba[l], lru_wi[l], lru_bi[l], lru_lam[l])
        y, yc = mixer(h, hc, rows, with_ctx_out, w_in[l], w_out[l], hy_conv_w[l], hy_conv_b[l],
                      hy_filt, hy_bias[l], na_rpb[l], lru_params)
        x = x + gt_m * rmsnorm(y, g_mix_post[l])

        h = rmsnorm(x, g_ffn_pre[l]) * (1.0 + sc_f) + sh_f
        x = x + gt_f * rmsnorm(swiglu(h, ffn_w_gu[l], ffn_w_down[l]), g_ffn_post[l])

        if with_ctx_out:
            xc = xc + cgt_m * rmsnorm(yc, g_mix_post[l])
            hc = rmsnorm(xc, g_ffn_pre[l]) * (1.0 + csc_f) + csh_f
            xc = xc + cgt_f * rmsnorm(swiglu(hc, ffn_w_gu[l], ffn_w_down[l]), g_ffn_post[l])
    return x
```

```python
import functools
import math

import jax
import jax.numpy as jnp
import numpy as np
from jax import lax
from jax.experimental import pallas as pl
from jax.experimental.pallas import tpu as pltpu

F32 = jnp.float32
BF16 = jnp.bfloat16

D_MODEL = 1024
DEPTH = 2
GRID_W = 64
HY_WIDTH = D_MODEL // 4
NA_HEAD_DIM = 64
NA_WIDTH = D_MODEL // 2
NA_HEADS = NA_WIDTH // NA_HEAD_DIM
LRU_WIDTH = D_MODEL // 4
LRU_BLOCKS = 4
IN_WIDTH = 3 * HY_WIDTH + 3 * NA_WIDTH + 2 * LRU_WIDTH
HY_ORDER = 2
HY_BANDS = 16
HY_FAST_DECAY = 0.3
HY_SLOW_DECAY = 1.5
HY_DECAY_TARGET = 1e-2
NA_WIN_ROWS = 8
NA_WIN_COLS = 16
NA_COL_BLOCK = 16
NA_KEY_COLS = 32
LRU_C = 8.0
D_FF = -(-8 * D_MODEL // (3 * 256)) * 256
RMS_EPS = 1e-6
NEG_INF = -1e30

_HY_END = 3 * HY_WIDTH
_NA_END = _HY_END + 3 * NA_WIDTH
_LRU_MID = _NA_END + LRU_WIDTH

HALO = 8
VMEM_LIMIT = 48 * 1024 * 1024


def _rms(x, g):
    return x * lax.rsqrt(jnp.mean(x * x, axis=-1, keepdims=True) + RMS_EPS) * g


def _mod_kernel(c_ref, w_ref, b_ref, o_ref):
    c = c_ref[...]
    s = c * jax.nn.sigmoid(c)
    o_ref[0] = jnp.dot(s, w_ref[0], preferred_element_type=F32,
                       precision=lax.Precision.HIGHEST) + b_ref[0]


def _modulation(cond, ada_w, ada_b):
    tn = 1536
    n = ada_w.shape[-1]
    return pl.pallas_call(
        _mod_kernel,
        out_shape=jax.ShapeDtypeStruct((DEPTH, 8, n), F32),
        grid=(DEPTH, n // tn),
        in_specs=[pl.BlockSpec((8, D_MODEL), lambda l, j: (0, 0)),
                  pl.BlockSpec((1, D_MODEL, tn), lambda l, j: (l, 0, j)),
                  pl.BlockSpec((1, 1, tn), lambda l, j: (l, 0, j))],
        out_specs=pl.BlockSpec((1, 8, tn), lambda l, j: (l, 0, j)),
        compiler_params=pltpu.CompilerParams(vmem_limit_bytes=VMEM_LIMIT),
        name="adaln_modulation",
    )(cond, ada_w, ada_b.reshape(DEPTH, 1, n))


def _inproj_kernel(xp_ref, xc_ref, xn_ref, sc_ref, sh_ref, g_ref, w_ref, hcw_ref, hcb_ref, lcw_ref, lcb_ref,
                   hv_ref, hx1_ref, hx2_ref, q_ref, k_ref, v_ref, lu_ref, lg_ref, pe_ref, *, tm):
    i = pl.program_id(1)
    last = pl.num_programs(1) - 1
    g = g_ref[...]
    sc1 = 1.0 + sc_ref[0]
    sh = sh_ref[0]

    def norm_mod(xv):
        return _rms(xv, g) * sc1 + sh

    hp = norm_mod(xp_ref[0]) * (i > 0).astype(F32)
    hn = norm_mod(xn_ref[0]) * (i < last).astype(F32)
    he = jnp.concatenate([hp, norm_mod(xc_ref[0]), hn], axis=0).astype(BF16)

    pe_ref[:, 0:_HY_END] = jnp.dot(he, w_ref[:, 0:_HY_END], preferred_element_type=F32)
    pe_ref[:, _HY_END:] = jnp.dot(he, w_ref[:, _NA_END:_LRU_MID], preferred_element_type=F32)
    hc = he[HALO:HALO + tm]
    qkv = jnp.dot(hc, w_ref[:, _HY_END:_NA_END], preferred_element_type=F32)
    q_ref[0] = (qkv[:, 0:NA_WIDTH] * (NA_HEAD_DIM ** -0.5)).astype(BF16)
    k_ref[0] = qkv[:, NA_WIDTH:2 * NA_WIDTH].astype(BF16)
    v_ref[0] = qkv[:, 2 * NA_WIDTH:].astype(BF16)
    lg_ref[0] = jnp.dot(hc, w_ref[:, _LRU_MID:], preferred_element_type=F32)

    u = hcb_ref[...]
    for kk in range(3):
        u = u + hcw_ref[kk:kk + 1, :] * pe_ref[pl.ds(HALO - 1 + kk, tm), 0:_HY_END]
    hv_ref[0] = u[:, 0:HY_WIDTH]
    hx1_ref[0] = u[:, HY_WIDTH:2 * HY_WIDTH]
    hx2_ref[0] = u[:, 2 * HY_WIDTH:]
    ul = lcb_ref[...]
    for kk in range(4):
        ul = ul + lcw_ref[kk:kk + 1, :] * pe_ref[pl.ds(HALO - 2 + kk, tm), _HY_END:]
    lu_ref[0] = ul


def _inproj(x, sc, sh, g, w_bf, hcw, hcb, lcw, lcb, *, tm):
    B, L, D = x.shape
    nb = tm // HALO
    nh = L // HALO
    tok = lambda w, dt: jax.ShapeDtypeStruct((B, L, w), dt)
    tspec = lambda w: pl.BlockSpec((1, tm, w), lambda b, i: (b, i, 0))
    full = lambda a: pl.BlockSpec(a.shape, lambda b, i: (0,) * a.ndim)
    vec = pl.BlockSpec((1, 1, D), lambda b, i: (b, 0, 0))
    return pl.pallas_call(
        functools.partial(_inproj_kernel, tm=tm),
        out_shape=[tok(HY_WIDTH, F32)] * 3 + [tok(NA_WIDTH, BF16)] * 3 + [tok(LRU_WIDTH, F32)] * 2,
        grid=(B, L // tm),
        in_specs=[pl.BlockSpec((1, HALO, D), lambda b, i: (b, jnp.maximum(i * nb - 1, 0), 0)),
                  pl.BlockSpec((1, tm, D), lambda b, i: (b, i, 0)),
                  pl.BlockSpec((1, HALO, D), lambda b, i: (b, jnp.minimum((i + 1) * nb, nh - 1), 0)),
                  vec, vec, full(g), full(w_bf), full(hcw), full(hcb), full(lcw), full(lcb)],
        out_specs=[tspec(HY_WIDTH)] * 3 + [tspec(NA_WIDTH)] * 3 + [tspec(LRU_WIDTH)] * 2,
        scratch_shapes=[pltpu.VMEM((tm + 2 * HALO, _HY_END + LRU_WIDTH), F32)],
        compiler_params=pltpu.CompilerParams(vmem_limit_bytes=VMEM_LIMIT),
        name="inproj",
    )(x, x, x, sc, sh, g, w_bf, hcw, hcb, lcw, lcb)


def _outproj_kernel(yh_ref, yn_ref, yl_ref, w_ref, x_ref, g_ref, gt_ref, o_ref):
    y = jnp.dot(yh_ref[0], w_ref[0:HY_WIDTH], preferred_element_type=F32)
    y = y + jnp.dot(yn_ref[0], w_ref[HY_WIDTH:HY_WIDTH + NA_WIDTH], preferred_element_type=F32)
    y = y + jnp.dot(yl_ref[0], w_ref[HY_WIDTH + NA_WIDTH:], preferred_element_type=F32)
    o_ref[0] = x_ref[0] + gt_ref[0] * _rms(y, g_ref[...])


def _outproj(yh, yn, yl, w_bf, x, g, gt, *, tm):
    B, L, D = x.shape
    tspec = lambda w: pl.BlockSpec((1, tm, w), lambda b, i: (b, i, 0))
    full = lambda a: pl.BlockSpec(a.shape, lambda b, i: (0,) * a.ndim)
    return pl.pallas_call(
        _outproj_kernel,
        out_shape=jax.ShapeDtypeStruct((B, L, D), F32),
        grid=(B, L // tm),
        in_specs=[tspec(HY_WIDTH), tspec(NA_WIDTH), tspec(LRU_WIDTH), full(w_bf), tspec(D), full(g),
                  pl.BlockSpec((1, 1, D), lambda b, i: (b, 0, 0))],
        out_specs=tspec(D),
        compiler_params=pltpu.CompilerParams(vmem_limit_bytes=VMEM_LIMIT),
        name="outproj",
    )(yh, yn, yl, w_bf, x, g, gt)


FF_CHUNK = 256


def _ffn_kernel(x_ref, sc_ref, sh_ref, gt_ref, gpre_ref, gpost_ref, wg_ref, wu_ref, wd_ref, o_ref):
    x = x_ref[0]
    h = (_rms(x, gpre_ref[...]) * (1.0 + sc_ref[0]) + sh_ref[0]).astype(BF16)
    acc = jnp.zeros(x.shape, F32)
    for j in range(D_FF // FF_CHUNK):
        g = jnp.dot(h, wg_ref[j], preferred_element_type=F32)
        u = jnp.dot(h, wu_ref[j], preferred_element_type=F32)
        a = (g * jax.nn.sigmoid(g) * u).astype(BF16)
        acc = acc + jnp.dot(a, wd_ref[j], preferred_element_type=F32)
    o_ref[0] = x + gt_ref[0] * _rms(acc, gpost_ref[...])


def _ffn(x, sc, sh, gt, gpre, gpost, wg, wu, wd, *, tm):
    B, L, D = x.shape
    tspec = pl.BlockSpec((1, tm, D), lambda b, i: (b, i, 0))
    vec = pl.BlockSpec((1, 1, D), lambda b, i: (b, 0, 0))
    full = lambda a: pl.BlockSpec(a.shape, lambda b, i: (0,) * a.ndim)
    res = lambda a: pl.BlockSpec(a.shape, lambda b, i: (0,) * a.ndim, pipeline_mode=pl.Buffered(1))
    return pl.pallas_call(
        _ffn_kernel,
        out_shape=jax.ShapeDtypeStruct((B, L, D), F32),
        grid=(B, L // tm),
        in_specs=[tspec, vec, vec, vec, full(gpre), full(gpost), res(wg), res(wu), res(wd)],
        out_specs=tspec,
        compiler_params=pltpu.CompilerParams(vmem_limit_bytes=VMEM_LIMIT),
        name="ffn",
    )(x, sc, sh, gt, gpre, gpost, wg, wu, wd)


def _hyena_filters(L, f_w1, f_b1, f_w2, f_b2, f_w3, f_freq):
    t = jnp.arange(L, dtype=F32) / L
    bands = jnp.arange(1, HY_BANDS + 1, dtype=F32)
    ang = 2.0 * math.pi * t[:, None] * bands[None, :]
    z = jnp.concatenate([t[:, None], jnp.cos(ang), jnp.sin(ang)], axis=-1)
    h = jnp.sin(f_freq[0] * (z @ f_w1 + f_b1))
    h = jnp.sin(f_freq[1] * (h @ f_w2 + f_b2))
    k = (h @ f_w3).astype(F32).reshape(L, HY_ORDER, 2, HY_WIDTH)
    deltas = jnp.abs(jnp.linspace(math.log(HY_DECAY_TARGET) / HY_FAST_DECAY,
                                  math.log(HY_DECAY_TARGET) / HY_SLOW_DECAY, HY_WIDTH, dtype=F32))
    k = k * jnp.exp(-t[:, None, None, None] * deltas)
    kf = k[:, :, 0]
    kb = k[:, :, 1]
    k_circ = jnp.concatenate([kf, jnp.zeros_like(kf[:1]), kb[:0:-1]], axis=0)
    k_circ = k_circ / jnp.sum(jnp.abs(k_circ), axis=0, keepdims=True)
    return jnp.fft.rfft(k_circ, axis=0)


def _fft_conv(u, k_f):
    L = u.shape[1]
    U = jnp.fft.rfft(u, n=2 * L, axis=1)
    return jnp.fft.irfft(U * k_f[None], n=2 * L, axis=1)[:, :L]


def _hyena_jax(v, x1, x2, filt, hy_bias):
    k_f = _hyena_filters(v.shape[1], *filt)
    z = v
    for n, gate in enumerate((x1, x2)):
        z = gate * (_fft_conv(z, k_f[:, n]) + hy_bias[n] * z)
    return z


def _na_static(grid_w):
    n_cb = grid_w // NA_COL_BLOCK
    qc = np.arange(grid_w).reshape(n_cb, NA_COL_BLOCK)
    kc0 = np.clip(np.arange(n_cb) * NA_COL_BLOCK - NA_WIN_COLS // 2, 0, grid_w - NA_KEY_COLS)
    kc = kc0[:, None] + np.arange(NA_KEY_COLS)[None, :]
    start = np.clip(qc - NA_WIN_COLS // 2, 0, grid_w - NA_WIN_COLS)
    valid = (kc[:, None, :] >= start[:, :, None]) & (kc[:, None, :] < start[:, :, None] + NA_WIN_COLS)
    dc = np.clip(kc[:, None, :] - qc[:, :, None], -(NA_WIN_COLS - 1), NA_WIN_COLS - 1) + NA_WIN_COLS - 1
    return kc, valid, dc


def _na_jax(q, k, v, k_ctx, v_ctx, rpb, rows):
    B, L, H, dh = q.shape
    kr = min(NA_WIN_ROWS, rows)
    kc, valid, dc = _na_static(GRID_W)
    n_cb = kc.shape[0]
    n_loc = kr * NA_KEY_COLS
    valid = jnp.asarray(valid)[:, :, None, :]
    qg = q.reshape(B, rows, n_cb, NA_COL_BLOCK, H, dh).transpose(1, 0, 4, 2, 3, 5)
    kg = k.reshape(B, rows, GRID_W, H, dh).transpose(0, 3, 1, 2, 4)
    vg = v.reshape(B, rows, GRID_W, H, dh).transpose(0, 3, 1, 2, 4)
    kcx = k_ctx.transpose(0, 2, 1, 3)
    vcx = v_ctx.transpose(0, 2, 1, 3)

    def row_step(args):
        q_r, r = args
        rs = jnp.clip(r - kr // 2, 0, rows - kr)
        k_blk = lax.dynamic_slice_in_dim(kg, rs, kr, axis=2)[:, :, :, kc]
        v_blk = lax.dynamic_slice_in_dim(vg, rs, kr, axis=2)[:, :, :, kc]
        s_loc = jnp.einsum('bhnqd,bhrnkd->bhnqrk', q_r, k_blk).astype(F32)
        dr = rs + jnp.arange(kr) - r + NA_WIN_ROWS - 1
        bias = rpb[:, dr][:, :, dc].transpose(0, 2, 3, 1, 4)
        s_loc = jnp.where(valid, s_loc + bias[None].astype(F32), NEG_INF)
        s_ctx = jnp.einsum('bhnqd,bhcd->bhnqc', q_r, kcx).astype(F32)
        s = jnp.concatenate([s_loc.reshape(B, H, n_cb, NA_COL_BLOCK, n_loc), s_ctx], axis=-1)
        p = jax.nn.softmax(s, axis=-1)
        p_loc = p[..., :n_loc].reshape(B, H, n_cb, NA_COL_BLOCK, kr, NA_KEY_COLS)
        p_ctx = p[..., n_loc:]
        return (jnp.einsum('bhnqrk,bhrnkd->bhnqd', p_loc, v_blk)
                + jnp.einsum('bhnqc,bhcd->bhnqd', p_ctx, vcx))

    o = lax.map(row_step, (qg, jnp.arange(rows)))
    return o.transpose(1, 0, 3, 4, 2, 5).reshape(B, L, H * dh)


def _attn_ctx_jax(q, k, v):
    B, Lc, H, dh = q.shape
    s = jnp.einsum('bqhd,bkhd->bhqk', q, k).astype(F32)
    p = jax.nn.softmax(s, axis=-1)
    return jnp.einsum('bhqk,bkhd->bqhd', p, v).reshape(B, Lc, H * dh)


def _linear_scan(a, b, h0, reverse):
    idx = -1 if reverse else 0
    b = b.at[:, idx].add(a[:, idx] * h0)

    def combine(e1, e2):
        a1, b1 = e1
        a2, b2 = e2
        return a1 * a2, a2 * b1 + b2

    _, h = lax.associative_scan(combine, (a, b), reverse=reverse, axis=1)
    return h


def _rglru_coeffs(u, wa, ba, wi, bi, lam):
    B, L, C = u.shape
    ub = u.reshape(B, L, LRU_BLOCKS, C // LRU_BLOCKS)
    r = jax.nn.sigmoid(jnp.einsum('blnc,ncd->blnd', ub, wa).reshape(B, L, C) + ba)
    i = jax.nn.sigmoid(jnp.einsum('blnc,ncd->blnd', ub, wi).reshape(B, L, C) + bi)
    log_a = -LRU_C * r * jax.nn.softplus(-lam)
    a = jnp.exp(log_a)
    b = jnp.sqrt(-jnp.expm1(2.0 * log_a)) * (i * u)
    return a, b


def _lru_jax(u, xg, u_c, xg_c, wa, ba, wi, bi, lam, with_ctx_out):
    h_lat = 0.0
    h_ctx = 0.0
    for d, rev in enumerate((False, True)):
        a_c, b_c = _rglru_coeffs(u_c, wa[d], ba[d], wi[d], bi[d], lam[d])
        hc = _linear_scan(a_c, b_c, jnp.zeros_like(b_c[:, 0]), rev)
        h_end = hc[:, 0] if rev else hc[:, -1]
        a_l, b_l = _rglru_coeffs(u, wa[d], ba[d], wi[d], bi[d], lam[d])
        h_lat = h_lat + _linear_scan(a_l, b_l, h_end, rev)
        h_ctx = h_ctx + hc
    y = h_lat * jax.nn.gelu(xg)
    if not with_ctx_out:
        return y, None
    return y, h_ctx * jax.nn.gelu(xg_c)


def kernel(x, c, ctx, c_ctx, ada_w, ada_b, g_mix_pre, g_mix_post, g_ffn_pre, g_ffn_post, w_in, w_out, hy_conv_w,
           hy_conv_b, hy_f_w1, hy_f_b1, hy_f_w2, hy_f_b2, hy_f_w3, hy_f_freq, hy_bias, na_rpb, lru_conv_w,
           lru_conv_b, lru_wa, lru_ba, lru_wi, lru_bi, lru_lam, ffn_w_gu, ffn_w_down):
    B, L, D = x.shape
    Lc = ctx.shape[1]
    rows = L // GRID_W
    tm = 512

    cond = jnp.zeros((8, D), F32).at[0:B].set(c).at[B].set(c_ctx)
    mods = _modulation(cond, ada_w, ada_b)

    xc = ctx
    for l in range(DEPTH):
        with_ctx_out = l < DEPTH - 1
        m = mods[l].reshape(8, 6, D)
        lat = [m[0:B, j][:, None, :] for j in range(6)]
        cx = [jnp.broadcast_to(m[B, j][None, None, :], (B, 1, D)) for j in range(6)]
        row = lambda a: a.reshape(1, -1)

        w_in_bf = w_in[l].astype(BF16)
        w_out_bf = w_out[l].astype(BF16)
        nchunk = D_FF // FF_CHUNK
        wg = ffn_w_gu[l][:, :D_FF].astype(BF16).reshape(D, nchunk, FF_CHUNK).transpose(1, 0, 2)
        wu = ffn_w_gu[l][:, D_FF:].astype(BF16).reshape(D, nchunk, FF_CHUNK).transpose(1, 0, 2)
        wd = ffn_w_down[l].astype(BF16).reshape(nchunk, FF_CHUNK, D)

        conv_args = (hy_conv_w[l], row(hy_conv_b[l]), lru_conv_w[l], row(lru_conv_b[l]))
        hv, hx1, hx2, q, k, v, lu, lg = _inproj(x, lat[1], lat[0], row(g_mix_pre[l]), w_in_bf, *conv_args, tm=tm)
        cv, cx1, cx2, cq, ck, cvv, clu, clg = _inproj(xc, cx[1], cx[0], row(g_mix_pre[l]), w_in_bf, *conv_args,
                                                      tm=Lc)

        filt = (hy_f_w1[l], hy_f_b1[l], hy_f_w2[l], hy_f_b2[l], hy_f_w3[l], hy_f_freq[l])
        y_hy = _hyena_jax(hv, hx1, hx2, filt, hy_bias[l])
        hd = lambda a: a.astype(F32).reshape(a.shape[0], a.shape[1], NA_HEADS, NA_HEAD_DIM)
        y_na = _na_jax(hd(q), hd(k), hd(v), hd(ck), hd(cvv), na_rpb[l], rows)
        y_lru, yc_lru = _lru_jax(lu, lg, clu, clg, lru_wa[l], lru_ba[l], lru_wi[l], lru_bi[l], lru_lam[l],
                                 with_ctx_out)

        x = _outproj(y_hy.astype(BF16), y_na.astype(BF16), y_lru.astype(BF16), w_out_bf, x,
                     row(g_mix_post[l]), lat[2], tm=tm)
        x = _ffn(x, lat[4], lat[3], lat[5], row(g_ffn_pre[l]), row(g_ffn_post[l]), wg, wu, wd, tm=tm)

        if with_ctx_out:
            yc_hy = _hyena_jax(cv, cx1, cx2, filt, hy_bias[l])
            yc_na = _attn_ctx_jax(hd(cq), hd(ck), hd(cvv))
            xc = _outproj(yc_hy.astype(BF16), yc_na.astype(BF16), yc_lru.astype(BF16), w_out_bf, xc,
                          row(g_mix_post[l]), cx[2], tm=Lc)
            xc = _ffn(xc, cx[4], cx[3], cx[5], row(g_ffn_pre[l]), row(g_ffn_post[l]), wg, wu, wd, tm=Lc)
    return x
```

```python
import functools
import math

import jax
import jax.numpy as jnp
import numpy as np
from jax import lax
from jax.experimental import pallas as pl
from jax.experimental.pallas import tpu as pltpu

F32 = jnp.float32
BF16 = jnp.bfloat16

D_MODEL = 1024
DEPTH = 2
GRID_W = 64
HY_WIDTH = D_MODEL // 4
NA_HEAD_DIM = 64
NA_WIDTH = D_MODEL // 2
NA_HEADS = NA_WIDTH // NA_HEAD_DIM
LRU_WIDTH = D_MODEL // 4
LRU_BLOCKS = 4
IN_WIDTH = 3 * HY_WIDTH + 3 * NA_WIDTH + 2 * LRU_WIDTH
HY_ORDER = 2
HY_BANDS = 16
HY_FAST_DECAY = 0.3
HY_SLOW_DECAY = 1.5
HY_DECAY_TARGET = 1e-2
NA_WIN_ROWS = 8
NA_WIN_COLS = 16
LRU_C = 8.0
D_FF = -(-8 * D_MODEL // (3 * 256)) * 256
RMS_EPS = 1e-6
NEG_INF = -1e30

_HY_END = 3 * HY_WIDTH
_NA_END = _HY_END + 3 * NA_WIDTH
_LRU_MID = _NA_END + LRU_WIDTH

HALO = 8
SUBLANES = 8
VMEM_LIMIT = 48 * 1024 * 1024


def _rms(x, g):
    return x * lax.rsqrt(jnp.mean(x * x, axis=-1, keepdims=True) + RMS_EPS) * g


def _mod_kernel(c_ref, w_ref, b_ref, o_ref):
    c = c_ref[...]
    s = c * jax.nn.sigmoid(c)
    o_ref[0] = jnp.dot(s, w_ref[0], preferred_element_type=F32,
                       precision=lax.Precision.HIGHEST) + b_ref[0]


def _modulation(cond, ada_w, ada_b):
    tn = 1536
    n = ada_w.shape[-1]
    return pl.pallas_call(
        _mod_kernel,
        out_shape=jax.ShapeDtypeStruct((DEPTH, 8, n), F32),
        grid=(DEPTH, n // tn),
        in_specs=[pl.BlockSpec((8, D_MODEL), lambda l, j: (0, 0)),
                  pl.BlockSpec((1, D_MODEL, tn), lambda l, j: (l, 0, j)),
                  pl.BlockSpec((1, 1, tn), lambda l, j: (l, 0, j))],
        out_specs=pl.BlockSpec((1, 8, tn), lambda l, j: (l, 0, j)),
        compiler_params=pltpu.CompilerParams(vmem_limit_bytes=VMEM_LIMIT),
        name="adaln_modulation",
    )(cond, ada_w, ada_b.reshape(DEPTH, 1, n))


def _inproj_kernel(xp_ref, xc_ref, xn_ref, sc_ref, sh_ref, g_ref, w_ref, hcw_ref, hcb_ref, lcw_ref, lcb_ref,
                   hv_ref, hx1_ref, hx2_ref, q_ref, k_ref, v_ref, lu_ref, lg_ref, pe_ref, *, tm):
    i = pl.program_id(1)
    last = pl.num_programs(1) - 1
    g = g_ref[...]
    sc1 = 1.0 + sc_ref[0]
    sh = sh_ref[0]

    def norm_mod(xv):
        return _rms(xv, g) * sc1 + sh

    hp = norm_mod(xp_ref[0]) * (i > 0).astype(F32)
    hn = norm_mod(xn_ref[0]) * (i < last).astype(F32)
    he = jnp.concatenate([hp, norm_mod(xc_ref[0]), hn], axis=0).astype(BF16)

    pe_ref[:, 0:_HY_END] = jnp.dot(he, w_ref[:, 0:_HY_END], preferred_element_type=F32)
    pe_ref[:, _HY_END:] = jnp.dot(he, w_ref[:, _NA_END:_LRU_MID], preferred_element_type=F32)
    hc = he[HALO:HALO + tm]
    qkv = jnp.dot(hc, w_ref[:, _HY_END:_NA_END], preferred_element_type=F32)
    q_ref[0] = (qkv[:, 0:NA_WIDTH] * (NA_HEAD_DIM ** -0.5)).astype(BF16)
    k_ref[0] = qkv[:, NA_WIDTH:2 * NA_WIDTH].astype(BF16)
    v_ref[0] = qkv[:, 2 * NA_WIDTH:].astype(BF16)
    lg_ref[0] = jnp.dot(hc, w_ref[:, _LRU_MID:], preferred_element_type=F32)

    u = hcb_ref[...]
    for kk in range(3):
        u = u + hcw_ref[kk:kk + 1, :] * pe_ref[pl.ds(HALO - 1 + kk, tm), 0:_HY_END]
    hv_ref[0] = u[:, 0:HY_WIDTH]
    hx1_ref[0] = u[:, HY_WIDTH:2 * HY_WIDTH]
    hx2_ref[0] = u[:, 2 * HY_WIDTH:]
    ul = lcb_ref[...]
    for kk in range(4):
        ul = ul + lcw_ref[kk:kk + 1, :] * pe_ref[pl.ds(HALO - 2 + kk, tm), _HY_END:]
    lu_ref[0] = ul


def _inproj(x, sc, sh, g, w_bf, hcw, hcb, lcw, lcb, *, tm):
    B, L, D = x.shape
    nb = tm // HALO
    nh = L // HALO
    tok = lambda w, dt: jax.ShapeDtypeStruct((B, L, w), dt)
    tspec = lambda w: pl.BlockSpec((1, tm, w), lambda b, i: (b, i, 0))
    full = lambda a: pl.BlockSpec(a.shape, lambda b, i: (0,) * a.ndim)
    vec = pl.BlockSpec((1, 1, D), lambda b, i: (b, 0, 0))
    return pl.pallas_call(
        functools.partial(_inproj_kernel, tm=tm),
        out_shape=[tok(HY_WIDTH, F32)] * 3 + [tok(NA_WIDTH, BF16)] * 3 + [tok(LRU_WIDTH, F32)] * 2,
        grid=(B, L // tm),
        in_specs=[pl.BlockSpec((1, HALO, D), lambda b, i: (b, jnp.maximum(i * nb - 1, 0), 0)),
                  pl.BlockSpec((1, tm, D), lambda b, i: (b, i, 0)),
                  pl.BlockSpec((1, HALO, D), lambda b, i: (b, jnp.minimum((i + 1) * nb, nh - 1), 0)),
                  vec, vec, full(g), full(w_bf), full(hcw), full(hcb), full(lcw), full(lcb)],
        out_specs=[tspec(HY_WIDTH)] * 3 + [tspec(NA_WIDTH)] * 3 + [tspec(LRU_WIDTH)] * 2,
        scratch_shapes=[pltpu.VMEM((tm + 2 * HALO, _HY_END + LRU_WIDTH), F32)],
        compiler_params=pltpu.CompilerParams(vmem_limit_bytes=VMEM_LIMIT),
        name="inproj",
    )(x, x, x, sc, sh, g, w_bf, hcw, hcb, lcw, lcb)


def _outproj_kernel(yh_ref, yn_ref, yl_ref, w_ref, x_ref, g_ref, gt_ref, o_ref):
    y = jnp.dot(yh_ref[0], w_ref[0:HY_WIDTH], preferred_element_type=F32)
    y = y + jnp.dot(yn_ref[0], w_ref[HY_WIDTH:HY_WIDTH + NA_WIDTH], preferred_element_type=F32)
    y = y + jnp.dot(yl_ref[0], w_ref[HY_WIDTH + NA_WIDTH:], preferred_element_type=F32)
    o_ref[0] = x_ref[0] + gt_ref[0] * _rms(y, g_ref[...])


def _outproj(yh, yn, yl, w_bf, x, g, gt, *, tm):
    B, L, D = x.shape
    tspec = lambda w: pl.BlockSpec((1, tm, w), lambda b, i: (b, i, 0))
    full = lambda a: pl.BlockSpec(a.shape, lambda b, i: (0,) * a.ndim)
    return pl.pallas_call(
        _outproj_kernel,
        out_shape=jax.ShapeDtypeStruct((B, L, D), F32),
        grid=(B, L // tm),
        in_specs=[tspec(HY_WIDTH), tspec(NA_WIDTH), tspec(LRU_WIDTH), full(w_bf), tspec(D), full(g),
                  pl.BlockSpec((1, 1, D), lambda b, i: (b, 0, 0))],
        out_specs=tspec(D),
        compiler_params=pltpu.CompilerParams(vmem_limit_bytes=VMEM_LIMIT),
        name="outproj",
    )(yh, yn, yl, w_bf, x, g, gt)


FF_CHUNK = 256


def _ffn_kernel(x_ref, sc_ref, sh_ref, gt_ref, gpre_ref, gpost_ref, wg_ref, wu_ref, wd_ref, o_ref):
    x = x_ref[0]
    h = (_rms(x, gpre_ref[...]) * (1.0 + sc_ref[0]) + sh_ref[0]).astype(BF16)
    acc = jnp.zeros(x.shape, F32)
    for j in range(D_FF // FF_CHUNK):
        g = jnp.dot(h, wg_ref[j], preferred_element_type=F32)
        u = jnp.dot(h, wu_ref[j], preferred_element_type=F32)
        a = (g * jax.nn.sigmoid(g) * u).astype(BF16)
        acc = acc + jnp.dot(a, wd_ref[j], preferred_element_type=F32)
    o_ref[0] = x + gt_ref[0] * _rms(acc, gpost_ref[...])


def _ffn(x, sc, sh, gt, gpre, gpost, wg, wu, wd, *, tm):
    B, L, D = x.shape
    tspec = pl.BlockSpec((1, tm, D), lambda b, i: (b, i, 0))
    vec = pl.BlockSpec((1, 1, D), lambda b, i: (b, 0, 0))
    full = lambda a: pl.BlockSpec(a.shape, lambda b, i: (0,) * a.ndim)
    res = lambda a: pl.BlockSpec(a.shape, lambda b, i: (0,) * a.ndim, pipeline_mode=pl.Buffered(1))
    return pl.pallas_call(
        _ffn_kernel,
        out_shape=jax.ShapeDtypeStruct((B, L, D), F32),
        grid=(B, L // tm),
        in_specs=[tspec, vec, vec, vec, full(gpre), full(gpost), res(wg), res(wu), res(wd)],
        out_specs=tspec,
        compiler_params=pltpu.CompilerParams(vmem_limit_bytes=VMEM_LIMIT),
        name="ffn",
    )(x, sc, sh, gt, gpre, gpost, wg, wu, wd)


def _hyena_filters(L, f_w1, f_b1, f_w2, f_b2, f_w3, f_freq):
    t = jnp.arange(L, dtype=F32) / L
    bands = jnp.arange(1, HY_BANDS + 1, dtype=F32)
    ang = 2.0 * math.pi * t[:, None] * bands[None, :]
    z = jnp.concatenate([t[:, None], jnp.cos(ang), jnp.sin(ang)], axis=-1)
    h = jnp.sin(f_freq[0] * (z @ f_w1 + f_b1))
    h = jnp.sin(f_freq[1] * (h @ f_w2 + f_b2))
    k = (h @ f_w3).astype(F32).reshape(L, HY_ORDER, 2, HY_WIDTH)
    deltas = jnp.abs(jnp.linspace(math.log(HY_DECAY_TARGET) / HY_FAST_DECAY,
                                  math.log(HY_DECAY_TARGET) / HY_SLOW_DECAY, HY_WIDTH, dtype=F32))
    k = k * jnp.exp(-t[:, None, None, None] * deltas)
    kf = k[:, :, 0]
    kb = k[:, :, 1]
    k_circ = jnp.concatenate([kf, jnp.zeros_like(kf[:1]), kb[:0:-1]], axis=0)
    k_circ = k_circ / jnp.sum(jnp.abs(k_circ), axis=0, keepdims=True)
    return jnp.fft.rfft(k_circ, axis=0)


def _fft_conv(u, k_f):
    L = u.shape[1]
    U = jnp.fft.rfft(u, n=2 * L, axis=1)
    return jnp.fft.irfft(U * k_f[None], n=2 * L, axis=1)[:, :L]


def _hyena_jax(v, x1, x2, filt, hy_bias):
    k_f = _hyena_filters(v.shape[1], *filt)
    z = v
    for n, gate in enumerate((x1, x2)):
        z = gate * (_fft_conv(z, k_f[:, n]) + hy_bias[n] * z)
    return z


NA_ROWS_PER_STEP = 8
_NT = (((1,), (1,)), ((), ()))


def _na_bias_table(rpb):
    qc = np.arange(GRID_W)[:, None]
    kc = np.arange(GRID_W)[None, :]
    start = np.clip(qc - NA_WIN_COLS // 2, 0, GRID_W - NA_WIN_COLS)
    valid = (kc >= start) & (kc < start + NA_WIN_COLS)
    dc = np.clip(kc - qc, -(NA_WIN_COLS - 1), NA_WIN_COLS - 1) + NA_WIN_COLS - 1
    full = jnp.where(jnp.asarray(valid)[None, None], rpb[:, :, dc], NEG_INF)
    dj = np.arange(NA_WIN_ROWS)[:, None] + np.arange(NA_WIN_ROWS)[None, :]
    t = full[:, dj]
    t = t.transpose(0, 1, 3, 2, 4).reshape(NA_HEADS // 2, 2 * NA_WIN_ROWS, GRID_W, NA_WIN_ROWS * GRID_W)
    return t.astype(F32)


def _softmax_pv(s_loc, s_ctx, v_loc, v_ctx):
    m = jnp.maximum(jnp.max(s_loc, axis=1, keepdims=True), jnp.max(s_ctx, axis=1, keepdims=True))
    p_loc = jnp.exp(s_loc - m)
    p_ctx = jnp.exp(s_ctx - m)
    l = jnp.sum(p_loc, axis=1, keepdims=True) + jnp.sum(p_ctx, axis=1, keepdims=True)
    o = jnp.dot(p_loc.astype(BF16), v_loc, preferred_element_type=F32)
    o = o + jnp.dot(p_ctx.astype(BF16), v_ctx, preferred_element_type=F32)
    return o / l


def _na_kernel(q_ref, kp_ref, kc_ref, kn_ref, vp_ref, vc_ref, vn_ref, ck_ref, cv_ref, bias_ref, o_ref,
               wk_ref, wv_ref):
    i = pl.program_id(2)
    last = pl.num_programs(2) - 1
    blk = NA_ROWS_PER_STEP * GRID_W
    for n, (kr, vr) in enumerate(((kp_ref, vp_ref), (kc_ref, vc_ref), (kn_ref, vn_ref))):
        wk_ref[n * blk:(n + 1) * blk] = kr[0]
        wv_ref[n * blk:(n + 1) * blk] = vr[0]
    first_head = lax.broadcasted_iota(jnp.int32, (GRID_W, 2 * NA_HEAD_DIM), 1) < NA_HEAD_DIM
    ck = ck_ref[0]
    cv = cv_ref[0]
    half = NA_WIN_ROWS // 2
    for j in range(NA_ROWS_PER_STEP):
        off = jnp.where(i == 0, max(j + half, NA_ROWS_PER_STEP),
                        jnp.where(i == last, min(j + half, NA_ROWS_PER_STEP), j + half))
        d = off - j - 1
        start = pl.multiple_of(off * GRID_W, GRID_W)
        kw = wk_ref[pl.ds(start, NA_WIN_ROWS * GRID_W), :]
        vw = wv_ref[pl.ds(start, NA_WIN_ROWS * GRID_W), :]
        qj = q_ref[0, j * GRID_W:(j + 1) * GRID_W, :]
        outs = []
        for h in range(2):
            qm = jnp.where(first_head if h == 0 else jnp.logical_not(first_head), qj, jnp.zeros_like(qj))
            s_loc = lax.dot_general(qm, kw, _NT, preferred_element_type=F32) + bias_ref[0, h * NA_WIN_ROWS + d]
            s_ctx = lax.dot_general(qm, ck, _NT, preferred_element_type=F32)
            outs.append(_softmax_pv(s_loc, s_ctx, vw, cv))
        o_ref[0, j * GRID_W:(j + 1) * GRID_W, :] = jnp.where(first_head, outs[0], outs[1]).astype(o_ref.dtype)


def _na(q, k, v, ck, cv, bias):
    B, L, W = q.shape
    Lc = ck.shape[1]
    blk = NA_ROWS_PER_STEP * GRID_W
    nblk = L // blk
    assert NA_ROWS_PER_STEP == NA_WIN_ROWS and nblk >= 2
    pw = 2 * NA_HEAD_DIM
    cur = pl.BlockSpec((1, blk, pw), lambda b, h, i: (b, i, h))
    prev = pl.BlockSpec((1, blk, pw), lambda b, h, i: (b, jnp.maximum(i - 1, 0), h))
    nxt = pl.BlockSpec((1, blk, pw), lambda b, h, i: (b, jnp.minimum(i + 1, nblk - 1), h))
    cspec = pl.BlockSpec((1, Lc, pw), lambda b, h, i: (b, 0, h))
    bspec = pl.BlockSpec((1,) + bias.shape[1:], lambda b, h, i: (h, 0, 0, 0))
    return pl.pallas_call(
        _na_kernel,
        out_shape=jax.ShapeDtypeStruct((B, L, W), BF16),
        grid=(B, W // pw, nblk),
        in_specs=[cur, prev, cur, nxt, prev, cur, nxt, cspec, cspec, bspec],
        out_specs=cur,
        scratch_shapes=[pltpu.VMEM((3 * blk, pw), BF16), pltpu.VMEM((3 * blk, pw), BF16)],
        compiler_params=pltpu.CompilerParams(vmem_limit_bytes=VMEM_LIMIT),
        name="nattn",
    )(q, k, k, k, v, v, v, ck, cv, bias)


def _ctx_attn_kernel(q_ref, k_ref, v_ref, o_ref):
    q = q_ref[0]
    k = k_ref[0]
    v = v_ref[0]
    first_head = lax.broadcasted_iota(jnp.int32, q.shape, 1) < NA_HEAD_DIM
    outs = []
    for h in range(2):
        qm = jnp.where(first_head if h == 0 else jnp.logical_not(first_head), q, jnp.zeros_like(q))
        s = lax.dot_general(qm, k, _NT, preferred_element_type=F32)
        p = jnp.exp(s - jnp.max(s, axis=1, keepdims=True))
        o = jnp.dot(p.astype(BF16), v, preferred_element_type=F32)
        outs.append(o / jnp.sum(p, axis=1, keepdims=True))
    o_ref[0] = jnp.where(first_head, outs[0], outs[1]).astype(o_ref.dtype)


def _ctx_attn(q, k, v):
    B, Lc, W = q.shape
    pw = 2 * NA_HEAD_DIM
    spec = pl.BlockSpec((1, Lc, pw), lambda b, h: (b, 0, h))
    return pl.pallas_call(
        _ctx_attn_kernel,
        out_shape=jax.ShapeDtypeStruct((B, Lc, W), BF16),
        grid=(B, W // pw),
        in_specs=[spec, spec, spec],
        out_specs=spec,
        name="ctx_attn",
    )(q, k, v)


LRU_CHUNK = 512


def _lru_gate_weights(wa, ba, wi, bi, lam):
    def bd(w):
        return jax.scipy.linalg.block_diag(*[w[n] for n in range(LRU_BLOCKS)])
    wg = jnp.stack([jnp.concatenate([bd(wa[d]), bd(wi[d])], axis=1) for d in range(2)]).astype(BF16)
    bg = jnp.stack([jnp.concatenate([ba[d], bi[d]])[None, :] for d in range(2)])
    return wg, bg, lam[:, None, :]


def _lru_coeffs(u, wg, bg, lam):
    C = u.shape[1]
    g = jnp.dot(u.astype(BF16), wg, preferred_element_type=F32) + bg
    r = jax.nn.sigmoid(g[:, :C])
    ig = jax.nn.sigmoid(g[:, C:])
    nl = -lam
    softplus = jnp.maximum(nl, 0.0) + jnp.log(1.0 + jnp.exp(-jnp.abs(nl)))
    log_a = (-LRU_C * softplus) * r
    a = jnp.exp(log_a)
    b = jnp.sqrt(1.0 - jnp.exp(2.0 * log_a)) * (ig * u)
    return a, b


def _lru_scan(a, b, h0, reverse, ac_ref, bc_ref, h_ref):
    T, C = a.shape
    row = lax.broadcasted_iota(jnp.int32, a.shape, 0) % SUBLANES
    for s in (1, 2, 4):
        shift = T - s if reverse else s
        keep = (row < SUBLANES - s) if reverse else (row >= s)
        b = jnp.where(keep, a * pltpu.roll(b, shift, 0) + b, b)
        a = jnp.where(keep, a * pltpu.roll(a, shift, 0), a)
    ac_ref[...] = a
    bc_ref[...] = b
    ng = T // SUBLANES

    def group(g, h):
        r0 = pl.multiple_of((ng - 1 - g if reverse else g) * SUBLANES, SUBLANES)
        hr = ac_ref[pl.ds(r0, SUBLANES), :] * h + bc_ref[pl.ds(r0, SUBLANES), :]
        h_ref[pl.ds(r0, SUBLANES), :] = hr
        edge = hr[0:1] if reverse else hr[SUBLANES - 1:SUBLANES]
        return jnp.broadcast_to(edge, (SUBLANES, C))

    return lax.fori_loop(0, ng, group, h0, unroll=4)


def _gelu_tanh(x):
    return 0.5 * x * (1.0 + jnp.tanh(math.sqrt(2.0 / math.pi) * (x + 0.044715 * (x * x * x))))


def _lru_ctx_kernel(u_ref, xg_ref, wg_ref, bg_ref, lam_ref, hend_ref, yc_ref, ac_ref, bc_ref, h_ref):
    u = u_ref[0]
    C = u.shape[1]
    total = jnp.zeros_like(u)
    for d, rev in enumerate((False, True)):
        a, b = _lru_coeffs(u, wg_ref[d], bg_ref[d], lam_ref[d])
        hl = _lru_scan(a, b, jnp.zeros((SUBLANES, C), F32), rev, ac_ref, bc_ref, h_ref)
        hend_ref[0, d:d + 1, :] = hl[0:1]
        total = total + h_ref[...]
    yc_ref[0] = (total * _gelu_tanh(xg_ref[0])).astype(yc_ref.dtype)


def _lru_ctx(u, xg, wg, bg, lam):
    B, Lc, C = u.shape
    tok = pl.BlockSpec((1, Lc, C), lambda b: (b, 0, 0))
    full = lambda a: pl.BlockSpec(a.shape, lambda b: (0,) * a.ndim)
    return pl.pallas_call(
        _lru_ctx_kernel,
        out_shape=[jax.ShapeDtypeStruct((B, 2, C), F32), jax.ShapeDtypeStruct((B, Lc, C), BF16)],
        grid=(B,),
        in_specs=[tok, tok, full(wg), full(bg), full(lam)],
        out_specs=[pl.BlockSpec((1, 2, C), lambda b: (b, 0, 0)), tok],
        scratch_shapes=[pltpu.VMEM((Lc, C), F32)] * 3,
        name="lru_ctx",
    )(u, xg, wg, bg, lam)


def _lru_dir_kernel(*refs, d, reverse):
    if reverse:
        u_ref, hend_ref, wg_ref, bg_ref, lam_ref, hf_ref, xg_ref, o_ref, ac_ref, bc_ref, h_ref, carry_ref = refs
    else:
        u_ref, hend_ref, wg_ref, bg_ref, lam_ref, o_ref, ac_ref, bc_ref, carry_ref = refs
        h_ref = o_ref.at[0]
    C = u_ref.shape[2]

    @pl.when(pl.program_id(1) == 0)
    def _():
        carry_ref[...] = jnp.broadcast_to(hend_ref[0, d:d + 1, :], (SUBLANES, C))

    a, b = _lru_coeffs(u_ref[0], wg_ref[d], bg_ref[d], lam_ref[d])
    carry_ref[...] = _lru_scan(a, b, carry_ref[...], reverse, ac_ref, bc_ref, h_ref)
    if reverse:
        o_ref[0] = ((hf_ref[0] + h_ref[...]) * _gelu_tanh(xg_ref[0])).astype(o_ref.dtype)


def _lru_dir(u, hend, wg, bg, lam, hf=None, xg=None):
    B, L, C = u.shape
    reverse = hf is not None
    T = LRU_CHUNK
    nb = L // T
    tok = pl.BlockSpec((1, T, C), (lambda b, i: (b, nb - 1 - i, 0)) if reverse else (lambda b, i: (b, i, 0)))
    full = lambda a: pl.BlockSpec(a.shape, lambda b, i: (0,) * a.ndim)
    ins = [u, hend, wg, bg, lam] + ([hf, xg] if reverse else [])
    specs = [tok, pl.BlockSpec((1, 2, C), lambda b, i: (b, 0, 0)), full(wg), full(bg), full(lam)]
    specs += [tok, tok] if reverse else []
    scratch = [pltpu.VMEM((T, C), F32)] * (3 if reverse else 2) + [pltpu.VMEM((SUBLANES, C), F32)]
    return pl.pallas_call(
        functools.partial(_lru_dir_kernel, d=int(reverse), reverse=reverse),
        out_shape=jax.ShapeDtypeStruct((B, L, C), BF16 if reverse else F32),
        grid=(B, nb),
        in_specs=specs,
        out_specs=tok,
        scratch_shapes=scratch,
        compiler_params=pltpu.CompilerParams(dimension_semantics=("arbitrary", "arbitrary")),
        name="lru_bwd" if reverse else "lru_fwd",
    )(*ins)


def _lru(u, xg, u_c, xg_c, wa, ba, wi, bi, lam):
    wg, bg, lam3 = _lru_gate_weights(wa, ba, wi, bi, lam)
    hend, yc = _lru_ctx(u_c, xg_c, wg, bg, lam3)
    hf = _lru_dir(u, hend, wg, bg, lam3)
    return _lru_dir(u, hend, wg, bg, lam3, hf, xg), yc


def kernel(x, c, ctx, c_ctx, ada_w, ada_b, g_mix_pre, g_mix_post, g_ffn_pre, g_ffn_post, w_in, w_out, hy_conv_w,
           hy_conv_b, hy_f_w1, hy_f_b1, hy_f_w2, hy_f_b2, hy_f_w3, hy_f_freq, hy_bias, na_rpb, lru_conv_w,
           lru_conv_b, lru_wa, lru_ba, lru_wi, lru_bi, lru_lam, ffn_w_gu, ffn_w_down):
    B, L, D = x.shape
    Lc = ctx.shape[1]
    tm = 512

    cond = jnp.zeros((8, D), F32).at[0:B].set(c).at[B].set(c_ctx)
    mods = _modulation(cond, ada_w, ada_b)

    xc = ctx
    for l in range(DEPTH):
        with_ctx_out = l < DEPTH - 1
        m = mods[l].reshape(8, 6, D)
        lat = [m[0:B, j][:, None, :] for j in range(6)]
        cx = [jnp.broadcast_to(m[B, j][None, None, :], (B, 1, D)) for j in range(6)]
        row = lambda a: a.reshape(1, -1)

        w_in_bf = w_in[l].astype(BF16)
        w_out_bf = w_out[l].astype(BF16)
        nchunk = D_FF // FF_CHUNK
        wg = ffn_w_gu[l][:, :D_FF].astype(BF16).reshape(D, nchunk, FF_CHUNK).transpose(1, 0, 2)
        wu = ffn_w_gu[l][:, D_FF:].astype(BF16).reshape(D, nchunk, FF_CHUNK).transpose(1, 0, 2)
        wd = ffn_w_down[l].astype(BF16).reshape(nchunk, FF_CHUNK, D)

        conv_args = (hy_conv_w[l], row(hy_conv_b[l]), lru_conv_w[l], row(lru_conv_b[l]))
        hv, hx1, hx2, q, k, v, lu, lg = _inproj(x, lat[1], lat[0], row(g_mix_pre[l]), w_in_bf, *conv_args, tm=tm)
        cv, cx1, cx2, cq, ck, cvv, clu, clg = _inproj(xc, cx[1], cx[0], row(g_mix_pre[l]), w_in_bf, *conv_args,
                                                      tm=Lc)

        filt = (hy_f_w1[l], hy_f_b1[l], hy_f_w2[l], hy_f_b2[l], hy_f_w3[l], hy_f_freq[l])
        y_hy = _hyena_jax(hv, hx1, hx2, filt, hy_bias[l])
        y_na = _na(q, k, v, ck, cvv, _na_bias_table(na_rpb[l]))
        y_lru, yc_lru = _lru(lu, lg, clu, clg, lru_wa[l], lru_ba[l], lru_wi[l], lru_bi[l], lru_lam[l])

        x = _outproj(y_hy.astype(BF16), y_na, y_lru, w_out_bf, x, row(g_mix_post[l]), lat[2], tm=tm)
        x = _ffn(x, lat[4], lat[3], lat[5], row(g_ffn_pre[l]), row(g_ffn_post[l]), wg, wu, wd, tm=tm)

        if with_ctx_out:
            yc_hy = _hyena_jax(cv, cx1, cx2, filt, hy_bias[l])
            yc_na = _ctx_attn(cq, ck, cvv)
            xc = _outproj(yc_hy.astype(BF16), yc_na, yc_lru, w_out_bf, xc, row(g_mix_post[l]), cx[2], tm=Lc)
            xc = _ffn(xc, cx[4], cx[3], cx[5], row(g_ffn_pre[l]), row(g_ffn_post[l]), wg, wu, wd, tm=Lc)
    return x
```

```python
import functools
import math

import jax
import jax.numpy as jnp
import numpy as np
from jax import lax
from jax.experimental import pallas as pl
from jax.experimental.pallas import tpu as pltpu

F32 = jnp.float32
BF16 = jnp.bfloat16

D_MODEL = 1024
DEPTH = 2
GRID_W = 64
HY_WIDTH = D_MODEL // 4
NA_HEAD_DIM = 64
NA_WIDTH = D_MODEL // 2
NA_HEADS = NA_WIDTH // NA_HEAD_DIM
LRU_WIDTH = D_MODEL // 4
LRU_BLOCKS = 4
IN_WIDTH = 3 * HY_WIDTH + 3 * NA_WIDTH + 2 * LRU_WIDTH
HY_ORDER = 2
HY_BANDS = 16
HY_FAST_DECAY = 0.3
HY_SLOW_DECAY = 1.5
HY_DECAY_TARGET = 1e-2
NA_WIN_ROWS = 8
NA_WIN_COLS = 16
LRU_C = 8.0
D_FF = -(-8 * D_MODEL // (3 * 256)) * 256
RMS_EPS = 1e-6
NEG_INF = -1e30

_HY_END = 3 * HY_WIDTH
_NA_END = _HY_END + 3 * NA_WIDTH
_LRU_MID = _NA_END + LRU_WIDTH

HALO = 8
SUBLANES = 8
VMEM_LIMIT = 48 * 1024 * 1024


def _rms(x, g):
    return x * lax.rsqrt(jnp.mean(x * x, axis=-1, keepdims=True) + RMS_EPS) * g


def _mod_kernel(ct_ref, w_ref, b_ref, o_ref, *, n_cond):
    ct = ct_ref[...]
    st = ct * jax.nn.sigmoid(ct)
    w = w_ref[0]
    rows = [jnp.sum(w * st[:, r:r + 1], axis=0, keepdims=True) for r in range(n_cond)]
    rows.append(jnp.zeros((SUBLANES - n_cond, w.shape[1]), F32))
    o_ref[0] = jnp.concatenate(rows, axis=0) + b_ref[0]


def _modulation(cond_t, n_cond, ada_w, ada_b):
    tn = 768
    n = ada_w.shape[-1]
    return pl.pallas_call(
        functools.partial(_mod_kernel, n_cond=n_cond),
        out_shape=jax.ShapeDtypeStruct((DEPTH, SUBLANES, n), F32),
        grid=(DEPTH, n // tn),
        in_specs=[pl.BlockSpec((D_MODEL, SUBLANES), lambda l, j: (0, 0)),
                  pl.BlockSpec((1, D_MODEL, tn), lambda l, j: (l, 0, j)),
                  pl.BlockSpec((1, 1, tn), lambda l, j: (l, 0, j))],
        out_specs=pl.BlockSpec((1, SUBLANES, tn), lambda l, j: (l, 0, j)),
        compiler_params=pltpu.CompilerParams(vmem_limit_bytes=VMEM_LIMIT),
        name="adaln_modulation",
    )(cond_t, ada_w, ada_b.reshape(DEPTH, 1, n))


def _inproj_kernel(xp_ref, xc_ref, xn_ref, sc_ref, sh_ref, g_ref, w_ref, hcw_ref, hcb_ref, lcw_ref, lcb_ref,
                   hv_ref, hx1_ref, hx2_ref, q_ref, k_ref, v_ref, lu_ref, lg_ref, pe_ref, *, tm):
    i = pl.program_id(1)
    last = pl.num_programs(1) - 1
    g = g_ref[...]
    sc1 = 1.0 + sc_ref[0]
    sh = sh_ref[0]

    def norm_mod(xv):
        return _rms(xv, g) * sc1 + sh

    hp = norm_mod(xp_ref[0]) * (i > 0).astype(F32)
    hn = norm_mod(xn_ref[0]) * (i < last).astype(F32)
    he = jnp.concatenate([hp, norm_mod(xc_ref[0]), hn], axis=0).astype(BF16)

    pe_ref[:, 0:_HY_END] = jnp.dot(he, w_ref[:, 0:_HY_END], preferred_element_type=F32)
    pe_ref[:, _HY_END:] = jnp.dot(he, w_ref[:, _NA_END:_LRU_MID], preferred_element_type=F32)
    hc = he[HALO:HALO + tm]
    qkv = jnp.dot(hc, w_ref[:, _HY_END:_NA_END], preferred_element_type=F32)
    q_ref[0] = (qkv[:, 0:NA_WIDTH] * (NA_HEAD_DIM ** -0.5)).astype(BF16)
    k_ref[0] = qkv[:, NA_WIDTH:2 * NA_WIDTH].astype(BF16)
    v_ref[0] = qkv[:, 2 * NA_WIDTH:].astype(BF16)
    lg_ref[0] = jnp.dot(hc, w_ref[:, _LRU_MID:], preferred_element_type=F32)

    u = hcb_ref[...]
    for kk in range(3):
        u = u + hcw_ref[kk:kk + 1, :] * pe_ref[pl.ds(HALO - 1 + kk, tm), 0:_HY_END]
    hv_ref[0] = u[:, 0:HY_WIDTH]
    hx1_ref[0] = u[:, HY_WIDTH:2 * HY_WIDTH]
    hx2_ref[0] = u[:, 2 * HY_WIDTH:]
    ul = lcb_ref[...]
    for kk in range(4):
        ul = ul + lcw_ref[kk:kk + 1, :] * pe_ref[pl.ds(HALO - 2 + kk, tm), _HY_END:]
    lu_ref[0] = ul


def _inproj(x, sc, sh, g, w_bf, hcw, hcb, lcw, lcb, *, tm):
    B, L, D = x.shape
    nb = tm // HALO
    nh = L // HALO
    tok = lambda w, dt: jax.ShapeDtypeStruct((B, L, w), dt)
    tspec = lambda w: pl.BlockSpec((1, tm, w), lambda b, i: (b, i, 0))
    full = lambda a: pl.BlockSpec(a.shape, lambda b, i: (0,) * a.ndim)
    vec = pl.BlockSpec((1, 1, D), lambda b, i: (b, 0, 0))
    return pl.pallas_call(
        functools.partial(_inproj_kernel, tm=tm),
        out_shape=[tok(HY_WIDTH, F32)] * 3 + [tok(NA_WIDTH, BF16)] * 3 + [tok(LRU_WIDTH, F32)] * 2,
        grid=(B, L // tm),
        in_specs=[pl.BlockSpec((1, HALO, D), lambda b, i: (b, jnp.maximum(i * nb - 1, 0), 0)),
                  pl.BlockSpec((1, tm, D), lambda b, i: (b, i, 0)),
                  pl.BlockSpec((1, HALO, D), lambda b, i: (b, jnp.minimum((i + 1) * nb, nh - 1), 0)),
                  vec, vec, full(g), full(w_bf), full(hcw), full(hcb), full(lcw), full(lcb)],
        out_specs=[tspec(HY_WIDTH)] * 3 + [tspec(NA_WIDTH)] * 3 + [tspec(LRU_WIDTH)] * 2,
        scratch_shapes=[pltpu.VMEM((tm + 2 * HALO, _HY_END + LRU_WIDTH), F32)],
        compiler_params=pltpu.CompilerParams(vmem_limit_bytes=VMEM_LIMIT),
        name="inproj",
    )(x, x, x, sc, sh, g, w_bf, hcw, hcb, lcw, lcb)


FF_CHUNK = 256


def _out_ffn_kernel(yh_ref, yn_ref, yl_ref, wo_ref, x_ref, gm_ref, gtm_ref, sc_ref, sh_ref, gt_ref, gpre_ref,
                    gpost_ref, wg_ref, wu_ref, wd_ref, o_ref):
    y = jnp.dot(yh_ref[0].astype(BF16), wo_ref[0:HY_WIDTH], preferred_element_type=F32)
    y = y + jnp.dot(yn_ref[0], wo_ref[HY_WIDTH:HY_WIDTH + NA_WIDTH], preferred_element_type=F32)
    y = y + jnp.dot(yl_ref[0], wo_ref[HY_WIDTH + NA_WIDTH:], preferred_element_type=F32)
    x = x_ref[0] + gtm_ref[0] * _rms(y, gm_ref[...])
    h = (_rms(x, gpre_ref[...]) * (1.0 + sc_ref[0]) + sh_ref[0]).astype(BF16)
    acc = jnp.zeros(x.shape, F32)
    for j in range(D_FF // FF_CHUNK):
        g = jnp.dot(h, wg_ref[j], preferred_element_type=F32)
        u = jnp.dot(h, wu_ref[j], preferred_element_type=F32)
        a = (g * jax.nn.sigmoid(g) * u).astype(BF16)
        acc = acc + jnp.dot(a, wd_ref[j], preferred_element_type=F32)
    o_ref[0] = x + gt_ref[0] * _rms(acc, gpost_ref[...])


def _out_ffn(yh, yn, yl, wo, x, gm, gtm, sc, sh, gt, gpre, gpost, wg, wu, wd, *, tm):
    B, L, D = x.shape
    tspec = lambda w: pl.BlockSpec((1, tm, w), lambda b, i: (b, i, 0))
    vec = pl.BlockSpec((1, 1, D), lambda b, i: (b, 0, 0))
    full = lambda a: pl.BlockSpec(a.shape, lambda b, i: (0,) * a.ndim)
    res = lambda a: pl.BlockSpec(a.shape, lambda b, i: (0,) * a.ndim, pipeline_mode=pl.Buffered(1))
    return pl.pallas_call(
        _out_ffn_kernel,
        out_shape=jax.ShapeDtypeStruct((B, L, D), F32),
        grid=(B, L // tm),
        in_specs=[tspec(HY_WIDTH), tspec(NA_WIDTH), tspec(LRU_WIDTH), res(wo), tspec(D), full(gm), vec,
                  vec, vec, vec, full(gpre), full(gpost), res(wg), res(wu), res(wd)],
        out_specs=tspec(D),
        compiler_params=pltpu.CompilerParams(vmem_limit_bytes=VMEM_LIMIT),
        name="out_ffn",
    )(yh, yn, yl, wo, x, gm, gtm, sc, sh, gt, gpre, gpost, wg, wu, wd)


HY_N1 = 256
HY_N2 = 128
HY_TB = 8
HY_FILT_ROWS = 2048


def _block_cplx(re, im):
    return np.block([[re, -im], [im, re]])


@functools.lru_cache(maxsize=None)
def _dft_consts():
    n = HY_N1 * HY_N2
    i1 = np.arange(HY_N1, dtype=np.float64)
    a1 = 2.0 * np.pi * np.outer(i1, i1) / HY_N1
    fr, fi = np.cos(a1), -np.sin(a1)
    h = HY_N1 // 2
    i2 = np.arange(HY_N2, dtype=np.float64)
    a2 = 2.0 * np.pi * np.outer(i2, i2) / HY_N2
    at = 2.0 * np.pi * np.outer(i1, i2) / n
    return dict(
        fa_data=np.asarray(_block_cplx(fr[:, :h], fi[:, :h]), np.float32),
        fa_filt=np.asarray(np.concatenate([fr, fi], axis=0), np.float32),
        fa_inv=np.asarray(_block_cplx(fr[:h, :], fi[:h, :]), np.float32),
        fb=np.asarray(_block_cplx(np.cos(a2), -np.sin(a2)), np.float32),
        twr=np.asarray(np.cos(at), np.float32), twi=np.asarray(-np.sin(at), np.float32))


def _dft_const(name):
    return jnp.asarray(_dft_consts()[name], F32)


def _twiddle_cols(twr_ref, twi_ref, col):
    sel = lax.broadcasted_iota(jnp.int32, twr_ref.shape, 1) == col
    cr = jnp.sum(jnp.where(sel, twr_ref[...], 0.0), axis=1, keepdims=True)
    ci = jnp.sum(jnp.where(sel, twi_ref[...], 0.0), axis=1, keepdims=True)
    return cr, ci


def _hy_stage_a_kernel(u_ref, fa_ref, twr_ref, twi_ref, o_ref, *, real_input):
    i = pl.program_id(0)
    fa = fa_ref[...].astype(BF16)
    for j in range(HY_TB):
        if real_input:
            x = u_ref[:, j, :]
        else:
            x = jnp.concatenate([u_ref[0, :, j, :], u_ref[1, :, j, :]], axis=0)
        y = jnp.dot(fa, x.astype(BF16), preferred_element_type=F32)
        cr, ci = _twiddle_cols(twr_ref, twi_ref, i * HY_TB + j)
        yr, yi = y[:HY_N1], y[HY_N1:]
        o_ref[j, 0:HY_N1, :] = yr * cr - yi * ci
        o_ref[j, HY_N1:, :] = yr * ci + yi * cr


def _hy_stage_a(u, real_input):
    C = u.shape[-1]
    if real_input:
        uv = u.reshape(HY_N1, HY_N2, C)
        uspec = pl.BlockSpec((HY_N1, HY_TB, C), lambda i: (0, i, 0))
        fa = _dft_const("fa_filt")
    else:
        assert u.shape[0] == 2
        uv = u.reshape(2, HY_N1 // 2, HY_N2, C)
        uspec = pl.BlockSpec((2, HY_N1 // 2, HY_TB, C), lambda i: (0, 0, i, 0))
        fa = _dft_const("fa_data")
    twr, twi = _dft_const("twr"), _dft_const("twi")
    full = lambda a: pl.BlockSpec(a.shape, lambda i: (0,) * a.ndim)
    return pl.pallas_call(
        functools.partial(_hy_stage_a_kernel, real_input=real_input),
        out_shape=jax.ShapeDtypeStruct((HY_N2, 2 * HY_N1, C), F32),
        grid=(HY_N2 // HY_TB,),
        in_specs=[uspec, full(fa), full(twr), full(twi)],
        out_specs=pl.BlockSpec((HY_TB, 2 * HY_N1, C), lambda i: (i, 0, 0)),
        compiler_params=pltpu.CompilerParams(vmem_limit_bytes=VMEM_LIMIT),
        name="hy_stage_a_filt" if real_input else "hy_stage_a",
    )(uv, fa, twr, twi)


def _hy_stage_b_kernel(ar_ref, ai_ref, fb_ref, k_ref, o_ref, *, filter_mode):
    fb = fb_ref[...].astype(BF16)
    for j in range(HY_TB):
        x = jnp.concatenate([ar_ref[:, j, :], ai_ref[:, j, :]], axis=0)
        y = jnp.dot(fb, x.astype(BF16), preferred_element_type=F32)
        if filter_mode:
            o_ref[j] = y * (1.0 / (k_ref[...] * float(HY_N1 * HY_N2)))
        else:
            yr, yi = y[:HY_N2], y[HY_N2:]
            kr, ki = k_ref[j, 0:HY_N2, :], k_ref[j, HY_N2:, :]
            p = jnp.concatenate([yr * kr - yi * ki, -(yr * ki + yi * kr)], axis=0)
            o_ref[j] = jnp.dot(fb, p.astype(BF16), preferred_element_type=F32)


def _hy_stage_b(a, k, filter_mode, order=0):
    C = a.shape[-1]
    nf = HY_N1 // HY_TB
    fb = _dft_const("fb")
    re = pl.BlockSpec((HY_N2, HY_TB, C), lambda i: (0, i, 0))
    im = pl.BlockSpec((HY_N2, HY_TB, C), lambda i: (0, nf + i, 0))
    if filter_mode:
        kspec = pl.BlockSpec(k.shape, lambda i: (0, 0))
    else:
        kspec = pl.BlockSpec((HY_TB, 2 * HY_N2, C), lambda i: (i, 0, order))
    ospec = pl.BlockSpec((HY_TB, 2 * HY_N2, C), lambda i: (i, 0, 0))
    return pl.pallas_call(
        functools.partial(_hy_stage_b_kernel, filter_mode=filter_mode),
        out_shape=jax.ShapeDtypeStruct((HY_N1, 2 * HY_N2, C), F32),
        grid=(nf,),
        in_specs=[re, im, pl.BlockSpec(fb.shape, lambda i: (0, 0)), kspec],
        out_specs=ospec,
        compiler_params=pltpu.CompilerParams(vmem_limit_bytes=VMEM_LIMIT),
        name="hy_stage_b_filt" if filter_mode else "hy_stage_b",
    )(a, a, fb, k)


def _hy_stage_c_kernel(cr_ref, ci_ref, fa_ref, twr_ref, twi_ref, gate_ref, zin_ref, bias_ref, o_ref):
    i = pl.program_id(0)
    fa = fa_ref[...].astype(BF16)
    h = HY_N1 // 2
    for j in range(HY_TB):
        tr, ti = _twiddle_cols(twr_ref, twi_ref, i * HY_TB + j)
        cr, ci = cr_ref[:, j, :], ci_ref[:, j, :]
        x = jnp.concatenate([cr * tr - ci * ti, cr * ti + ci * tr], axis=0)
        y = jnp.dot(fa, x.astype(BF16), preferred_element_type=F32)
        for b, yb in ((0, y[:h]), (1, -y[h:])):
            o_ref[b, :, j, :] = gate_ref[b, :, j, :] * (yb + bias_ref[...] * zin_ref[b, :, j, :])


def _hy_stage_c(c, gate, zin, bias):
    B, L, C = gate.shape
    nm = HY_N2 // HY_TB
    fa, twr, twi = _dft_const("fa_inv"), _dft_const("twr"), _dft_const("twi")
    re = pl.BlockSpec((HY_N1, HY_TB, C), lambda i: (0, i, 0))
    im = pl.BlockSpec((HY_N1, HY_TB, C), lambda i: (0, nm + i, 0))
    full = lambda a: pl.BlockSpec(a.shape, lambda i: (0,) * a.ndim)
    tok = pl.BlockSpec((B, HY_N1 // 2, HY_TB, C), lambda i: (0, 0, i, 0))
    shape4 = (B, HY_N1 // 2, HY_N2, C)
    out = pl.pallas_call(
        _hy_stage_c_kernel,
        out_shape=jax.ShapeDtypeStruct(shape4, F32),
        grid=(nm,),
        in_specs=[re, im, full(fa), full(twr), full(twi), tok, tok, full(bias)],
        out_specs=tok,
        compiler_params=pltpu.CompilerParams(vmem_limit_bytes=VMEM_LIMIT),
        name="hy_stage_c",
    )(c, c, fa, twr, twi, gate.reshape(shape4), zin.reshape(shape4), bias)
    return out.reshape(B, L, C)


def _hy_filter_kernel(w1t_ref, w1c_ref, w1s_ref, b1_ref, f0_ref, w2_ref, b2_ref, f1_ref, w3_ref, dl_ref,
                      k_ref, s_ref, *, L, R):
    i = pl.program_id(0)
    base = i * (R * SUBLANES)

    def pos_t(width, lanes):
        row = lax.broadcasted_iota(jnp.int32, (R, lanes), 0)
        grp = lax.broadcasted_iota(jnp.int32, (R, lanes), 1) // width
        n = base + row * SUBLANES + grp
        return n, jnp.where(n < L, n, 2 * L - n).astype(F32) / L

    _, t16 = pos_t(HY_BANDS, SUBLANES * HY_BANDS)
    band = (lax.broadcasted_iota(jnp.int32, t16.shape, 1) % HY_BANDS + 1).astype(F32)
    ang = 2.0 * math.pi * t16 * band
    _, t64 = pos_t(w2_ref.shape[0] // SUBLANES, w2_ref.shape[0])
    h = t64 * w1t_ref[...] + b1_ref[...]
    h = h + jnp.dot(jnp.cos(ang).astype(BF16), w1c_ref[...], preferred_element_type=F32)
    h = h + jnp.dot(jnp.sin(ang).astype(BF16), w1s_ref[...], preferred_element_type=F32)
    h = jnp.sin(f0_ref[...] * h)
    h = jnp.sin(f1_ref[...] * (jnp.dot(h.astype(BF16), w2_ref[...], preferred_element_type=F32) + b2_ref[...]))
    k = jnp.dot(h.astype(BF16), w3_ref[0], preferred_element_type=F32)
    width = k.shape[1] // SUBLANES
    n, t = pos_t(width, k.shape[1])
    k = jnp.where(n == L, 0.0, k * jnp.exp(-t * dl_ref[...]))
    k_ref[...] = k
    ks = jnp.sum(jnp.abs(k), axis=0, keepdims=True)
    tot = ks[:, 0:width]
    for s in range(1, SUBLANES):
        tot = tot + ks[:, s * width:(s + 1) * width]

    @pl.when(i == 0)
    def _():
        s_ref[...] = jnp.zeros_like(s_ref)

    s_ref[...] += tot


def _hy_filter(L, rows, f_w1, f_b1, f_w2, f_b2, f_w3, f_freq):
    R = rows // SUBLANES
    nb = 2 * L // rows
    assert nb % 2 == 0
    hid = f_w2.shape[0]
    oc = HY_ORDER * HY_WIDTH
    eye = jnp.eye(SUBLANES, dtype=F32)
    tile = lambda v: jnp.tile(v.reshape(1, -1), (1, SUBLANES))
    w3 = f_w3.reshape(hid, HY_ORDER, 2, HY_WIDTH)
    w3 = jnp.stack([jnp.kron(eye, w3[:, :, d].reshape(hid, oc)) for d in range(2)]).astype(BF16)
    deltas = np.abs(np.linspace(math.log(HY_DECAY_TARGET) / HY_FAST_DECAY,
                                math.log(HY_DECAY_TARGET) / HY_SLOW_DECAY, HY_WIDTH, dtype=np.float32))
    ins = [tile(f_w1[0]), jnp.kron(eye, f_w1[1:1 + HY_BANDS]).astype(BF16),
           jnp.kron(eye, f_w1[1 + HY_BANDS:]).astype(BF16), tile(f_b1), tile(f_freq[0]),
           jnp.kron(eye, f_w2).astype(BF16), tile(f_b2), tile(f_freq[1]), w3,
           jnp.asarray(np.tile(deltas, HY_ORDER * SUBLANES)[None, :])]
    full = lambda a: pl.BlockSpec(a.shape, lambda i: (0,) * a.ndim)
    specs = [full(a) for a in ins]
    specs[8] = pl.BlockSpec((1,) + w3.shape[1:], lambda i: (i // (nb // 2), 0, 0))
    k, s = pl.pallas_call(
        functools.partial(_hy_filter_kernel, L=L, R=R),
        out_shape=[jax.ShapeDtypeStruct((2 * L // SUBLANES, SUBLANES * oc), F32), jax.ShapeDtypeStruct((1, oc), F32)],
        grid=(nb,),
        in_specs=specs,
        out_specs=[pl.BlockSpec((R, SUBLANES * oc), lambda i: (i, 0)), pl.BlockSpec((1, oc), lambda i: (0, 0))],
        compiler_params=pltpu.CompilerParams(dimension_semantics=("arbitrary",), vmem_limit_bytes=VMEM_LIMIT),
        name="hy_filter",
    )(*ins)
    return k.reshape(2 * L, oc), s


def _hyena(v, x1, x2, filt, hy_bias):
    B, L, C = v.shape
    assert B == 2 and 2 * L == HY_N1 * HY_N2
    kc, ksum = _hy_filter(L, HY_FILT_ROWS, *filt)
    kspec = _hy_stage_b(_hy_stage_a(kc, True), ksum, True)
    z = v
    for n, gate in enumerate((x1, x2)):
        c = _hy_stage_b(_hy_stage_a(z, False), kspec, False, order=n)
        z = _hy_stage_c(c, gate, z, hy_bias[n].reshape(1, C))
    return z


@functools.lru_cache(maxsize=None)
def _ctx_dft_consts(n):
    i = np.arange(n, dtype=np.float64)
    a = 2.0 * np.pi * np.outer(i, i) / n
    f = np.concatenate([np.cos(a), -np.sin(a)], axis=0)
    return np.asarray(f, np.float32), np.asarray(f.T[:n // 2], np.float32)


def _hyena_ctx_kernel(v_ref, x1_ref, x2_ref, kc_ref, ks_ref, bias_ref, ff_ref, fi_ref, o_ref):
    n = ff_ref.shape[1]
    lc = n // 2
    C = v_ref.shape[2]
    ff = ff_ref[...].astype(BF16)
    kspec = jnp.dot(ff, kc_ref[...].astype(BF16), preferred_element_type=F32) * (1.0 / (ks_ref[...] * float(n)))
    z = jnp.concatenate([v_ref[0], v_ref[1]], axis=1)
    for o, g_ref in enumerate((x1_ref, x2_ref)):
        x = jnp.dot(ff[:, :lc], z.astype(BF16), preferred_element_type=F32)
        k = kspec[:, o * C:(o + 1) * C]
        kr = jnp.concatenate([k[:n], k[:n]], axis=1)
        ki = jnp.concatenate([k[n:], k[n:]], axis=1)
        xr, xi = x[:n], x[n:]
        y = jnp.concatenate([xr * kr - xi * ki, xr * ki + xi * kr], axis=0)
        conv = jnp.dot(fi_ref[...].astype(BF16), y.astype(BF16), preferred_element_type=F32)
        bias = jnp.concatenate([bias_ref[o:o + 1, :], bias_ref[o:o + 1, :]], axis=1)
        z = jnp.concatenate([g_ref[0], g_ref[1]], axis=1) * (conv + bias * z)
    o_ref[0] = z[:, :C]
    o_ref[1] = z[:, C:]


def _hyena_ctx(v, x1, x2, filt, hy_bias):
    B, Lc, C = v.shape
    assert B == 2
    kc, ksum = _hy_filter(Lc, Lc, *filt)
    ff, fi = (jnp.asarray(m, F32) for m in _ctx_dft_consts(2 * Lc))
    return pl.pallas_call(
        _hyena_ctx_kernel,
        out_shape=jax.ShapeDtypeStruct((B, Lc, C), F32),
        compiler_params=pltpu.CompilerParams(vmem_limit_bytes=VMEM_LIMIT),
        name="hyena_ctx",
    )(v, x1, x2, kc, ksum, hy_bias, ff, fi)


NA_ROWS_PER_STEP = 8
_NT = (((1,), (1,)), ((), ()))


def _na_bias_table(rpb):
    qc = np.arange(GRID_W)[:, None]
    kc = np.arange(GRID_W)[None, :]
    start = np.clip(qc - NA_WIN_COLS // 2, 0, GRID_W - NA_WIN_COLS)
    valid = (kc >= start) & (kc < start + NA_WIN_COLS)
    pad = jnp.pad(rpb, ((0, 0), (0, 0), (GRID_W, GRID_W)))
    shift = GRID_W + NA_WIN_COLS - 1
    toep = jnp.stack([pad[:, :, shift - c:shift - c + GRID_W] for c in range(GRID_W)], axis=2)
    full = jnp.where(jnp.asarray(valid)[None, None], toep, NEG_INF)
    t = jnp.stack([full[:, d:d + NA_WIN_ROWS] for d in range(NA_WIN_ROWS)], axis=1)
    t = t.reshape(NA_HEADS // 2, 2, NA_WIN_ROWS, NA_WIN_ROWS, GRID_W, GRID_W).transpose(0, 2, 1, 4, 3, 5)
    return t.reshape(NA_HEADS // 2, NA_WIN_ROWS, 2 * GRID_W, NA_WIN_ROWS * GRID_W).astype(F32)


def _na_kernel(q_ref, kp_ref, kc_ref, kn_ref, vp_ref, vc_ref, vn_ref, ck_ref, cv_ref, bias_ref, o_ref,
               wk_ref, wv_ref, sc_ref, pc_ref, ol_ref, li_ref):
    i = pl.program_id(2)
    last = pl.num_programs(2) - 1
    blk = NA_ROWS_PER_STEP * GRID_W
    for n, (kr, vr) in enumerate(((kp_ref, vp_ref), (kc_ref, vc_ref), (kn_ref, vn_ref))):
        wk_ref[n * blk:(n + 1) * blk] = kr[0]
        wv_ref[n * blk:(n + 1) * blk] = vr[0]
    first_head = lax.broadcasted_iota(jnp.int32, (GRID_W, 2 * NA_HEAD_DIM), 1) < NA_HEAD_DIM
    q = q_ref[0]
    zero = jnp.zeros((GRID_W, 2 * NA_HEAD_DIM), q.dtype)
    pieces = []
    for j in range(NA_ROWS_PER_STEP):
        qj = q[j * GRID_W:(j + 1) * GRID_W]
        pieces += [jnp.where(first_head, qj, zero), jnp.where(first_head, zero, qj)]
    qs = jnp.concatenate(pieces, axis=0)
    sc_ref[...] = lax.dot_general(qs, ck_ref[0], _NT, preferred_element_type=F32)
    half = NA_WIN_ROWS // 2
    rows = 2 * GRID_W
    for j in range(NA_ROWS_PER_STEP):
        off = jnp.where(i == 0, max(j + half, NA_ROWS_PER_STEP),
                        jnp.where(i == last, min(j + half, NA_ROWS_PER_STEP), j + half))
        d = off - j - 1
        start = pl.multiple_of(off * GRID_W, GRID_W)
        kw = wk_ref[pl.ds(start, NA_WIN_ROWS * GRID_W), :]
        vw = wv_ref[pl.ds(start, NA_WIN_ROWS * GRID_W), :]
        r0 = j * rows
        s_loc = lax.dot_general(qs[r0:r0 + rows], kw, _NT, preferred_element_type=F32) + bias_ref[0, d]
        s_ctx = sc_ref[r0:r0 + rows, :]
        m = jnp.maximum(jnp.max(s_loc, axis=1, keepdims=True), jnp.max(s_ctx, axis=1, keepdims=True))
        p_loc = jnp.exp(s_loc - m)
        p_ctx = jnp.exp(s_ctx - m)
        l = jnp.sum(p_loc, axis=1, keepdims=True) + jnp.sum(p_ctx, axis=1, keepdims=True)
        pc_ref[r0:r0 + rows, :] = p_ctx.astype(pc_ref.dtype)
        ol_ref[r0:r0 + rows, :] = jnp.dot(p_loc.astype(BF16), vw, preferred_element_type=F32)
        li_ref[r0:r0 + rows, :] = jnp.broadcast_to(1.0 / l, (rows, 2 * NA_HEAD_DIM))
    o = (ol_ref[...] + jnp.dot(pc_ref[...], cv_ref[0], preferred_element_type=F32)) * li_ref[...]
    for j in range(NA_ROWS_PER_STEP):
        r0 = j * rows
        oj = jnp.where(first_head, o[r0:r0 + GRID_W], o[r0 + GRID_W:r0 + rows])
        o_ref[0, j * GRID_W:(j + 1) * GRID_W, :] = oj.astype(o_ref.dtype)


def _na(q, k, v, ck, cv, bias):
    B, L, W = q.shape
    Lc = ck.shape[1]
    blk = NA_ROWS_PER_STEP * GRID_W
    nblk = L // blk
    assert NA_ROWS_PER_STEP == NA_WIN_ROWS and nblk >= 2
    pw = 2 * NA_HEAD_DIM
    cur = pl.BlockSpec((1, blk, pw), lambda b, h, i: (b, i, h))
    prev = pl.BlockSpec((1, blk, pw), lambda b, h, i: (b, jnp.maximum(i - 1, 0), h))
    nxt = pl.BlockSpec((1, blk, pw), lambda b, h, i: (b, jnp.minimum(i + 1, nblk - 1), h))
    cspec = pl.BlockSpec((1, Lc, pw), lambda b, h, i: (b, 0, h))
    bspec = pl.BlockSpec((1,) + bias.shape[1:], lambda b, h, i: (h, 0, 0, 0))
    stacked = 2 * blk
    return pl.pallas_call(
        _na_kernel,
        out_shape=jax.ShapeDtypeStruct((B, L, W), BF16),
        grid=(B, W // pw, nblk),
        in_specs=[cur, prev, cur, nxt, prev, cur, nxt, cspec, cspec, bspec],
        out_specs=cur,
        scratch_shapes=[pltpu.VMEM((3 * blk, pw), BF16), pltpu.VMEM((3 * blk, pw), BF16),
                        pltpu.VMEM((stacked, Lc), F32), pltpu.VMEM((stacked, Lc), BF16),
                        pltpu.VMEM((stacked, pw), F32), pltpu.VMEM((stacked, pw), F32)],
        compiler_params=pltpu.CompilerParams(vmem_limit_bytes=VMEM_LIMIT),
        name="nattn",
    )(q, k, k, k, v, v, v, ck, cv, bias)


def _ctx_attn_kernel(q_ref, k_ref, v_ref, o_ref):
    q = q_ref[0]
    k = k_ref[0]
    v = v_ref[0]
    first_head = lax.broadcasted_iota(jnp.int32, q.shape, 1) < NA_HEAD_DIM
    outs = []
    for h in range(2):
        qm = jnp.where(first_head if h == 0 else jnp.logical_not(first_head), q, jnp.zeros_like(q))
        s = lax.dot_general(qm, k, _NT, preferred_element_type=F32)
        p = jnp.exp(s - jnp.max(s, axis=1, keepdims=True))
        o = jnp.dot(p.astype(BF16), v, preferred_element_type=F32)
        outs.append(o / jnp.sum(p, axis=1, keepdims=True))
    o_ref[0] = jnp.where(first_head, outs[0], outs[1]).astype(o_ref.dtype)


def _ctx_attn(q, k, v):
    B, Lc, W = q.shape
    pw = 2 * NA_HEAD_DIM
    spec = pl.BlockSpec((1, Lc, pw), lambda b, h: (b, 0, h))
    return pl.pallas_call(
        _ctx_attn_kernel,
        out_shape=jax.ShapeDtypeStruct((B, Lc, W), BF16),
        grid=(B, W // pw),
        in_specs=[spec, spec, spec],
        out_specs=spec,
        name="ctx_attn",
    )(q, k, v)


LRU_CHUNK = 512


def _lru_gate_weights(wa, ba, wi, bi, lam):
    def bd(w):
        return jax.scipy.linalg.block_diag(*[w[n] for n in range(LRU_BLOCKS)])
    wg = jnp.stack([jnp.concatenate([bd(wa[d]), bd(wi[d])], axis=1) for d in range(2)]).astype(BF16)
    bg = jnp.stack([jnp.concatenate([ba[d], bi[d]])[None, :] for d in range(2)])
    return wg, bg, lam[:, None, :]


def _lru_coeffs(u, wg, bg, lam):
    C = u.shape[1]
    g = jnp.dot(u.astype(BF16), wg, preferred_element_type=F32) + bg
    r = jax.nn.sigmoid(g[:, :C])
    ig = jax.nn.sigmoid(g[:, C:])
    nl = -lam
    softplus = jnp.maximum(nl, 0.0) + jnp.log(1.0 + jnp.exp(-jnp.abs(nl)))
    log_a = (-LRU_C * softplus) * r
    a = jnp.exp(log_a)
    b = jnp.sqrt(1.0 - jnp.exp(2.0 * log_a)) * (ig * u)
    return a, b


def _lru_scan(a, b, h0, reverse, ac_ref, bc_ref, h_ref):
    T, C = a.shape
    row = lax.broadcasted_iota(jnp.int32, a.shape, 0) % SUBLANES
    for s in (1, 2, 4):
        shift = T - s if reverse else s
        keep = (row < SUBLANES - s) if reverse else (row >= s)
        b = jnp.where(keep, a * pltpu.roll(b, shift, 0) + b, b)
        a = jnp.where(keep, a * pltpu.roll(a, shift, 0), a)
    ac_ref[...] = a
    bc_ref[...] = b
    ng = T // SUBLANES

    def group(g, h):
        r0 = pl.multiple_of((ng - 1 - g if reverse else g) * SUBLANES, SUBLANES)
        hr = ac_ref[pl.ds(r0, SUBLANES), :] * h + bc_ref[pl.ds(r0, SUBLANES), :]
        h_ref[pl.ds(r0, SUBLANES), :] = hr
        edge = hr[0:1] if reverse else hr[SUBLANES - 1:SUBLANES]
        return jnp.broadcast_to(edge, (SUBLANES, C))

    return lax.fori_loop(0, ng, group, h0, unroll=4)


def _gelu_tanh(x):
    return 0.5 * x * (1.0 + jnp.tanh(math.sqrt(2.0 / math.pi) * (x + 0.044715 * (x * x * x))))


def _lru_ctx_kernel(u_ref, xg_ref, wg_ref, bg_ref, lam_ref, hend_ref, yc_ref, ac_ref, bc_ref, h_ref):
    u = u_ref[0]
    C = u.shape[1]
    total = jnp.zeros_like(u)
    for d, rev in enumerate((False, True)):
        a, b = _lru_coeffs(u, wg_ref[d], bg_ref[d], lam_ref[d])
        hl = _lru_scan(a, b, jnp.zeros((SUBLANES, C), F32), rev, ac_ref, bc_ref, h_ref)
        hend_ref[0, d:d + 1, :] = hl[0:1]
        total = total + h_ref[...]
    yc_ref[0] = (total * _gelu_tanh(xg_ref[0])).astype(yc_ref.dtype)


def _lru_ctx(u, xg, wg, bg, lam):
    B, Lc, C = u.shape
    tok = pl.BlockSpec((1, Lc, C), lambda b: (b, 0, 0))
    full = lambda a: pl.BlockSpec(a.shape, lambda b: (0,) * a.ndim)
    return pl.pallas_call(
        _lru_ctx_kernel,
        out_shape=[jax.ShapeDtypeStruct((B, 2, C), F32), jax.ShapeDtypeStruct((B, Lc, C), BF16)],
        grid=(B,),
        in_specs=[tok, tok, full(wg), full(bg), full(lam)],
        out_specs=[pl.BlockSpec((1, 2, C), lambda b: (b, 0, 0)), tok],
        scratch_shapes=[pltpu.VMEM((Lc, C), F32)] * 3,
        name="lru_ctx",
    )(u, xg, wg, bg, lam)


def _lru_dir_kernel(*refs, d, reverse):
    if reverse:
        u_ref, hend_ref, wg_ref, bg_ref, lam_ref, hf_ref, xg_ref, o_ref, ac_ref, bc_ref, h_ref, carry_ref = refs
    else:
        u_ref, hend_ref, wg_ref, bg_ref, lam_ref, o_ref, ac_ref, bc_ref, carry_ref = refs
        h_ref = o_ref.at[0]
    C = u_ref.shape[2]

    @pl.when(pl.program_id(1) == 0)
    def _():
        carry_ref[...] = jnp.broadcast_to(hend_ref[0, d:d + 1, :], (SUBLANES, C))

    a, b = _lru_coeffs(u_ref[0], wg_ref[d], bg_ref[d], lam_ref[d])
    carry_ref[...] = _lru_scan(a, b, carry_ref[...], reverse, ac_ref, bc_ref, h_ref)
    if reverse:
        o_ref[0] = ((hf_ref[0] + h_ref[...]) * _gelu_tanh(xg_ref[0])).astype(o_ref.dtype)


def _lru_dir(u, hend, wg, bg, lam, hf=None, xg=None):
    B, L, C = u.shape
    reverse = hf is not None
    T = LRU_CHUNK
    nb = L // T
    tok = pl.BlockSpec((1, T, C), (lambda b, i: (b, nb - 1 - i, 0)) if reverse else (lambda b, i: (b, i, 0)))
    full = lambda a: pl.BlockSpec(a.shape, lambda b, i: (0,) * a.ndim)
    ins = [u, hend, wg, bg, lam] + ([hf, xg] if reverse else [])
    specs = [tok, pl.BlockSpec((1, 2, C), lambda b, i: (b, 0, 0)), full(wg), full(bg), full(lam)]
    specs += [tok, tok] if reverse else []
    scratch = [pltpu.VMEM((T, C), F32)] * (3 if reverse else 2) + [pltpu.VMEM((SUBLANES, C), F32)]
    return pl.pallas_call(
        functools.partial(_lru_dir_kernel, d=int(reverse), reverse=reverse),
        out_shape=jax.ShapeDtypeStruct((B, L, C), BF16 if reverse else F32),
        grid=(B, nb),
        in_specs=specs,
        out_specs=tok,
        scratch_shapes=scratch,
        compiler_params=pltpu.CompilerParams(dimension_semantics=("arbitrary", "arbitrary")),
        name="lru_bwd" if reverse else "lru_fwd",
    )(*ins)


def _lru(u, xg, u_c, xg_c, wa, ba, wi, bi, lam):
    wg, bg, lam3 = _lru_gate_weights(wa, ba, wi, bi, lam)
    hend, yc = _lru_ctx(u_c, xg_c, wg, bg, lam3)
    hf = _lru_dir(u, hend, wg, bg, lam3)
    return _lru_dir(u, hend, wg, bg, lam3, hf, xg), yc


def kernel(x, c, ctx, c_ctx, ada_w, ada_b, g_mix_pre, g_mix_post, g_ffn_pre, g_ffn_post, w_in, w_out, hy_conv_w,
           hy_conv_b, hy_f_w1, hy_f_b1, hy_f_w2, hy_f_b2, hy_f_w3, hy_f_freq, hy_bias, na_rpb, lru_conv_w,
           lru_conv_b, lru_wa, lru_ba, lru_wi, lru_bi, lru_lam, ffn_w_gu, ffn_w_down):
    B, L, D = x.shape
    Lc = ctx.shape[1]
    tm = 512

    assert B + 1 <= SUBLANES
    cond_t = jnp.zeros((D, SUBLANES), F32).at[:, 0:B].set(c.T).at[:, B].set(c_ctx)
    mods = _modulation(cond_t, B + 1, ada_w, ada_b)

    xc = ctx
    for l in range(DEPTH):
        with_ctx_out = l < DEPTH - 1
        m = mods[l].reshape(8, 6, D)
        lat = [m[0:B, j][:, None, :] for j in range(6)]
        cx = [jnp.broadcast_to(m[B, j][None, None, :], (B, 1, D)) for j in range(6)]
        row = lambda a: a.reshape(1, -1)

        w_in_bf = w_in[l].astype(BF16)
        w_out_bf = w_out[l].astype(BF16)
        nchunk = D_FF // FF_CHUNK
        wg = ffn_w_gu[l][:, :D_FF].astype(BF16).reshape(D, nchunk, FF_CHUNK).transpose(1, 0, 2)
        wu = ffn_w_gu[l][:, D_FF:].astype(BF16).reshape(D, nchunk, FF_CHUNK).transpose(1, 0, 2)
        wd = ffn_w_down[l].astype(BF16).reshape(nchunk, FF_CHUNK, D)

        conv_args = (hy_conv_w[l], row(hy_conv_b[l]), lru_conv_w[l], row(lru_conv_b[l]))
        hv, hx1, hx2, q, k, v, lu, lg = _inproj(x, lat[1], lat[0], row(g_mix_pre[l]), w_in_bf, *conv_args, tm=tm)
        cv, cx1, cx2, cq, ck, cvv, clu, clg = _inproj(xc, cx[1], cx[0], row(g_mix_pre[l]), w_in_bf, *conv_args,
                                                      tm=Lc)

        filt = (hy_f_w1[l], hy_f_b1[l], hy_f_w2[l], hy_f_b2[l], hy_f_w3[l], hy_f_freq[l])
        y_hy = _hyena(hv, hx1, hx2, filt, hy_bias[l])
        y_na = _na(q, k, v, ck, cvv, _na_bias_table(na_rpb[l]))
        y_lru, yc_lru = _lru(lu, lg, clu, clg, lru_wa[l], lru_ba[l], lru_wi[l], lru_bi[l], lru_lam[l])

        ffn_args = (row(g_ffn_pre[l]), row(g_ffn_post[l]), wg, wu, wd)
        x = _out_ffn(y_hy, y_na, y_lru, w_out_bf, x, row(g_mix_post[l]), lat[2], lat[4], lat[3], lat[5], *ffn_args,
                     tm=tm)

        if with_ctx_out:
            yc_hy = _hyena_ctx(cv, cx1, cx2, filt, hy_bias[l])
            yc_na = _ctx_attn(cq, ck, cvv)
            xc = _out_ffn(yc_hy, yc_na, yc_lru, w_out_bf, xc, row(g_mix_post[l]), cx[2], cx[4], cx[3], cx[5],
                          *ffn_args, tm=Lc)
    return x
```

```python
import functools
import math

import jax
import jax.numpy as jnp
import numpy as np
from jax import lax
from jax.experimental import pallas as pl
from jax.experimental.pallas import tpu as pltpu

F32 = jnp.float32
BF16 = jnp.bfloat16

D_MODEL = 1024
DEPTH = 2
GRID_W = 64
HY_WIDTH = D_MODEL // 4
NA_HEAD_DIM = 64
NA_WIDTH = D_MODEL // 2
NA_HEADS = NA_WIDTH // NA_HEAD_DIM
LRU_WIDTH = D_MODEL // 4
LRU_BLOCKS = 4
IN_WIDTH = 3 * HY_WIDTH + 3 * NA_WIDTH + 2 * LRU_WIDTH
HY_ORDER = 2
HY_BANDS = 16
HY_FAST_DECAY = 0.3
HY_SLOW_DECAY = 1.5
HY_DECAY_TARGET = 1e-2
NA_WIN_ROWS = 8
NA_WIN_COLS = 16
LRU_C = 8.0
D_FF = -(-8 * D_MODEL // (3 * 256)) * 256
RMS_EPS = 1e-6
NEG_INF = -1e30

_HY_END = 3 * HY_WIDTH
_NA_END = _HY_END + 3 * NA_WIDTH
_LRU_MID = _NA_END + LRU_WIDTH

HALO = 8
SUBLANES = 8
VMEM_LIMIT = 48 * 1024 * 1024


def _rms(x, g):
    return x * lax.rsqrt(jnp.mean(x * x, axis=-1, keepdims=True) + RMS_EPS) * g


def _mod_kernel(ct_ref, w_ref, b_ref, o_ref, *, n_cond):
    ct = ct_ref[...]
    st = ct * jax.nn.sigmoid(ct)
    w = w_ref[0]
    rows = [jnp.sum(w * st[:, r:r + 1], axis=0, keepdims=True) for r in range(n_cond)]
    rows.append(jnp.zeros((SUBLANES - n_cond, w.shape[1]), F32))
    o_ref[0] = jnp.concatenate(rows, axis=0) + b_ref[0]


def _modulation(cond_t, n_cond, ada_w, ada_b):
    tn = 768
    n = ada_w.shape[-1]
    return pl.pallas_call(
        functools.partial(_mod_kernel, n_cond=n_cond),
        out_shape=jax.ShapeDtypeStruct((DEPTH, SUBLANES, n), F32),
        grid=(DEPTH, n // tn),
        in_specs=[pl.BlockSpec((D_MODEL, SUBLANES), lambda l, j: (0, 0)),
                  pl.BlockSpec((1, D_MODEL, tn), lambda l, j: (l, 0, j)),
                  pl.BlockSpec((1, 1, tn), lambda l, j: (l, 0, j))],
        out_specs=pl.BlockSpec((1, SUBLANES, tn), lambda l, j: (l, 0, j)),
        compiler_params=pltpu.CompilerParams(vmem_limit_bytes=VMEM_LIMIT),
        name="adaln_modulation",
    )(cond_t, ada_w, ada_b.reshape(DEPTH, 1, n))


CM_LANES = 256


def _chan_major_shape(B, L, C):
    return (B, C // SUBLANES, L // CM_LANES, SUBLANES, CM_LANES)


def _chan_major_spec(tm, C):
    return pl.BlockSpec((1, C // SUBLANES, tm // CM_LANES, SUBLANES, CM_LANES), lambda b, i: (b, 0, i, 0, 0))


def _store_chan_major(ref, u):
    ut = u.T
    for g in range(u.shape[1] // SUBLANES):
        for j in range(u.shape[0] // CM_LANES):
            ref[0, g, j] = ut[g * SUBLANES:(g + 1) * SUBLANES, j * CM_LANES:(j + 1) * CM_LANES]


def _load_chan_major(ref):
    _, ng, nj, _, _ = ref.shape
    rows = [jnp.concatenate([ref[0, g, j] for j in range(nj)], axis=1) for g in range(ng)]
    return jnp.concatenate(rows, axis=0).T


def _inproj_kernel(xp_ref, xc_ref, xn_ref, sc_ref, sh_ref, g_ref, w_ref, hcw_ref, hcb_ref, lcw_ref, lcb_ref,
                   hv_ref, hx1_ref, hx2_ref, q_ref, k_ref, v_ref, lu_ref, lg_ref, pe_ref, *, tm, chan_major):
    i = pl.program_id(1)
    last = pl.num_programs(1) - 1
    g = g_ref[...]
    sc1 = 1.0 + sc_ref[0]
    sh = sh_ref[0]

    def norm_mod(xv):
        return _rms(xv, g) * sc1 + sh

    hp = norm_mod(xp_ref[0]) * (i > 0).astype(F32)
    hn = norm_mod(xn_ref[0]) * (i < last).astype(F32)
    he = jnp.concatenate([hp, norm_mod(xc_ref[0]), hn], axis=0).astype(BF16)

    pe_ref[:, 0:_HY_END] = jnp.dot(he, w_ref[:, 0:_HY_END], preferred_element_type=F32)
    pe_ref[:, _HY_END:] = jnp.dot(he, w_ref[:, _NA_END:_LRU_MID], preferred_element_type=F32)
    hc = he[HALO:HALO + tm]
    qkv = jnp.dot(hc, w_ref[:, _HY_END:_NA_END], preferred_element_type=F32)
    q_ref[0] = (qkv[:, 0:NA_WIDTH] * (NA_HEAD_DIM ** -0.5)).astype(BF16)
    k_ref[0] = qkv[:, NA_WIDTH:2 * NA_WIDTH].astype(BF16)
    v_ref[0] = qkv[:, 2 * NA_WIDTH:].astype(BF16)
    lg_ref[0] = jnp.dot(hc, w_ref[:, _LRU_MID:], preferred_element_type=F32)

    u = hcb_ref[...]
    for kk in range(3):
        u = u + hcw_ref[kk:kk + 1, :] * pe_ref[pl.ds(HALO - 1 + kk, tm), 0:_HY_END]
    for n, ref in enumerate((hv_ref, hx1_ref, hx2_ref)):
        un = u[:, n * HY_WIDTH:(n + 1) * HY_WIDTH]
        if chan_major:
            _store_chan_major(ref, un)
        else:
            ref[0] = un
    ul = lcb_ref[...]
    for kk in range(4):
        ul = ul + lcw_ref[kk:kk + 1, :] * pe_ref[pl.ds(HALO - 2 + kk, tm), _HY_END:]
    lu_ref[0] = ul


def _inproj(x, sc, sh, g, w_bf, hcw, hcb, lcw, lcb, *, tm, chan_major):
    B, L, D = x.shape
    nb = tm // HALO
    nh = L // HALO
    tok = lambda w, dt: jax.ShapeDtypeStruct((B, L, w), dt)
    tspec = lambda w: pl.BlockSpec((1, tm, w), lambda b, i: (b, i, 0))
    full = lambda a: pl.BlockSpec(a.shape, lambda b, i: (0,) * a.ndim)
    vec = pl.BlockSpec((1, 1, D), lambda b, i: (b, 0, 0))
    if chan_major:
        hy_shape = jax.ShapeDtypeStruct(_chan_major_shape(B, L, HY_WIDTH), F32)
        hy_spec = _chan_major_spec(tm, HY_WIDTH)
    else:
        hy_shape, hy_spec = tok(HY_WIDTH, F32), tspec(HY_WIDTH)
    return pl.pallas_call(
        functools.partial(_inproj_kernel, tm=tm, chan_major=chan_major),
        out_shape=[hy_shape] * 3 + [tok(NA_WIDTH, BF16)] * 3 + [tok(LRU_WIDTH, F32)] * 2,
        grid=(B, L // tm),
        in_specs=[pl.BlockSpec((1, HALO, D), lambda b, i: (b, jnp.maximum(i * nb - 1, 0), 0)),
                  pl.BlockSpec((1, tm, D), lambda b, i: (b, i, 0)),
                  pl.BlockSpec((1, HALO, D), lambda b, i: (b, jnp.minimum((i + 1) * nb, nh - 1), 0)),
                  vec, vec, full(g), full(w_bf), full(hcw), full(hcb), full(lcw), full(lcb)],
        out_specs=[hy_spec] * 3 + [tspec(NA_WIDTH)] * 3 + [tspec(LRU_WIDTH)] * 2,
        scratch_shapes=[pltpu.VMEM((tm + 2 * HALO, _HY_END + LRU_WIDTH), F32)],
        compiler_params=pltpu.CompilerParams(vmem_limit_bytes=VMEM_LIMIT),
        name="inproj",
    )(x, x, x, sc, sh, g, w_bf, hcw, hcb, lcw, lcb)


FF_CHUNK = 256


def _out_ffn_kernel(yh_ref, yn_ref, yl_ref, wo_ref, x_ref, gm_ref, gtm_ref, sc_ref, sh_ref, gt_ref, gpre_ref,
                    gpost_ref, wg_ref, wu_ref, wd_ref, o_ref, *, chan_major):
    yh = _load_chan_major(yh_ref) if chan_major else yh_ref[0]
    y = jnp.dot(yh.astype(BF16), wo_ref[0:HY_WIDTH], preferred_element_type=F32)
    y = y + jnp.dot(yn_ref[0], wo_ref[HY_WIDTH:HY_WIDTH + NA_WIDTH], preferred_element_type=F32)
    y = y + jnp.dot(yl_ref[0], wo_ref[HY_WIDTH + NA_WIDTH:], preferred_element_type=F32)
    x = x_ref[0] + gtm_ref[0] * _rms(y, gm_ref[...])
    h = (_rms(x, gpre_ref[...]) * (1.0 + sc_ref[0]) + sh_ref[0]).astype(BF16)
    acc = jnp.zeros(x.shape, F32)
    for j in range(D_FF // FF_CHUNK):
        g = jnp.dot(h, wg_ref[j], preferred_element_type=F32)
        u = jnp.dot(h, wu_ref[j], preferred_element_type=F32)
        a = (g * jax.nn.sigmoid(g) * u).astype(BF16)
        acc = acc + jnp.dot(a, wd_ref[j], preferred_element_type=F32)
    o_ref[0] = x + gt_ref[0] * _rms(acc, gpost_ref[...])


def _out_ffn(yh, yn, yl, wo, x, gm, gtm, sc, sh, gt, gpre, gpost, wg, wu, wd, *, tm, chan_major):
    B, L, D = x.shape
    tspec = lambda w: pl.BlockSpec((1, tm, w), lambda b, i: (b, i, 0))
    vec = pl.BlockSpec((1, 1, D), lambda b, i: (b, 0, 0))
    full = lambda a: pl.BlockSpec(a.shape, lambda b, i: (0,) * a.ndim)
    res = lambda a: pl.BlockSpec(a.shape, lambda b, i: (0,) * a.ndim, pipeline_mode=pl.Buffered(1))
    hy_spec = _chan_major_spec(tm, HY_WIDTH) if chan_major else tspec(HY_WIDTH)
    return pl.pallas_call(
        functools.partial(_out_ffn_kernel, chan_major=chan_major),
        out_shape=jax.ShapeDtypeStruct((B, L, D), F32),
        grid=(B, L // tm),
        in_specs=[hy_spec, tspec(NA_WIDTH), tspec(LRU_WIDTH), res(wo), tspec(D), full(gm), vec,
                  vec, vec, vec, full(gpre), full(gpost), res(wg), res(wu), res(wd)],
        out_specs=tspec(D),
        compiler_params=pltpu.CompilerParams(vmem_limit_bytes=VMEM_LIMIT),
        name="out_ffn",
    )(yh, yn, yl, wo, x, gm, gtm, sc, sh, gt, gpre, gpost, wg, wu, wd)


HY_N1 = CM_LANES
HY_N2 = 128
HY_FILT_COLS = 2048


def _block_cplx(re, im):
    return np.block([[re, -im], [im, re]])


@functools.lru_cache(maxsize=None)
def _dft_consts():
    n = HY_N1 * HY_N2
    h = HY_N2 // 2
    i1 = np.arange(HY_N1, dtype=np.float64)
    i2 = np.arange(HY_N2, dtype=np.float64)
    a1 = 2.0 * np.pi * np.outer(i1, i1) / HY_N1
    a2 = 2.0 * np.pi * np.outer(i2, i2) / HY_N2
    at = 2.0 * np.pi * np.outer(i2, i1) / n
    r2, m2 = np.cos(a2), -np.sin(a2)
    r1, m1 = np.cos(a1), -np.sin(a1)
    f32 = lambda m: np.asarray(m, np.float32)
    return dict(
        rows_data=f32(_block_cplx(r2[:, :h], m2[:, :h])),
        rows_filt=f32(np.concatenate([r2, m2], axis=0)),
        rows_out=f32(_block_cplx(r2[:h, :], m2[:h, :])),
        lanes=f32(np.block([[r1, m1], [-m1, r1]])),
        twr=f32(np.cos(at)), twi=f32(-np.sin(at)))


def _dft_const(name):
    return jnp.asarray(_dft_consts()[name], F32)


def _cmul(ar, ai, br, bi):
    return ar * br - ai * bi, ar * bi + ai * br


def _hy_conv_kernel(v_ref, x1_ref, x2_ref, k0_ref, k1_ref, ks0_ref, ks1_ref, bias_ref, rd_ref, rf_ref, ro_ref,
                    ln_ref, twr_ref, twi_ref, o_ref, st_ref, ks_ref):
    g = pl.program_id(0)
    n_ch = SUBLANES
    h = HY_N2 // 2
    rows_data = rd_ref[...].astype(BF16)
    rows_filt = rf_ref[...].astype(BF16)
    rows_out = ro_ref[...].astype(BF16)
    lanes = ln_ref[...].astype(BF16)
    twr, twi = twr_ref[...], twi_ref[...]

    def rows_then_twiddle(mat, x, c):
        a = jnp.dot(mat, x.astype(BF16), preferred_element_type=F32)
        ar, ai = _cmul(a[:HY_N2], a[HY_N2:], twr, twi)
        st_ref[c * HY_N2:(c + 1) * HY_N2, 0:HY_N1] = ar.astype(BF16)
        st_ref[c * HY_N2:(c + 1) * HY_N2, HY_N1:] = ai.astype(BF16)

    for o, (k_ref, s_ref) in enumerate(((k0_ref, ks0_ref), (k1_ref, ks1_ref))):
        for c in range(n_ch):
            rows_then_twiddle(rows_filt, k_ref[0, :, c, :], c)
        spec = jnp.dot(st_ref[...], lanes, preferred_element_type=F32)
        inv = 1.0 / (jnp.sum(s_ref[...], axis=1, keepdims=True) * float(HY_N1 * HY_N2))
        for c in range(n_ch):
            ks_ref[o, c * HY_N2:(c + 1) * HY_N2, :] = spec[c * HY_N2:(c + 1) * HY_N2] * inv[c:c + 1, :]

    z = [[v_ref[b, 0, :, c, :] for c in range(n_ch)] for b in range(2)]
    for o, gate_ref in enumerate((x1_ref, x2_ref)):
        for c in range(n_ch):
            rows_then_twiddle(rows_data, jnp.concatenate([z[0][c], z[1][c]], axis=0), c)
        x = jnp.dot(st_ref[...], lanes, preferred_element_type=F32)
        pr, pi = _cmul(x[:, :HY_N1], x[:, HY_N1:], ks_ref[o, :, 0:HY_N1], ks_ref[o, :, HY_N1:])
        st_ref[:, 0:HY_N1] = pr.astype(BF16)
        st_ref[:, HY_N1:] = (-pi).astype(BF16)
        y = jnp.dot(st_ref[...], lanes, preferred_element_type=F32)
        for c in range(n_ch):
            yc = y[c * HY_N2:(c + 1) * HY_N2]
            yr, yi = _cmul(yc[:, :HY_N1], yc[:, HY_N1:], twr, twi)
            out = jnp.dot(rows_out, jnp.concatenate([yr, yi], axis=0).astype(BF16),
                          preferred_element_type=F32)
            bias = bias_ref[o, g * n_ch + c]
            for b, conv in ((0, out[:h]), (1, -out[h:])):
                z[b][c] = gate_ref[b, 0, :, c, :] * (conv + bias * z[b][c])
    for b in range(2):
        for c in range(n_ch):
            o_ref[b, 0, :, c, :] = z[b][c]


def _hy_conv(v, x1, x2, kc, ksum, hy_bias):
    B, ng = v.shape[0], v.shape[1]
    assert B == 2 and v.shape[2] * 2 == HY_N2 and kc.shape[1] == HY_N2
    consts = [_dft_const(n) for n in ("rows_data", "rows_filt", "rows_out", "lanes", "twr", "twi")]
    tok = pl.BlockSpec((B, 1) + v.shape[2:], lambda g: (0, g, 0, 0, 0))
    kspec = lambda o: pl.BlockSpec((1,) + kc.shape[1:], lambda g: (o * ng + g, 0, 0, 0))
    sspec = lambda o: pl.BlockSpec((SUBLANES, ksum.shape[1]), lambda g: (o * ng + g, 0))
    full = lambda a: pl.BlockSpec(a.shape, lambda g: (0,) * a.ndim)
    return pl.pallas_call(
        _hy_conv_kernel,
        out_shape=jax.ShapeDtypeStruct(v.shape, F32),
        grid=(ng,),
        in_specs=[tok, tok, tok, kspec(0), kspec(1), sspec(0), sspec(1),
                  pl.BlockSpec(memory_space=pltpu.SMEM)] + [full(a) for a in consts],
        out_specs=tok,
        scratch_shapes=[pltpu.VMEM((SUBLANES * HY_N2, 2 * HY_N1), BF16),
                        pltpu.VMEM((HY_ORDER, SUBLANES * HY_N2, 2 * HY_N1), F32)],
        compiler_params=pltpu.CompilerParams(vmem_limit_bytes=VMEM_LIMIT),
        name="hy_conv",
    )(v, x1, x2, kc, kc, ksum, ksum, hy_bias, *consts)


def _hy_filter_kernel(w1t_ref, w1c_ref, w1s_ref, b1_ref, f0_ref, w2_ref, b2_ref, f1_ref, w3_ref, dl_ref,
                      k_ref, s_ref, *, L, P, chan_major):
    i = pl.program_id(0)

    def pos_t(rows):
        n = i * P + lax.broadcasted_iota(jnp.int32, (rows, P), 1)
        return n, jnp.where(n < L, n, 2 * L - n).astype(F32) / L

    _, tb = pos_t(HY_BANDS)
    band = (lax.broadcasted_iota(jnp.int32, tb.shape, 0) + 1).astype(F32)
    ang = 2.0 * math.pi * tb * band
    _, t1 = pos_t(1)
    h = w1t_ref[...] * t1 + b1_ref[...]
    h = h + jnp.dot(w1c_ref[...], jnp.cos(ang).astype(BF16), preferred_element_type=F32)
    h = h + jnp.dot(w1s_ref[...], jnp.sin(ang).astype(BF16), preferred_element_type=F32)
    h = jnp.sin(f0_ref[...] * h)
    h = jnp.sin(f1_ref[...] * (jnp.dot(w2_ref[...], h.astype(BF16), preferred_element_type=F32) + b2_ref[...]))
    k = jnp.dot(w3_ref[0], h.astype(BF16), preferred_element_type=F32)
    n, t = pos_t(1)
    k = jnp.where(n == L, 0.0, k * jnp.exp(-t * dl_ref[...]))
    if chan_major:
        for j in range(P // CM_LANES):
            k_ref[:, j] = k[:, j * CM_LANES:(j + 1) * CM_LANES].reshape(k_ref.shape[0], SUBLANES, CM_LANES)
    else:
        k_ref[...] = k
    ka = jnp.abs(k)
    tot = ka[:, 0:128]
    for j in range(1, P // 128):
        tot = tot + ka[:, j * 128:(j + 1) * 128]

    @pl.when(i == 0)
    def _():
        s_ref[...] = jnp.zeros_like(s_ref)

    s_ref[...] += tot


def _hy_filter(L, P, chan_major, f_w1, f_b1, f_w2, f_b2, f_w3, f_freq):
    nb = 2 * L // P
    assert nb % 2 == 0 and P % CM_LANES == 0
    hid = f_w2.shape[0]
    oc = HY_ORDER * HY_WIDTH
    col = lambda v: v.reshape(-1, 1)
    w3 = f_w3.reshape(hid, HY_ORDER, 2, HY_WIDTH)
    w3 = jnp.stack([w3[:, :, d].reshape(hid, oc).T for d in range(2)]).astype(BF16)
    deltas = np.abs(np.linspace(math.log(HY_DECAY_TARGET) / HY_FAST_DECAY,
                                math.log(HY_DECAY_TARGET) / HY_SLOW_DECAY, HY_WIDTH, dtype=np.float32))
    ins = [col(f_w1[0]), f_w1[1:1 + HY_BANDS].T.astype(BF16), f_w1[1 + HY_BANDS:].T.astype(BF16), col(f_b1),
           col(f_freq[0]), f_w2.T.astype(BF16), col(f_b2), col(f_freq[1]), w3,
           jnp.asarray(np.tile(deltas, HY_ORDER)[:, None])]
    full = lambda a: pl.BlockSpec(a.shape, lambda i: (0,) * a.ndim)
    specs = [full(a) for a in ins]
    specs[8] = pl.BlockSpec((1,) + w3.shape[1:], lambda i: (i // (nb // 2), 0, 0))
    if chan_major:
        kshape = (oc // SUBLANES, 2 * L // CM_LANES, SUBLANES, CM_LANES)
        kspec = pl.BlockSpec((oc // SUBLANES, P // CM_LANES, SUBLANES, CM_LANES), lambda i: (0, i, 0, 0))
    else:
        kshape = (oc, 2 * L)
        kspec = pl.BlockSpec((oc, P), lambda i: (0, i))
    return pl.pallas_call(
        functools.partial(_hy_filter_kernel, L=L, P=P, chan_major=chan_major),
        out_shape=[jax.ShapeDtypeStruct(kshape, F32), jax.ShapeDtypeStruct((oc, 128), F32)],
        grid=(nb,),
        in_specs=specs,
        out_specs=[kspec, pl.BlockSpec((oc, 128), lambda i: (0, 0))],
        compiler_params=pltpu.CompilerParams(dimension_semantics=("arbitrary",), vmem_limit_bytes=VMEM_LIMIT),
        name="hy_filter",
    )(*ins)


def _hyena(v, x1, x2, filt, hy_bias):
    L = v.shape[2] * CM_LANES
    assert 2 * L == HY_N1 * HY_N2
    kc, ksum = _hy_filter(L, HY_FILT_COLS, True, *filt)
    return _hy_conv(v, x1, x2, kc, ksum, hy_bias)


@functools.lru_cache(maxsize=None)
def _ctx_dft_consts(n):
    i = np.arange(n, dtype=np.float64)
    a = 2.0 * np.pi * np.outer(i, i) / n
    f = np.concatenate([np.cos(a), -np.sin(a)], axis=0)
    return np.asarray(f, np.float32), np.asarray(f.T[:n // 2], np.float32)


def _hyena_ctx_kernel(v_ref, x1_ref, x2_ref, kc_ref, ks_ref, bias_ref, ff_ref, fi_ref, o_ref):
    n = ff_ref.shape[1]
    lc = n // 2
    C = v_ref.shape[2]
    ff = ff_ref[...].astype(BF16)
    inv = 1.0 / (jnp.sum(ks_ref[...], axis=1, keepdims=True) * float(n))
    kn = (kc_ref[...] * inv).T
    kspec = jnp.dot(ff, kn.astype(BF16), preferred_element_type=F32)
    z = jnp.concatenate([v_ref[0], v_ref[1]], axis=1)
    for o, g_ref in enumerate((x1_ref, x2_ref)):
        x = jnp.dot(ff[:, :lc], z.astype(BF16), preferred_element_type=F32)
        k = kspec[:, o * C:(o + 1) * C]
        kr = jnp.concatenate([k[:n], k[:n]], axis=1)
        ki = jnp.concatenate([k[n:], k[n:]], axis=1)
        yr, yi = _cmul(x[:n], x[n:], kr, ki)
        y = jnp.concatenate([yr, yi], axis=0)
        conv = jnp.dot(fi_ref[...].astype(BF16), y.astype(BF16), preferred_element_type=F32)
        bias = jnp.concatenate([bias_ref[o:o + 1, :], bias_ref[o:o + 1, :]], axis=1)
        z = jnp.concatenate([g_ref[0], g_ref[1]], axis=1) * (conv + bias * z)
    o_ref[0] = z[:, :C]
    o_ref[1] = z[:, C:]


def _hyena_ctx(v, x1, x2, filt, hy_bias):
    B, Lc, C = v.shape
    assert B == 2
    kc, ksum = _hy_filter(Lc, Lc, False, *filt)
    ff, fi = (jnp.asarray(m, F32) for m in _ctx_dft_consts(2 * Lc))
    return pl.pallas_call(
        _hyena_ctx_kernel,
        out_shape=jax.ShapeDtypeStruct((B, Lc, C), F32),
        compiler_params=pltpu.CompilerParams(vmem_limit_bytes=VMEM_LIMIT),
        name="hyena_ctx",
    )(v, x1, x2, kc, ksum, hy_bias, ff, fi)


NA_ROWS_PER_STEP = 8
_NT = (((1,), (1,)), ((), ()))


def _na_bias_table(rpb):
    qc = np.arange(GRID_W)[:, None]
    kc = np.arange(GRID_W)[None, :]
    start = np.clip(qc - NA_WIN_COLS // 2, 0, GRID_W - NA_WIN_COLS)
    valid = (kc >= start) & (kc < start + NA_WIN_COLS)
    pad = jnp.pad(rpb, ((0, 0), (0, 0), (GRID_W, GRID_W)))
    shift = GRID_W + NA_WIN_COLS - 1
    toep = jnp.stack([pad[:, :, shift - c:shift - c + GRID_W] for c in range(GRID_W)], axis=2)
    full = jnp.where(jnp.asarray(valid)[None, None], toep, NEG_INF)
    t = jnp.stack([full[:, d:d + NA_WIN_ROWS] for d in range(NA_WIN_ROWS)], axis=1)
    t = t.reshape(NA_HEADS // 2, 2, NA_WIN_ROWS, NA_WIN_ROWS, GRID_W, GRID_W).transpose(0, 2, 1, 4, 3, 5)
    return t.reshape(NA_HEADS // 2, NA_WIN_ROWS, 2 * GRID_W, NA_WIN_ROWS * GRID_W).astype(F32)


def _na_kernel(q_ref, kp_ref, kc_ref, kn_ref, vp_ref, vc_ref, vn_ref, ck_ref, cv_ref, bias_ref, o_ref,
               wk_ref, wv_ref, sc_ref, pc_ref, ol_ref, li_ref):
    i = pl.program_id(2)
    last = pl.num_programs(2) - 1
    blk = NA_ROWS_PER_STEP * GRID_W
    for n, (kr, vr) in enumerate(((kp_ref, vp_ref), (kc_ref, vc_ref), (kn_ref, vn_ref))):
        wk_ref[n * blk:(n + 1) * blk] = kr[0]
        wv_ref[n * blk:(n + 1) * blk] = vr[0]
    first_head = lax.broadcasted_iota(jnp.int32, (GRID_W, 2 * NA_HEAD_DIM), 1) < NA_HEAD_DIM
    q = q_ref[0]
    zero = jnp.zeros((GRID_W, 2 * NA_HEAD_DIM), q.dtype)
    pieces = []
    for j in range(NA_ROWS_PER_STEP):
        qj = q[j * GRID_W:(j + 1) * GRID_W]
        pieces += [jnp.where(first_head, qj, zero), jnp.where(first_head, zero, qj)]
    qs = jnp.concatenate(pieces, axis=0)
    sc_ref[...] = lax.dot_general(qs, ck_ref[0], _NT, preferred_element_type=F32)
    half = NA_WIN_ROWS // 2
    rows = 2 * GRID_W
    for j in range(NA_ROWS_PER_STEP):
        off = jnp.where(i == 0, max(j + half, NA_ROWS_PER_STEP),
                        jnp.where(i == last, min(j + half, NA_ROWS_PER_STEP), j + half))
        d = off - j - 1
        start = pl.multiple_of(off * GRID_W, GRID_W)
        kw = wk_ref[pl.ds(start, NA_WIN_ROWS * GRID_W), :]
        vw = wv_ref[pl.ds(start, NA_WIN_ROWS * GRID_W), :]
        r0 = j * rows
        s_loc = lax.dot_general(qs[r0:r0 + rows], kw, _NT, preferred_element_type=F32) + bias_ref[0, d]
        s_ctx = sc_ref[r0:r0 + rows, :]
        m = jnp.maximum(jnp.max(s_loc, axis=1, keepdims=True), jnp.max(s_ctx, axis=1, keepdims=True))
        p_loc = jnp.exp(s_loc - m)
        p_ctx = jnp.exp(s_ctx - m)
        l = jnp.sum(p_loc, axis=1, keepdims=True) + jnp.sum(p_ctx, axis=1, keepdims=True)
        pc_ref[r0:r0 + rows, :] = p_ctx.astype(pc_ref.dtype)
        ol_ref[r0:r0 + rows, :] = jnp.dot(p_loc.astype(BF16), vw, preferred_element_type=F32)
        li_ref[r0:r0 + rows, :] = jnp.broadcast_to(1.0 / l, (rows, 2 * NA_HEAD_DIM))
    o = (ol_ref[...] + jnp.dot(pc_ref[...], cv_ref[0], preferred_element_type=F32)) * li_ref[...]
    for j in range(NA_ROWS_PER_STEP):
        r0 = j * rows
        oj = jnp.where(first_head, o[r0:r0 + GRID_W], o[r0 + GRID_W:r0 + rows])
        o_ref[0, j * GRID_W:(j + 1) * GRID_W, :] = oj.astype(o_ref.dtype)


def _na(q, k, v, ck, cv, bias):
    B, L, W = q.shape
    Lc = ck.shape[1]
    blk = NA_ROWS_PER_STEP * GRID_W
    nblk = L // blk
    assert NA_ROWS_PER_STEP == NA_WIN_ROWS and nblk >= 2
    pw = 2 * NA_HEAD_DIM
    cur = pl.BlockSpec((1, blk, pw), lambda b, h, i: (b, i, h))
    prev = pl.BlockSpec((1, blk, pw), lambda b, h, i: (b, jnp.maximum(i - 1, 0), h))
    nxt = pl.BlockSpec((1, blk, pw), lambda b, h, i: (b, jnp.minimum(i + 1, nblk - 1), h))
    cspec = pl.BlockSpec((1, Lc, pw), lambda b, h, i: (b, 0, h))
    bspec = pl.BlockSpec((1,) + bias.shape[1:], lambda b, h, i: (h, 0, 0, 0))
    stacked = 2 * blk
    return pl.pallas_call(
        _na_kernel,
        out_shape=jax.ShapeDtypeStruct((B, L, W), BF16),
        grid=(B, W // pw, nblk),
        in_specs=[cur, prev, cur, nxt, prev, cur, nxt, cspec, cspec, bspec],
        out_specs=cur,
        scratch_shapes=[pltpu.VMEM((3 * blk, pw), BF16), pltpu.VMEM((3 * blk, pw), BF16),
                        pltpu.VMEM((stacked, Lc), F32), pltpu.VMEM((stacked, Lc), BF16),
                        pltpu.VMEM((stacked, pw), F32), pltpu.VMEM((stacked, pw), F32)],
        compiler_params=pltpu.CompilerParams(vmem_limit_bytes=VMEM_LIMIT),
        name="nattn",
    )(q, k, k, k, v, v, v, ck, cv, bias)


def _ctx_attn_kernel(q_ref, k_ref, v_ref, o_ref):
    q = q_ref[0]
    k = k_ref[0]
    v = v_ref[0]
    first_head = lax.broadcasted_iota(jnp.int32, q.shape, 1) < NA_HEAD_DIM
    outs = []
    for h in range(2):
        qm = jnp.where(first_head if h == 0 else jnp.logical_not(first_head), q, jnp.zeros_like(q))
        s = lax.dot_general(qm, k, _NT, preferred_element_type=F32)
        p = jnp.exp(s - jnp.max(s, axis=1, keepdims=True))
        o = jnp.dot(p.astype(BF16), v, preferred_element_type=F32)
        outs.append(o / jnp.sum(p, axis=1, keepdims=True))
    o_ref[0] = jnp.where(first_head, outs[0], outs[1]).astype(o_ref.dtype)


def _ctx_attn(q, k, v):
    B, Lc, W = q.shape
    pw = 2 * NA_HEAD_DIM
    spec = pl.BlockSpec((1, Lc, pw), lambda b, h: (b, 0, h))
    return pl.pallas_call(
        _ctx_attn_kernel,
        out_shape=jax.ShapeDtypeStruct((B, Lc, W), BF16),
        grid=(B, W // pw),
        in_specs=[spec, spec, spec],
        out_specs=spec,
        name="ctx_attn",
    )(q, k, v)


LRU_CHUNK = 512


def _lru_gate_weights(wa, ba, wi, bi, lam):
    def bd(w):
        return jax.scipy.linalg.block_diag(*[w[n] for n in range(LRU_BLOCKS)])
    wg = jnp.stack([jnp.concatenate([bd(wa[d]), bd(wi[d])], axis=1) for d in range(2)]).astype(BF16)
    bg = jnp.stack([jnp.concatenate([ba[d], bi[d]])[None, :] for d in range(2)])
    return wg, bg, lam[:, None, :]


def _lru_coeffs(u, wg, bg, lam):
    C = u.shape[1]
    g = jnp.dot(u.astype(BF16), wg, preferred_element_type=F32) + bg
    r = jax.nn.sigmoid(g[:, :C])
    ig = jax.nn.sigmoid(g[:, C:])
    nl = -lam
    softplus = jnp.maximum(nl, 0.0) + jnp.log(1.0 + jnp.exp(-jnp.abs(nl)))
    log_a = (-LRU_C * softplus) * r
    a = jnp.exp(log_a)
    b = jnp.sqrt(1.0 - jnp.exp(2.0 * log_a)) * (ig * u)
    return a, b


def _lru_scan(a, b, h0, reverse, ac_ref, bc_ref, h_ref):
    T, C = a.shape
    row = lax.broadcasted_iota(jnp.int32, a.shape, 0) % SUBLANES
    for s in (1, 2, 4):
        shift = T - s if reverse else s
        keep = (row < SUBLANES - s) if reverse else (row >= s)
        b = jnp.where(keep, a * pltpu.roll(b, shift, 0) + b, b)
        a = jnp.where(keep, a * pltpu.roll(a, shift, 0), a)
    ac_ref[...] = a
    bc_ref[...] = b
    ng = T // SUBLANES

    def group(g, h):
        r0 = pl.multiple_of((ng - 1 - g if reverse else g) * SUBLANES, SUBLANES)
        hr = ac_ref[pl.ds(r0, SUBLANES), :] * h + bc_ref[pl.ds(r0, SUBLANES), :]
        h_ref[pl.ds(r0, SUBLANES), :] = hr
        edge = hr[0:1] if reverse else hr[SUBLANES - 1:SUBLANES]
        return jnp.broadcast_to(edge, (SUBLANES, C))

    return lax.fori_loop(0, ng, group, h0, unroll=4)


def _gelu_tanh(x):
    return 0.5 * x * (1.0 + jnp.tanh(math.sqrt(2.0 / math.pi) * (x + 0.044715 * (x * x * x))))


def _lru_ctx_kernel(u_ref, xg_ref, wg_ref, bg_ref, lam_ref, hend_ref, yc_ref, ac_ref, bc_ref, h_ref):
    u = u_ref[0]
    C = u.shape[1]
    total = jnp.zeros_like(u)
    for d, rev in enumerate((False, True)):
        a, b = _lru_coeffs(u, wg_ref[d], bg_ref[d], lam_ref[d])
        hl = _lru_scan(a, b, jnp.zeros((SUBLANES, C), F32), rev, ac_ref, bc_ref, h_ref)
        hend_ref[0, d:d + 1, :] = hl[0:1]
        total = total + h_ref[...]
    yc_ref[0] = (total * _gelu_tanh(xg_ref[0])).astype(yc_ref.dtype)


def _lru_ctx(u, xg, wg, bg, lam):
    B, Lc, C = u.shape
    tok = pl.BlockSpec((1, Lc, C), lambda b: (b, 0, 0))
    full = lambda a: pl.BlockSpec(a.shape, lambda b: (0,) * a.ndim)
    return pl.pallas_call(
        _lru_ctx_kernel,
        out_shape=[jax.ShapeDtypeStruct((B, 2, C), F32), jax.ShapeDtypeStruct((B, Lc, C), BF16)],
        grid=(B,),
        in_specs=[tok, tok, full(wg), full(bg), full(lam)],
        out_specs=[pl.BlockSpec((1, 2, C), lambda b: (b, 0, 0)), tok],
        scratch_shapes=[pltpu.VMEM((Lc, C), F32)] * 3,
        name="lru_ctx",
    )(u, xg, wg, bg, lam)


def _lru_dir_kernel(*refs, d, reverse):
    if reverse:
        u_ref, hend_ref, wg_ref, bg_ref, lam_ref, hf_ref, xg_ref, o_ref, ac_ref, bc_ref, h_ref, carry_ref = refs
    else:
        u_ref, hend_ref, wg_ref, bg_ref, lam_ref, o_ref, ac_ref, bc_ref, carry_ref = refs
        h_ref = o_ref.at[0]
    C = u_ref.shape[2]

    @pl.when(pl.program_id(1) == 0)
    def _():
        carry_ref[...] = jnp.broadcast_to(hend_ref[0, d:d + 1, :], (SUBLANES, C))

    a, b = _lru_coeffs(u_ref[0], wg_ref[d], bg_ref[d], lam_ref[d])
    carry_ref[...] = _lru_scan(a, b, carry_ref[...], reverse, ac_ref, bc_ref, h_ref)
    if reverse:
        o_ref[0] = ((hf_ref[0] + h_ref[...]) * _gelu_tanh(xg_ref[0])).astype(o_ref.dtype)


def _lru_dir(u, hend, wg, bg, lam, hf=None, xg=None):
    B, L, C = u.shape
    reverse = hf is not None
    T = LRU_CHUNK
    nb = L // T
    tok = pl.BlockSpec((1, T, C), (lambda b, i: (b, nb - 1 - i, 0)) if reverse else (lambda b, i: (b, i, 0)))
    full = lambda a: pl.BlockSpec(a.shape, lambda b, i: (0,) * a.ndim)
    ins = [u, hend, wg, bg, lam] + ([hf, xg] if reverse else [])
    specs = [tok, pl.BlockSpec((1, 2, C), lambda b, i: (b, 0, 0)), full(wg), full(bg), full(lam)]
    specs += [tok, tok] if reverse else []
    scratch = [pltpu.VMEM((T, C), F32)] * (3 if reverse else 2) + [pltpu.VMEM((SUBLANES, C), F32)]
    return pl.pallas_call(
        functools.partial(_lru_dir_kernel, d=int(reverse), reverse=reverse),
        out_shape=jax.ShapeDtypeStruct((B, L, C), BF16 if reverse else F32),
        grid=(B, nb),
        in_specs=specs,
        out_specs=tok,
        scratch_shapes=scratch,
        compiler_params=pltpu.CompilerParams(dimension_semantics=("arbitrary", "arbitrary")),
        name="lru_bwd" if reverse else "lru_fwd",
    )(*ins)


def _lru(u, xg, u_c, xg_c, wa, ba, wi, bi, lam):
    wg, bg, lam3 = _lru_gate_weights(wa, ba, wi, bi, lam)
    hend, yc = _lru_ctx(u_c, xg_c, wg, bg, lam3)
    hf = _lru_dir(u, hend, wg, bg, lam3)
    return _lru_dir(u, hend, wg, bg, lam3, hf, xg), yc


def kernel(x, c, ctx, c_ctx, ada_w, ada_b, g_mix_pre, g_mix_post, g_ffn_pre, g_ffn_post, w_in, w_out, hy_conv_w,
           hy_conv_b, hy_f_w1, hy_f_b1, hy_f_w2, hy_f_b2, hy_f_w3, hy_f_freq, hy_bias, na_rpb, lru_conv_w,
           lru_conv_b, lru_wa, lru_ba, lru_wi, lru_bi, lru_lam, ffn_w_gu, ffn_w_down):
    B, L, D = x.shape
    Lc = ctx.shape[1]
    tm = 512

    assert B + 1 <= SUBLANES
    cond_t = jnp.zeros((D, SUBLANES), F32).at[:, 0:B].set(c.T).at[:, B].set(c_ctx)
    mods = _modulation(cond_t, B + 1, ada_w, ada_b)

    xc = ctx
    for l in range(DEPTH):
        with_ctx_out = l < DEPTH - 1
        m = mods[l].reshape(8, 6, D)
        lat = [m[0:B, j][:, None, :] for j in range(6)]
        cx = [jnp.broadcast_to(m[B, j][None, None, :], (B, 1, D)) for j in range(6)]
        row = lambda a: a.reshape(1, -1)

        w_in_bf = w_in[l].astype(BF16)
        w_out_bf = w_out[l].astype(BF16)
        nchunk = D_FF // FF_CHUNK
        wg = ffn_w_gu[l][:, :D_FF].astype(BF16).reshape(D, nchunk, FF_CHUNK).transpose(1, 0, 2)
        wu = ffn_w_gu[l][:, D_FF:].astype(BF16).reshape(D, nchunk, FF_CHUNK).transpose(1, 0, 2)
        wd = ffn_w_down[l].astype(BF16).reshape(nchunk, FF_CHUNK, D)

        conv_args = (hy_conv_w[l], row(hy_conv_b[l]), lru_conv_w[l], row(lru_conv_b[l]))
        hv, hx1, hx2, q, k, v, lu, lg = _inproj(x, lat[1], lat[0], row(g_mix_pre[l]), w_in_bf, *conv_args, tm=tm,
                                                chan_major=True)
        cv, cx1, cx2, cq, ck, cvv, clu, clg = _inproj(xc, cx[1], cx[0], row(g_mix_pre[l]), w_in_bf, *conv_args,
                                                      tm=Lc, chan_major=False)

        filt = (hy_f_w1[l], hy_f_b1[l], hy_f_w2[l], hy_f_b2[l], hy_f_w3[l], hy_f_freq[l])
        y_hy = _hyena(hv, hx1, hx2, filt, hy_bias[l])
        y_na = _na(q, k, v, ck, cvv, _na_bias_table(na_rpb[l]))
        y_lru, yc_lru = _lru(lu, lg, clu, clg, lru_wa[l], lru_ba[l], lru_wi[l], lru_bi[l], lru_lam[l])

        ffn_args = (row(g_ffn_pre[l]), row(g_ffn_post[l]), wg, wu, wd)
        x = _out_ffn(y_hy, y_na, y_lru, w_out_bf, x, row(g_mix_post[l]), lat[2], lat[4], lat[3], lat[5], *ffn_args,
                     tm=tm, chan_major=True)

        if with_ctx_out:
            yc_hy = _hyena_ctx(cv, cx1, cx2, filt, hy_bias[l])
            yc_na = _ctx_attn(cq, ck, cvv)
            xc = _out_ffn(yc_hy, yc_na, yc_lru, w_out_bf, xc, row(g_mix_post[l]), cx[2], cx[4], cx[3], cx[5],
                          *ffn_args, tm=Lc, chan_major=False)
    return x
```

```python
import functools
import math

import jax
import jax.numpy as jnp
import numpy as np
from jax import lax
from jax.experimental import pallas as pl
from jax.experimental.pallas import tpu as pltpu

F32 = jnp.float32
BF16 = jnp.bfloat16

D_MODEL = 1024
DEPTH = 2
GRID_W = 64
HY_WIDTH = D_MODEL // 4
NA_HEAD_DIM = 64
NA_WIDTH = D_MODEL // 2
NA_HEADS = NA_WIDTH // NA_HEAD_DIM
LRU_WIDTH = D_MODEL // 4
LRU_BLOCKS = 4
IN_WIDTH = 3 * HY_WIDTH + 3 * NA_WIDTH + 2 * LRU_WIDTH
HY_ORDER = 2
HY_BANDS = 16
HY_FAST_DECAY = 0.3
HY_SLOW_DECAY = 1.5
HY_DECAY_TARGET = 1e-2
NA_WIN_ROWS = 8
NA_WIN_COLS = 16
LRU_C = 8.0
D_FF = -(-8 * D_MODEL // (3 * 256)) * 256
RMS_EPS = 1e-6
NEG_INF = -1e30

_HY_END = 3 * HY_WIDTH
_NA_END = _HY_END + 3 * NA_WIDTH
_LRU_MID = _NA_END + LRU_WIDTH

HALO = 8
SUBLANES = 8
VMEM_LIMIT = 48 * 1024 * 1024


def _rms(x, g):
    return x * lax.rsqrt(jnp.mean(x * x, axis=-1, keepdims=True) + RMS_EPS) * g


def _mod_kernel(ct_ref, w_ref, b_ref, o_ref, *, n_cond):
    ct = ct_ref[...]
    st = ct * jax.nn.sigmoid(ct)
    w = w_ref[0]
    rows = [jnp.sum(w * st[:, r:r + 1], axis=0, keepdims=True) for r in range(n_cond)]
    rows.append(jnp.zeros((SUBLANES - n_cond, w.shape[1]), F32))
    o_ref[0] = jnp.concatenate(rows, axis=0) + b_ref[0]


def _modulation(cond_t, n_cond, ada_w, ada_b):
    tn = 768
    n = ada_w.shape[-1]
    return pl.pallas_call(
        functools.partial(_mod_kernel, n_cond=n_cond),
        out_shape=jax.ShapeDtypeStruct((DEPTH, SUBLANES, n), F32),
        grid=(DEPTH, n // tn),
        in_specs=[pl.BlockSpec((D_MODEL, SUBLANES), lambda l, j: (0, 0)),
                  pl.BlockSpec((1, D_MODEL, tn), lambda l, j: (l, 0, j)),
                  pl.BlockSpec((1, 1, tn), lambda l, j: (l, 0, j))],
        out_specs=pl.BlockSpec((1, SUBLANES, tn), lambda l, j: (l, 0, j)),
        compiler_params=pltpu.CompilerParams(vmem_limit_bytes=VMEM_LIMIT),
        name="adaln_modulation",
    )(cond_t, ada_w, ada_b.reshape(DEPTH, 1, n))


CM_LANES = 256
LANES = 128
CM_HALVES = CM_LANES // LANES


def _chan_major_shape(B, L, C):
    return (B, C // SUBLANES, CM_HALVES, L // CM_LANES * SUBLANES, LANES)


def _chan_major_spec(tm, C):
    return pl.BlockSpec((1, C // SUBLANES, CM_HALVES, tm // CM_LANES * SUBLANES, LANES),
                        lambda b, i: (b, 0, 0, i, 0))


def _store_chan_major(ref, u):
    ut = u.T
    for g in range(u.shape[1] // SUBLANES):
        for j in range(u.shape[0] // CM_LANES):
            for h in range(CM_HALVES):
                lane0 = j * CM_LANES + h * LANES
                ref[0, g, h, j * SUBLANES:(j + 1) * SUBLANES, :] = ut[g * SUBLANES:(g + 1) * SUBLANES,
                                                                      lane0:lane0 + LANES]


def _load_chan_major(ref):
    _, ng, _, nr, _ = ref.shape
    rows = [jnp.concatenate([ref[0, g, h, j * SUBLANES:(j + 1) * SUBLANES, :]
                             for j in range(nr // SUBLANES) for h in range(CM_HALVES)], axis=1) for g in range(ng)]
    return jnp.concatenate(rows, axis=0).T


def _chan_rows(ref, lead, c, n):
    return jnp.concatenate([ref.at[lead + (h,)][pl.ds(c, n, stride=SUBLANES), :] for h in range(CM_HALVES)], axis=1)


def _inproj_kernel(xp_ref, xc_ref, xn_ref, sc_ref, sh_ref, g_ref, w_ref, hcw_ref, hcb_ref, lcw_ref, lcb_ref,
                   hv_ref, hx1_ref, hx2_ref, q_ref, k_ref, v_ref, lu_ref, lg_ref, pe_ref, *, tm, chan_major):
    i = pl.program_id(1)
    last = pl.num_programs(1) - 1
    g = g_ref[...]
    sc1 = 1.0 + sc_ref[0]
    sh = sh_ref[0]

    def norm_mod(xv):
        return _rms(xv, g) * sc1 + sh

    hp = norm_mod(xp_ref[0]) * (i > 0).astype(F32)
    hn = norm_mod(xn_ref[0]) * (i < last).astype(F32)
    he = jnp.concatenate([hp, norm_mod(xc_ref[0]), hn], axis=0).astype(BF16)

    pe_ref[:, 0:_HY_END] = jnp.dot(he, w_ref[:, 0:_HY_END], preferred_element_type=F32)
    pe_ref[:, _HY_END:] = jnp.dot(he, w_ref[:, _NA_END:_LRU_MID], preferred_element_type=F32)
    hc = he[HALO:HALO + tm]
    qkv = jnp.dot(hc, w_ref[:, _HY_END:_NA_END], preferred_element_type=F32)
    q_ref[0] = (qkv[:, 0:NA_WIDTH] * (NA_HEAD_DIM ** -0.5)).astype(BF16)
    k_ref[0] = qkv[:, NA_WIDTH:2 * NA_WIDTH].astype(BF16)
    v_ref[0] = qkv[:, 2 * NA_WIDTH:].astype(BF16)
    lg_ref[0] = jnp.dot(hc, w_ref[:, _LRU_MID:], preferred_element_type=F32)

    u = hcb_ref[...]
    for kk in range(3):
        u = u + hcw_ref[kk:kk + 1, :] * pe_ref[pl.ds(HALO - 1 + kk, tm), 0:_HY_END]
    for n, ref in enumerate((hv_ref, hx1_ref, hx2_ref)):
        un = u[:, n * HY_WIDTH:(n + 1) * HY_WIDTH]
        if chan_major:
            _store_chan_major(ref, un)
        else:
            ref[0] = un
    ul = lcb_ref[...]
    for kk in range(4):
        ul = ul + lcw_ref[kk:kk + 1, :] * pe_ref[pl.ds(HALO - 2 + kk, tm), _HY_END:]
    lu_ref[0] = ul


def _inproj(x, sc, sh, g, w_bf, hcw, hcb, lcw, lcb, *, tm, chan_major):
    B, L, D = x.shape
    nb = tm // HALO
    nh = L // HALO
    tok = lambda w, dt: jax.ShapeDtypeStruct((B, L, w), dt)
    tspec = lambda w: pl.BlockSpec((1, tm, w), lambda b, i: (b, i, 0))
    full = lambda a: pl.BlockSpec(a.shape, lambda b, i: (0,) * a.ndim)
    vec = pl.BlockSpec((1, 1, D), lambda b, i: (b, 0, 0))
    if chan_major:
        hy_shape = jax.ShapeDtypeStruct(_chan_major_shape(B, L, HY_WIDTH), F32)
        hy_spec = _chan_major_spec(tm, HY_WIDTH)
    else:
        hy_shape, hy_spec = tok(HY_WIDTH, F32), tspec(HY_WIDTH)
    return pl.pallas_call(
        functools.partial(_inproj_kernel, tm=tm, chan_major=chan_major),
        out_shape=[hy_shape] * 3 + [tok(NA_WIDTH, BF16)] * 3 + [tok(LRU_WIDTH, F32)] * 2,
        grid=(B, L // tm),
        in_specs=[pl.BlockSpec((1, HALO, D), lambda b, i: (b, jnp.maximum(i * nb - 1, 0), 0)),
                  pl.BlockSpec((1, tm, D), lambda b, i: (b, i, 0)),
                  pl.BlockSpec((1, HALO, D), lambda b, i: (b, jnp.minimum((i + 1) * nb, nh - 1), 0)),
                  vec, vec, full(g), full(w_bf), full(hcw), full(hcb), full(lcw), full(lcb)],
        out_specs=[hy_spec] * 3 + [tspec(NA_WIDTH)] * 3 + [tspec(LRU_WIDTH)] * 2,
        scratch_shapes=[pltpu.VMEM((tm + 2 * HALO, _HY_END + LRU_WIDTH), F32)],
        compiler_params=pltpu.CompilerParams(vmem_limit_bytes=VMEM_LIMIT),
        name="inproj",
    )(x, x, x, sc, sh, g, w_bf, hcw, hcb, lcw, lcb)


FF_CHUNK = 256


def _out_ffn_kernel(yh_ref, yn_ref, yl_ref, wo_ref, x_ref, gm_ref, gtm_ref, sc_ref, sh_ref, gt_ref, gpre_ref,
                    gpost_ref, wg_ref, wu_ref, wd_ref, o_ref, *, chan_major):
    yh = _load_chan_major(yh_ref) if chan_major else yh_ref[0]
    y = jnp.dot(yh.astype(BF16), wo_ref[0:HY_WIDTH], preferred_element_type=F32)
    y = y + jnp.dot(yn_ref[0], wo_ref[HY_WIDTH:HY_WIDTH + NA_WIDTH], preferred_element_type=F32)
    y = y + jnp.dot(yl_ref[0], wo_ref[HY_WIDTH + NA_WIDTH:], preferred_element_type=F32)
    x = x_ref[0] + gtm_ref[0] * _rms(y, gm_ref[...])
    h = (_rms(x, gpre_ref[...]) * (1.0 + sc_ref[0]) + sh_ref[0]).astype(BF16)
    acc = jnp.zeros(x.shape, F32)
    for j in range(D_FF // FF_CHUNK):
        g = jnp.dot(h, wg_ref[j], preferred_element_type=F32)
        u = jnp.dot(h, wu_ref[j], preferred_element_type=F32)
        a = (g * jax.nn.sigmoid(g) * u).astype(BF16)
        acc = acc + jnp.dot(a, wd_ref[j], preferred_element_type=F32)
    o_ref[0] = x + gt_ref[0] * _rms(acc, gpost_ref[...])


def _out_ffn(yh, yn, yl, wo, x, gm, gtm, sc, sh, gt, gpre, gpost, wg, wu, wd, *, tm, chan_major):
    B, L, D = x.shape
    tspec = lambda w: pl.BlockSpec((1, tm, w), lambda b, i: (b, i, 0))
    vec = pl.BlockSpec((1, 1, D), lambda b, i: (b, 0, 0))
    full = lambda a: pl.BlockSpec(a.shape, lambda b, i: (0,) * a.ndim)
    res = lambda a: pl.BlockSpec(a.shape, lambda b, i: (0,) * a.ndim, pipeline_mode=pl.Buffered(1))
    hy_spec = _chan_major_spec(tm, HY_WIDTH) if chan_major else tspec(HY_WIDTH)
    return pl.pallas_call(
        functools.partial(_out_ffn_kernel, chan_major=chan_major),
        out_shape=jax.ShapeDtypeStruct((B, L, D), F32),
        grid=(B, L // tm),
        in_specs=[hy_spec, tspec(NA_WIDTH), tspec(LRU_WIDTH), res(wo), tspec(D), full(gm), vec,
                  vec, vec, vec, full(gpre), full(gpost), res(wg), res(wu), res(wd)],
        out_specs=tspec(D),
        compiler_params=pltpu.CompilerParams(vmem_limit_bytes=VMEM_LIMIT),
        name="out_ffn",
    )(yh, yn, yl, wo, x, gm, gtm, sc, sh, gt, gpre, gpost, wg, wu, wd)


HY_N1 = CM_LANES
HY_N2 = 128
HY_FILT_COLS = 2048


def _block_cplx(re, im):
    return np.block([[re, -im], [im, re]])


@functools.lru_cache(maxsize=None)
def _dft_consts():
    n = HY_N1 * HY_N2
    h = HY_N2 // 2
    i1 = np.arange(HY_N1, dtype=np.float64)
    i2 = np.arange(HY_N2, dtype=np.float64)
    a1 = 2.0 * np.pi * np.outer(i1, i1) / HY_N1
    a2 = 2.0 * np.pi * np.outer(i2, i2) / HY_N2
    at = 2.0 * np.pi * np.outer(i2, i1) / n
    r2, m2 = np.cos(a2), -np.sin(a2)
    r1, m1 = np.cos(a1), -np.sin(a1)
    f32 = lambda m: np.asarray(m, np.float32)
    return dict(
        rows_data=f32(_block_cplx(r2[:, :h], m2[:, :h])),
        rows_filt=f32(np.concatenate([r2, m2], axis=0)),
        rows_out=f32(_block_cplx(r2[:h, :], m2[:h, :])),
        lanes=f32(np.block([[r1, m1], [-m1, r1]])),
        twr=f32(np.cos(at)), twi=f32(-np.sin(at)))


def _dft_const(name):
    return jnp.asarray(_dft_consts()[name], F32)


def _cmul(ar, ai, br, bi):
    return ar * br - ai * bi, ar * bi + ai * br


def _hy_conv_kernel(v_ref, x1_ref, x2_ref, k0_ref, k1_ref, ks0_ref, ks1_ref, bias_ref, rd_ref, rf_ref, ro_ref,
                    ln_ref, twr_ref, twi_ref, o_ref, st_ref, ks_ref):
    g = pl.program_id(0)
    n_ch = SUBLANES
    h = HY_N2 // 2
    rows_data = rd_ref[...].astype(BF16)
    rows_filt = rf_ref[...].astype(BF16)
    rows_out = ro_ref[...].astype(BF16)
    lanes = ln_ref[...].astype(BF16)
    twr, twi = twr_ref[...], twi_ref[...]

    def rows_then_twiddle(mat, x, c):
        a = jnp.dot(mat, x.astype(BF16), preferred_element_type=F32)
        ar, ai = _cmul(a[:HY_N2], a[HY_N2:], twr, twi)
        st_ref[c * HY_N2:(c + 1) * HY_N2, 0:HY_N1] = ar.astype(BF16)
        st_ref[c * HY_N2:(c + 1) * HY_N2, HY_N1:] = ai.astype(BF16)

    for o, (k_ref, s_ref) in enumerate(((k0_ref, ks0_ref), (k1_ref, ks1_ref))):
        for c in range(n_ch):
            rows_then_twiddle(rows_filt, _chan_rows(k_ref, (0,), c, HY_N2), c)
        spec = jnp.dot(st_ref[...], lanes, preferred_element_type=F32)
        inv = 1.0 / (jnp.sum(s_ref[...], axis=1, keepdims=True) * float(HY_N1 * HY_N2))
        for c in range(n_ch):
            ks_ref[o, c * HY_N2:(c + 1) * HY_N2, :] = spec[c * HY_N2:(c + 1) * HY_N2] * inv[c:c + 1, :]

    z = [[_chan_rows(v_ref, (b, 0), c, h) for c in range(n_ch)] for b in range(2)]
    for o, gate_ref in enumerate((x1_ref, x2_ref)):
        for c in range(n_ch):
            rows_then_twiddle(rows_data, jnp.concatenate([z[0][c], z[1][c]], axis=0), c)
        x = jnp.dot(st_ref[...], lanes, preferred_element_type=F32)
        pr, pi = _cmul(x[:, :HY_N1], x[:, HY_N1:], ks_ref[o, :, 0:HY_N1], ks_ref[o, :, HY_N1:])
        st_ref[:, 0:HY_N1] = pr.astype(BF16)
        st_ref[:, HY_N1:] = (-pi).astype(BF16)
        y = jnp.dot(st_ref[...], lanes, preferred_element_type=F32)
        for c in range(n_ch):
            yc = y[c * HY_N2:(c + 1) * HY_N2]
            yr, yi = _cmul(yc[:, :HY_N1], yc[:, HY_N1:], twr, twi)
            out = jnp.dot(rows_out, jnp.concatenate([yr, yi], axis=0).astype(BF16),
                          preferred_element_type=F32)
            bias = bias_ref[o, g * n_ch + c]
            for b, conv in ((0, out[:h]), (1, -out[h:])):
                z[b][c] = _chan_rows(gate_ref, (b, 0), c, h) * (conv + bias * z[b][c])
    for b in range(2):
        for c in range(n_ch):
            for hh in range(CM_HALVES):
                o_ref.at[b, 0, hh][pl.ds(c, h, stride=SUBLANES), :] = z[b][c][:, hh * LANES:(hh + 1) * LANES]


def _hy_conv(v, x1, x2, kc, ksum, hy_bias):
    B, ng = v.shape[0], v.shape[1]
    assert B == 2 and v.shape[3] * 2 == HY_N2 * SUBLANES and kc.shape[2] == HY_N2 * SUBLANES
    consts = [_dft_const(n) for n in ("rows_data", "rows_filt", "rows_out", "lanes", "twr", "twi")]
    tok = pl.BlockSpec((B, 1) + v.shape[2:], lambda g: (0, g, 0, 0, 0))
    kspec = lambda o: pl.BlockSpec((1,) + kc.shape[1:], lambda g: (o * ng + g, 0, 0, 0))
    sspec = lambda o: pl.BlockSpec((SUBLANES, ksum.shape[1]), lambda g: (o * ng + g, 0))
    full = lambda a: pl.BlockSpec(a.shape, lambda g: (0,) * a.ndim)
    return pl.pallas_call(
        _hy_conv_kernel,
        out_shape=jax.ShapeDtypeStruct(v.shape, F32),
        grid=(ng,),
        in_specs=[tok, tok, tok, kspec(0), kspec(1), sspec(0), sspec(1),
                  pl.BlockSpec(memory_space=pltpu.SMEM)] + [full(a) for a in consts],
        out_specs=tok,
        scratch_shapes=[pltpu.VMEM((SUBLANES * HY_N2, 2 * HY_N1), BF16),
                        pltpu.VMEM((HY_ORDER, SUBLANES * HY_N2, 2 * HY_N1), F32)],
        compiler_params=pltpu.CompilerParams(vmem_limit_bytes=VMEM_LIMIT),
        name="hy_conv",
    )(v, x1, x2, kc, kc, ksum, ksum, hy_bias, *consts)


def _hy_filter_kernel(w1t_ref, w1c_ref, w1s_ref, b1_ref, f0_ref, w2_ref, b2_ref, f1_ref, w3_ref, dl_ref,
                      k_ref, s_ref, *, L, P, chan_major):
    i = pl.program_id(0)

    def pos_t(rows):
        n = i * P + lax.broadcasted_iota(jnp.int32, (rows, P), 1)
        return n, jnp.where(n < L, n, 2 * L - n).astype(F32) / L

    _, tb = pos_t(HY_BANDS)
    band = (lax.broadcasted_iota(jnp.int32, tb.shape, 0) + 1).astype(F32)
    ang = 2.0 * math.pi * tb * band
    _, t1 = pos_t(1)
    h = w1t_ref[...] * t1 + b1_ref[...]
    h = h + jnp.dot(w1c_ref[...], jnp.cos(ang).astype(BF16), preferred_element_type=F32)
    h = h + jnp.dot(w1s_ref[...], jnp.sin(ang).astype(BF16), preferred_element_type=F32)
    h = jnp.sin(f0_ref[...] * h)
    h = jnp.sin(f1_ref[...] * (jnp.dot(w2_ref[...], h.astype(BF16), preferred_element_type=F32) + b2_ref[...]))
    k = jnp.dot(w3_ref[0], h.astype(BF16), preferred_element_type=F32)
    n, t = pos_t(1)
    k = jnp.where(n == L, 0.0, k * jnp.exp(-t * dl_ref[...]))
    if chan_major:
        for j in range(P // CM_LANES):
            for hh in range(CM_HALVES):
                lane0 = j * CM_LANES + hh * LANES
                k_ref[:, hh, j * SUBLANES:(j + 1) * SUBLANES, :] = k[:, lane0:lane0 + LANES].reshape(
                    k_ref.shape[0], SUBLANES, LANES)
    else:
        k_ref[...] = k
    ka = jnp.abs(k)
    tot = ka[:, 0:128]
    for j in range(1, P // 128):
        tot = tot + ka[:, j * 128:(j + 1) * 128]

    @pl.when(i == 0)
    def _():
        s_ref[...] = jnp.zeros_like(s_ref)

    s_ref[...] += tot


def _hy_filter(L, P, chan_major, f_w1, f_b1, f_w2, f_b2, f_w3, f_freq):
    nb = 2 * L // P
    assert nb % 2 == 0 and P % CM_LANES == 0
    hid = f_w2.shape[0]
    oc = HY_ORDER * HY_WIDTH
    col = lambda v: v.reshape(-1, 1)
    w3 = f_w3.reshape(hid, HY_ORDER, 2, HY_WIDTH)
    w3 = jnp.stack([w3[:, :, d].reshape(hid, oc).T for d in range(2)]).astype(BF16)
    deltas = np.abs(np.linspace(math.log(HY_DECAY_TARGET) / HY_FAST_DECAY,
                                math.log(HY_DECAY_TARGET) / HY_SLOW_DECAY, HY_WIDTH, dtype=np.float32))
    ins = [col(f_w1[0]), f_w1[1:1 + HY_BANDS].T.astype(BF16), f_w1[1 + HY_BANDS:].T.astype(BF16), col(f_b1),
           col(f_freq[0]), f_w2.T.astype(BF16), col(f_b2), col(f_freq[1]), w3,
           jnp.asarray(np.tile(deltas, HY_ORDER)[:, None])]
    full = lambda a: pl.BlockSpec(a.shape, lambda i: (0,) * a.ndim)
    specs = [full(a) for a in ins]
    specs[8] = pl.BlockSpec((1,) + w3.shape[1:], lambda i: (i // (nb // 2), 0, 0))
    if chan_major:
        kshape = (oc // SUBLANES, CM_HALVES, 2 * L // CM_LANES * SUBLANES, LANES)
        kspec = pl.BlockSpec((oc // SUBLANES, CM_HALVES, P // CM_LANES * SUBLANES, LANES), lambda i: (0, 0, i, 0))
    else:
        kshape = (oc, 2 * L)
        kspec = pl.BlockSpec((oc, P), lambda i: (0, i))
    return pl.pallas_call(
        functools.partial(_hy_filter_kernel, L=L, P=P, chan_major=chan_major),
        out_shape=[jax.ShapeDtypeStruct(kshape, F32), jax.ShapeDtypeStruct((oc, 128), F32)],
        grid=(nb,),
        in_specs=specs,
        out_specs=[kspec, pl.BlockSpec((oc, 128), lambda i: (0, 0))],
        compiler_params=pltpu.CompilerParams(dimension_semantics=("arbitrary",), vmem_limit_bytes=VMEM_LIMIT),
        name="hy_filter",
    )(*ins)


def _hyena(v, x1, x2, filt, hy_bias):
    L = v.shape[3] // SUBLANES * CM_LANES
    assert 2 * L == HY_N1 * HY_N2
    kc, ksum = _hy_filter(L, HY_FILT_COLS, True, *filt)
    return _hy_conv(v, x1, x2, kc, ksum, hy_bias)


@functools.lru_cache(maxsize=None)
def _ctx_dft_consts(n):
    i = np.arange(n, dtype=np.float64)
    a = 2.0 * np.pi * np.outer(i, i) / n
    f = np.concatenate([np.cos(a), -np.sin(a)], axis=0)
    return np.asarray(f, np.float32), np.asarray(f.T[:n // 2], np.float32)


def _hyena_ctx_kernel(v_ref, x1_ref, x2_ref, kc_ref, ks_ref, bias_ref, ff_ref, fi_ref, o_ref):
    n = ff_ref.shape[1]
    lc = n // 2
    C = v_ref.shape[2]
    ff = ff_ref[...].astype(BF16)
    inv = 1.0 / (jnp.sum(ks_ref[...], axis=1, keepdims=True) * float(n))
    kn = (kc_ref[...] * inv).T
    kspec = jnp.dot(ff, kn.astype(BF16), preferred_element_type=F32)
    z = jnp.concatenate([v_ref[0], v_ref[1]], axis=1)
    for o, g_ref in enumerate((x1_ref, x2_ref)):
        x = jnp.dot(ff[:, :lc], z.astype(BF16), preferred_element_type=F32)
        k = kspec[:, o * C:(o + 1) * C]
        kr = jnp.concatenate([k[:n], k[:n]], axis=1)
        ki = jnp.concatenate([k[n:], k[n:]], axis=1)
        yr, yi = _cmul(x[:n], x[n:], kr, ki)
        y = jnp.concatenate([yr, yi], axis=0)
        conv = jnp.dot(fi_ref[...].astype(BF16), y.astype(BF16), preferred_element_type=F32)
        bias = jnp.concatenate([bias_ref[o:o + 1, :], bias_ref[o:o + 1, :]], axis=1)
        z = jnp.concatenate([g_ref[0], g_ref[1]], axis=1) * (conv + bias * z)
    o_ref[0] = z[:, :C]
    o_ref[1] = z[:, C:]


def _hyena_ctx(v, x1, x2, filt, hy_bias):
    B, Lc, C = v.shape
    assert B == 2
    kc, ksum = _hy_filter(Lc, Lc, False, *filt)
    ff, fi = (jnp.asarray(m, F32) for m in _ctx_dft_consts(2 * Lc))
    return pl.pallas_call(
        _hyena_ctx_kernel,
        out_shape=jax.ShapeDtypeStruct((B, Lc, C), F32),
        compiler_params=pltpu.CompilerParams(vmem_limit_bytes=VMEM_LIMIT),
        name="hyena_ctx",
    )(v, x1, x2, kc, ksum, hy_bias, ff, fi)


NA_ROWS_PER_STEP = 8
_NT = (((1,), (1,)), ((), ()))


def _na_bias_table(rpb):
    qc = np.arange(GRID_W)[:, None]
    kc = np.arange(GRID_W)[None, :]
    start = np.clip(qc - NA_WIN_COLS // 2, 0, GRID_W - NA_WIN_COLS)
    valid = (kc >= start) & (kc < start + NA_WIN_COLS)
    pad = jnp.pad(rpb, ((0, 0), (0, 0), (GRID_W, GRID_W)))
    shift = GRID_W + NA_WIN_COLS - 1
    toep = jnp.stack([pad[:, :, shift - c:shift - c + GRID_W] for c in range(GRID_W)], axis=2)
    full = jnp.where(jnp.asarray(valid)[None, None], toep, NEG_INF)
    t = jnp.stack([full[:, d:d + NA_WIN_ROWS] for d in range(NA_WIN_ROWS)], axis=1)
    t = t.reshape(NA_HEADS // 2, 2, NA_WIN_ROWS, NA_WIN_ROWS, GRID_W, GRID_W).transpose(0, 2, 1, 4, 3, 5)
    return t.reshape(NA_HEADS // 2, NA_WIN_ROWS, 2 * GRID_W, NA_WIN_ROWS * GRID_W).astype(F32)


def _na_kernel(q_ref, kp_ref, kc_ref, kn_ref, vp_ref, vc_ref, vn_ref, ck_ref, cv_ref, bias_ref, o_ref,
               wk_ref, wv_ref, sc_ref, pc_ref, ol_ref, li_ref):
    i = pl.program_id(2)
    last = pl.num_programs(2) - 1
    blk = NA_ROWS_PER_STEP * GRID_W
    for n, (kr, vr) in enumerate(((kp_ref, vp_ref), (kc_ref, vc_ref), (kn_ref, vn_ref))):
        wk_ref[n * blk:(n + 1) * blk] = kr[0]
        wv_ref[n * blk:(n + 1) * blk] = vr[0]
    first_head = lax.broadcasted_iota(jnp.int32, (GRID_W, 2 * NA_HEAD_DIM), 1) < NA_HEAD_DIM
    q = q_ref[0]
    zero = jnp.zeros((GRID_W, 2 * NA_HEAD_DIM), q.dtype)
    pieces = []
    for j in range(NA_ROWS_PER_STEP):
        qj = q[j * GRID_W:(j + 1) * GRID_W]
        pieces += [jnp.where(first_head, qj, zero), jnp.where(first_head, zero, qj)]
    qs = jnp.concatenate(pieces, axis=0)
    sc_ref[...] = lax.dot_general(qs, ck_ref[0], _NT, preferred_element_type=F32)
    half = NA_WIN_ROWS // 2
    rows = 2 * GRID_W
    for j in range(NA_ROWS_PER_STEP):
        off = jnp.where(i == 0, max(j + half, NA_ROWS_PER_STEP),
                        jnp.where(i == last, min(j + half, NA_ROWS_PER_STEP), j + half))
        d = off - j - 1
        start = pl.multiple_of(off * GRID_W, GRID_W)
        kw = wk_ref[pl.ds(start, NA_WIN_ROWS * GRID_W), :]
        vw = wv_ref[pl.ds(start, NA_WIN_ROWS * GRID_W), :]
        r0 = j * rows
        s_loc = lax.dot_general(qs[r0:r0 + rows], kw, _NT, preferred_element_type=F32) + bias_ref[0, d]
        s_ctx = sc_ref[r0:r0 + rows, :]
        m = jnp.maximum(jnp.max(s_loc, axis=1, keepdims=True), jnp.max(s_ctx, axis=1, keepdims=True))
        p_loc = jnp.exp(s_loc - m)
        p_ctx = jnp.exp(s_ctx - m)
        l = jnp.sum(p_loc, axis=1, keepdims=True) + jnp.sum(p_ctx, axis=1, keepdims=True)
        pc_ref[r0:r0 + rows, :] = p_ctx.astype(pc_ref.dtype)
        ol_ref[r0:r0 + rows, :] = jnp.dot(p_loc.astype(BF16), vw, preferred_element_type=F32)
        li_ref[r0:r0 + rows, :] = jnp.broadcast_to(1.0 / l, (rows, 2 * NA_HEAD_DIM))
    o = (ol_ref[...] + jnp.dot(pc_ref[...], cv_ref[0], preferred_element_type=F32)) * li_ref[...]
    for j in range(NA_ROWS_PER_STEP):
        r0 = j * rows
        oj = jnp.where(first_head, o[r0:r0 + GRID_W], o[r0 + GRID_W:r0 + rows])
        o_ref[0, j * GRID_W:(j + 1) * GRID_W, :] = oj.astype(o_ref.dtype)


def _na(q, k, v, ck, cv, bias):
    B, L, W = q.shape
    Lc = ck.shape[1]
    blk = NA_ROWS_PER_STEP * GRID_W
    nblk = L // blk
    assert NA_ROWS_PER_STEP == NA_WIN_ROWS and nblk >= 2
    pw = 2 * NA_HEAD_DIM
    cur = pl.BlockSpec((1, blk, pw), lambda b, h, i: (b, i, h))
    prev = pl.BlockSpec((1, blk, pw), lambda b, h, i: (b, jnp.maximum(i - 1, 0), h))
    nxt = pl.BlockSpec((1, blk, pw), lambda b, h, i: (b, jnp.minimum(i + 1, nblk - 1), h))
    cspec = pl.BlockSpec((1, Lc, pw), lambda b, h, i: (b, 0, h))
    bspec = pl.BlockSpec((1,) + bias.shape[1:], lambda b, h, i: (h, 0, 0, 0))
    stacked = 2 * blk
    return pl.pallas_call(
        _na_kernel,
        out_shape=jax.ShapeDtypeStruct((B, L, W), BF16),
        grid=(B, W // pw, nblk),
        in_specs=[cur, prev, cur, nxt, prev, cur, nxt, cspec, cspec, bspec],
        out_specs=cur,
        scratch_shapes=[pltpu.VMEM((3 * blk, pw), BF16), pltpu.VMEM((3 * blk, pw), BF16),
                        pltpu.VMEM((stacked, Lc), F32), pltpu.VMEM((stacked, Lc), BF16),
                        pltpu.VMEM((stacked, pw), F32), pltpu.VMEM((stacked, pw), F32)],
        compiler_params=pltpu.CompilerParams(vmem_limit_bytes=VMEM_LIMIT),
        name="nattn",
    )(q, k, k, k, v, v, v, ck, cv, bias)


def _ctx_attn_kernel(q_ref, k_ref, v_ref, o_ref):
    q = q_ref[0]
    k = k_ref[0]
    v = v_ref[0]
    first_head = lax.broadcasted_iota(jnp.int32, q.shape, 1) < NA_HEAD_DIM
    outs = []
    for h in range(2):
        qm = jnp.where(first_head if h == 0 else jnp.logical_not(first_head), q, jnp.zeros_like(q))
        s = lax.dot_general(qm, k, _NT, preferred_element_type=F32)
        p = jnp.exp(s - jnp.max(s, axis=1, keepdims=True))
        o = jnp.dot(p.astype(BF16), v, preferred_element_type=F32)
        outs.append(o / jnp.sum(p, axis=1, keepdims=True))
    o_ref[0] = jnp.where(first_head, outs[0], outs[1]).astype(o_ref.dtype)


def _ctx_attn(q, k, v):
    B, Lc, W = q.shape
    pw = 2 * NA_HEAD_DIM
    spec = pl.BlockSpec((1, Lc, pw), lambda b, h: (b, 0, h))
    return pl.pallas_call(
        _ctx_attn_kernel,
        out_shape=jax.ShapeDtypeStruct((B, Lc, W), BF16),
        grid=(B, W // pw),
        in_specs=[spec, spec, spec],
        out_specs=spec,
        name="ctx_attn",
    )(q, k, v)


LRU_CHUNK = 512


def _lru_gate_weights(wa, ba, wi, bi, lam):
    def bd(w):
        return jax.scipy.linalg.block_diag(*[w[n] for n in range(LRU_BLOCKS)])
    wg = jnp.stack([jnp.concatenate([bd(wa[d]), bd(wi[d])], axis=1) for d in range(2)]).astype(BF16)
    bg = jnp.stack([jnp.concatenate([ba[d], bi[d]])[None, :] for d in range(2)])
    return wg, bg, lam[:, None, :]


def _lru_coeffs(u, wg, bg, lam):
    C = u.shape[1]
    g = jnp.dot(u.astype(BF16), wg, preferred_element_type=F32) + bg
    sig = 0.5 + 0.5 * jnp.tanh(0.5 * g)
    r, ig = sig[:, :C], sig[:, C:]
    nl = -lam
    softplus = jnp.maximum(nl, 0.0) + jnp.log(1.0 + jnp.exp(-jnp.abs(nl)))
    log_a = (-LRU_C * softplus) * r
    a = jnp.exp(log_a)
    t = jnp.tanh(log_a)
    b = jnp.sqrt(-2.0 * t / (1.0 - t)) * (ig * u)
    return a, b


def _lru_scan(a, b, h0, reverse, ac_ref, bc_ref, h_ref):
    T, C = a.shape
    row = lax.broadcasted_iota(jnp.int32, a.shape, 0) % SUBLANES
    for s in (1, 2, 4):
        shift = T - s if reverse else s
        keep = (row < SUBLANES - s) if reverse else (row >= s)
        b = jnp.where(keep, a * pltpu.roll(b, shift, 0) + b, b)
        a = jnp.where(keep, a * pltpu.roll(a, shift, 0), a)
    ac_ref[...] = a
    bc_ref[...] = b
    ng = T // SUBLANES

    def group(g, h):
        r0 = pl.multiple_of((ng - 1 - g if reverse else g) * SUBLANES, SUBLANES)
        hr = ac_ref[pl.ds(r0, SUBLANES), :] * h + bc_ref[pl.ds(r0, SUBLANES), :]
        h_ref[pl.ds(r0, SUBLANES), :] = hr
        edge = hr[0:1] if reverse else hr[SUBLANES - 1:SUBLANES]
        return jnp.broadcast_to(edge, (SUBLANES, C))

    return lax.fori_loop(0, ng, group, h0, unroll=4)


def _gelu_tanh(x):
    return 0.5 * x * (1.0 + jnp.tanh(math.sqrt(2.0 / math.pi) * (x + 0.044715 * (x * x * x))))


def _lru_ctx_kernel(u_ref, xg_ref, wg_ref, bg_ref, lam_ref, hend_ref, yc_ref, ac_ref, bc_ref, h_ref):
    u = u_ref[0]
    C = u.shape[1]
    total = jnp.zeros_like(u)
    for d, rev in enumerate((False, True)):
        a, b = _lru_coeffs(u, wg_ref[d], bg_ref[d], lam_ref[d])
        hl = _lru_scan(a, b, jnp.zeros((SUBLANES, C), F32), rev, ac_ref, bc_ref, h_ref)
        hend_ref[0, d:d + 1, :] = hl[0:1]
        total = total + h_ref[...]
    yc_ref[0] = (total * _gelu_tanh(xg_ref[0])).astype(yc_ref.dtype)


def _lru_ctx(u, xg, wg, bg, lam):
    B, Lc, C = u.shape
    tok = pl.BlockSpec((1, Lc, C), lambda b: (b, 0, 0))
    full = lambda a: pl.BlockSpec(a.shape, lambda b: (0,) * a.ndim)
    return pl.pallas_call(
        _lru_ctx_kernel,
        out_shape=[jax.ShapeDtypeStruct((B, 2, C), F32), jax.ShapeDtypeStruct((B, Lc, C), BF16)],
        grid=(B,),
        in_specs=[tok, tok, full(wg), full(bg), full(lam)],
        out_specs=[pl.BlockSpec((1, 2, C), lambda b: (b, 0, 0)), tok],
        scratch_shapes=[pltpu.VMEM((Lc, C), F32)] * 3,
        name="lru_ctx",
    )(u, xg, wg, bg, lam)


def _lru_dir_kernel(*refs, d, reverse):
    if reverse:
        u_ref, hend_ref, wg_ref, bg_ref, lam_ref, hf_ref, xg_ref, o_ref, ac_ref, bc_ref, h_ref, carry_ref = refs
    else:
        u_ref, hend_ref, wg_ref, bg_ref, lam_ref, o_ref, ac_ref, bc_ref, carry_ref = refs
        h_ref = o_ref.at[0]
    C = u_ref.shape[2]

    @pl.when(pl.program_id(1) == 0)
    def _():
        carry_ref[...] = jnp.broadcast_to(hend_ref[0, d:d + 1, :], (SUBLANES, C))

    a, b = _lru_coeffs(u_ref[0], wg_ref[d], bg_ref[d], lam_ref[d])
    carry_ref[...] = _lru_scan(a, b, carry_ref[...], reverse, ac_ref, bc_ref, h_ref)
    if reverse:
        o_ref[0] = ((hf_ref[0] + h_ref[...]) * _gelu_tanh(xg_ref[0])).astype(o_ref.dtype)


def _lru_dir(u, hend, wg, bg, lam, hf=None, xg=None):
    B, L, C = u.shape
    reverse = hf is not None
    T = LRU_CHUNK
    nb = L // T
    tok = pl.BlockSpec((1, T, C), (lambda b, i: (b, nb - 1 - i, 0)) if reverse else (lambda b, i: (b, i, 0)))
    full = lambda a: pl.BlockSpec(a.shape, lambda b, i: (0,) * a.ndim)
    ins = [u, hend, wg, bg, lam] + ([hf, xg] if reverse else [])
    specs = [tok, pl.BlockSpec((1, 2, C), lambda b, i: (b, 0, 0)), full(wg), full(bg), full(lam)]
    specs += [tok, tok] if reverse else []
    scratch = [pltpu.VMEM((T, C), F32)] * (3 if reverse else 2) + [pltpu.VMEM((SUBLANES, C), F32)]
    return pl.pallas_call(
        functools.partial(_lru_dir_kernel, d=int(reverse), reverse=reverse),
        out_shape=jax.ShapeDtypeStruct((B, L, C), BF16 if reverse else F32),
        grid=(B, nb),
        in_specs=specs,
        out_specs=tok,
        scratch_shapes=scratch,
        compiler_params=pltpu.CompilerParams(dimension_semantics=("arbitrary", "arbitrary")),
        name="lru_bwd" if reverse else "lru_fwd",
    )(*ins)


def _lru(u, xg, u_c, xg_c, wa, ba, wi, bi, lam):
    wg, bg, lam3 = _lru_gate_weights(wa, ba, wi, bi, lam)
    hend, yc = _lru_ctx(u_c, xg_c, wg, bg, lam3)
    hf = _lru_dir(u, hend, wg, bg, lam3)
    return _lru_dir(u, hend, wg, bg, lam3, hf, xg), yc


def kernel(x, c, ctx, c_ctx, ada_w, ada_b, g_mix_pre, g_mix_post, g_ffn_pre, g_ffn_post, w_in, w_out, hy_conv_w,
           hy_conv_b, hy_f_w1, hy_f_b1, hy_f_w2, hy_f_b2, hy_f_w3, hy_f_freq, hy_bias, na_rpb, lru_conv_w,
           lru_conv_b, lru_wa, lru_ba, lru_wi, lru_bi, lru_lam, ffn_w_gu, ffn_w_down):
    B, L, D = x.shape
    Lc = ctx.shape[1]
    tm = 512

    assert B + 1 <= SUBLANES
    cond_t = jnp.zeros((D, SUBLANES), F32).at[:, 0:B].set(c.T).at[:, B].set(c_ctx)
    mods = _modulation(cond_t, B + 1, ada_w, ada_b)

    xc = ctx
    for l in range(DEPTH):
        with_ctx_out = l < DEPTH - 1
        m = mods[l].reshape(8, 6, D)
        lat = [m[0:B, j][:, None, :] for j in range(6)]
        cx = [jnp.broadcast_to(m[B, j][None, None, :], (B, 1, D)) for j in range(6)]
        row = lambda a: a.reshape(1, -1)

        w_in_bf = w_in[l].astype(BF16)
        w_out_bf = w_out[l].astype(BF16)
        nchunk = D_FF // FF_CHUNK
        wg = ffn_w_gu[l][:, :D_FF].astype(BF16).reshape(D, nchunk, FF_CHUNK).transpose(1, 0, 2)
        wu = ffn_w_gu[l][:, D_FF:].astype(BF16).reshape(D, nchunk, FF_CHUNK).transpose(1, 0, 2)
        wd = ffn_w_down[l].astype(BF16).reshape(nchunk, FF_CHUNK, D)

        conv_args = (hy_conv_w[l], row(hy_conv_b[l]), lru_conv_w[l], row(lru_conv_b[l]))
        hv, hx1, hx2, q, k, v, lu, lg = _inproj(x, lat[1], lat[0], row(g_mix_pre[l]), w_in_bf, *conv_args, tm=tm,
                                                chan_major=True)
        cv, cx1, cx2, cq, ck, cvv, clu, clg = _inproj(xc, cx[1], cx[0], row(g_mix_pre[l]), w_in_bf, *conv_args,
                                                      tm=Lc, chan_major=False)

        filt = (hy_f_w1[l], hy_f_b1[l], hy_f_w2[l], hy_f_b2[l], hy_f_w3[l], hy_f_freq[l])
        y_hy = _hyena(hv, hx1, hx2, filt, hy_bias[l])
        y_na = _na(q, k, v, ck, cvv, _na_bias_table(na_rpb[l]))
        y_lru, yc_lru = _lru(lu, lg, clu, clg, lru_wa[l], lru_ba[l], lru_wi[l], lru_bi[l], lru_lam[l])

        ffn_args = (row(g_ffn_pre[l]), row(g_ffn_post[l]), wg, wu, wd)
        x = _out_ffn(y_hy, y_na, y_lru, w_out_bf, x, row(g_mix_post[l]), lat[2], lat[4], lat[3], lat[5], *ffn_args,
                     tm=tm, chan_major=True)

        if with_ctx_out:
            yc_hy = _hyena_ctx(cv, cx1, cx2, filt, hy_bias[l])
            yc_na = _ctx_attn(cq, ck, cvv)
            xc = _out_ffn(yc_hy, yc_na, yc_lru, w_out_bf, xc, row(g_mix_post[l]), cx[2], cx[4], cx[3], cx[5],
                          *ffn_args, tm=Lc, chan_major=False)
    return x
```

```python
import functools
import math

import jax
import jax.numpy as jnp
import numpy as np
from jax import lax
from jax.experimental import pallas as pl
from jax.experimental.pallas import tpu as pltpu

F32 = jnp.float32
BF16 = jnp.bfloat16

D_MODEL = 1024
DEPTH = 2
GRID_W = 64
HY_WIDTH = D_MODEL // 4
NA_HEAD_DIM = 64
NA_WIDTH = D_MODEL // 2
NA_HEADS = NA_WIDTH // NA_HEAD_DIM
LRU_WIDTH = D_MODEL // 4
LRU_BLOCKS = 4
IN_WIDTH = 3 * HY_WIDTH + 3 * NA_WIDTH + 2 * LRU_WIDTH
HY_ORDER = 2
HY_BANDS = 16
HY_FAST_DECAY = 0.3
HY_SLOW_DECAY = 1.5
HY_DECAY_TARGET = 1e-2
NA_WIN_ROWS = 8
NA_WIN_COLS = 16
LRU_C = 8.0
D_FF = -(-8 * D_MODEL // (3 * 256)) * 256
RMS_EPS = 1e-6
NEG_INF = -1e30

_HY_END = 3 * HY_WIDTH
_NA_END = _HY_END + 3 * NA_WIDTH
_LRU_MID = _NA_END + LRU_WIDTH

HALO = 8
SUBLANES = 8
VMEM_LIMIT = 48 * 1024 * 1024


def _rms(x, g):
    return x * lax.rsqrt(jnp.mean(x * x, axis=-1, keepdims=True) + RMS_EPS) * g


def _mod_kernel(ct_ref, w_ref, b_ref, o_ref, *, n_cond):
    ct = ct_ref[...]
    st = ct * jax.nn.sigmoid(ct)
    w = w_ref[0]
    rows = [jnp.sum(w * st[:, r:r + 1], axis=0, keepdims=True) for r in range(n_cond)]
    rows.append(jnp.zeros((SUBLANES - n_cond, w.shape[1]), F32))
    o_ref[0] = jnp.concatenate(rows, axis=0) + b_ref[0]


def _modulation(cond_t, n_cond, ada_w, ada_b):
    tn = 768
    n = ada_w.shape[-1]
    return pl.pallas_call(
        functools.partial(_mod_kernel, n_cond=n_cond),
        out_shape=jax.ShapeDtypeStruct((DEPTH, SUBLANES, n), F32),
        grid=(DEPTH, n // tn),
        in_specs=[pl.BlockSpec((D_MODEL, SUBLANES), lambda l, j: (0, 0)),
                  pl.BlockSpec((1, D_MODEL, tn), lambda l, j: (l, 0, j)),
                  pl.BlockSpec((1, 1, tn), lambda l, j: (l, 0, j))],
        out_specs=pl.BlockSpec((1, SUBLANES, tn), lambda l, j: (l, 0, j)),
        compiler_params=pltpu.CompilerParams(vmem_limit_bytes=VMEM_LIMIT),
        name="adaln_modulation",
    )(cond_t, ada_w, ada_b.reshape(DEPTH, 1, n))


CM_LANES = 256
LANES = 128
CM_HALVES = CM_LANES // LANES


def _chan_major_shape(B, L, C):
    return (B, C // SUBLANES, CM_HALVES, L // CM_LANES * SUBLANES, LANES)


def _chan_major_spec(tm, C):
    return pl.BlockSpec((1, C // SUBLANES, CM_HALVES, tm // CM_LANES * SUBLANES, LANES),
                        lambda b, i: (b, 0, 0, i, 0))


def _store_chan_major(ref, u):
    ut = u.T
    for g in range(u.shape[1] // SUBLANES):
        for j in range(u.shape[0] // CM_LANES):
            for h in range(CM_HALVES):
                lane0 = j * CM_LANES + h * LANES
                ref[0, g, h, j * SUBLANES:(j + 1) * SUBLANES, :] = ut[g * SUBLANES:(g + 1) * SUBLANES,
                                                                      lane0:lane0 + LANES]


def _load_chan_major(ref):
    _, ng, _, nr, _ = ref.shape
    rows = [jnp.concatenate([ref[0, g, h, j * SUBLANES:(j + 1) * SUBLANES, :]
                             for j in range(nr // SUBLANES) for h in range(CM_HALVES)], axis=1) for g in range(ng)]
    return jnp.concatenate(rows, axis=0).T


def _chan_rows(ref, lead, c, n):
    return jnp.concatenate([ref.at[lead + (h,)][pl.ds(c, n, stride=SUBLANES), :] for h in range(CM_HALVES)], axis=1)


def _inproj_kernel(xp_ref, xc_ref, xn_ref, sc_ref, sh_ref, g_ref, w_ref, hcw_ref, hcb_ref, lcw_ref, lcb_ref,
                   hv_ref, hx1_ref, hx2_ref, q_ref, k_ref, v_ref, lu_ref, lg_ref, pe_ref, *, tm, chan_major):
    i = pl.program_id(1)
    last = pl.num_programs(1) - 1
    g = g_ref[...]
    sc1 = 1.0 + sc_ref[0]
    sh = sh_ref[0]

    def norm_mod(xv):
        return _rms(xv, g) * sc1 + sh

    hp = norm_mod(xp_ref[0]) * (i > 0).astype(F32)
    hn = norm_mod(xn_ref[0]) * (i < last).astype(F32)
    he = jnp.concatenate([hp, norm_mod(xc_ref[0]), hn], axis=0).astype(BF16)

    pe_ref[:, 0:_HY_END] = jnp.dot(he, w_ref[:, 0:_HY_END], preferred_element_type=F32)
    pe_ref[:, _HY_END:] = jnp.dot(he, w_ref[:, _NA_END:_LRU_MID], preferred_element_type=F32)
    hc = he[HALO:HALO + tm]
    qkv = jnp.dot(hc, w_ref[:, _HY_END:_NA_END], preferred_element_type=F32)
    q_ref[0] = (qkv[:, 0:NA_WIDTH] * (NA_HEAD_DIM ** -0.5)).astype(BF16)
    kk = qkv[:, NA_WIDTH:2 * NA_WIDTH]
    k_ref[0] = (kk.T if chan_major else kk).astype(BF16)
    v_ref[0] = qkv[:, 2 * NA_WIDTH:].astype(BF16)
    lg_ref[0] = jnp.dot(hc, w_ref[:, _LRU_MID:], preferred_element_type=F32)

    u = hcb_ref[...]
    for kk in range(3):
        u = u + hcw_ref[kk:kk + 1, :] * pe_ref[pl.ds(HALO - 1 + kk, tm), 0:_HY_END]
    for n, ref in enumerate((hv_ref, hx1_ref, hx2_ref)):
        un = u[:, n * HY_WIDTH:(n + 1) * HY_WIDTH]
        if chan_major:
            _store_chan_major(ref, un)
        else:
            ref[0] = un
    ul = lcb_ref[...]
    for kk in range(4):
        ul = ul + lcw_ref[kk:kk + 1, :] * pe_ref[pl.ds(HALO - 2 + kk, tm), _HY_END:]
    lu_ref[0] = ul


def _inproj(x, sc, sh, g, w_bf, hcw, hcb, lcw, lcb, *, tm, chan_major):
    B, L, D = x.shape
    nb = tm // HALO
    nh = L // HALO
    tok = lambda w, dt: jax.ShapeDtypeStruct((B, L, w), dt)
    tspec = lambda w: pl.BlockSpec((1, tm, w), lambda b, i: (b, i, 0))
    full = lambda a: pl.BlockSpec(a.shape, lambda b, i: (0,) * a.ndim)
    vec = pl.BlockSpec((1, 1, D), lambda b, i: (b, 0, 0))
    if chan_major:
        hy_shape = jax.ShapeDtypeStruct(_chan_major_shape(B, L, HY_WIDTH), F32)
        hy_spec = _chan_major_spec(tm, HY_WIDTH)
        k_shape = jax.ShapeDtypeStruct((B, NA_WIDTH, L), BF16)
        k_spec = pl.BlockSpec((1, NA_WIDTH, tm), lambda b, i: (b, 0, i))
    else:
        hy_shape, hy_spec = tok(HY_WIDTH, F32), tspec(HY_WIDTH)
        k_shape, k_spec = tok(NA_WIDTH, BF16), tspec(NA_WIDTH)
    return pl.pallas_call(
        functools.partial(_inproj_kernel, tm=tm, chan_major=chan_major),
        out_shape=[hy_shape] * 3 + [tok(NA_WIDTH, BF16), k_shape, tok(NA_WIDTH, BF16)] + [tok(LRU_WIDTH, F32)] * 2,
        grid=(B, L // tm),
        in_specs=[pl.BlockSpec((1, HALO, D), lambda b, i: (b, jnp.maximum(i * nb - 1, 0), 0)),
                  pl.BlockSpec((1, tm, D), lambda b, i: (b, i, 0)),
                  pl.BlockSpec((1, HALO, D), lambda b, i: (b, jnp.minimum((i + 1) * nb, nh - 1), 0)),
                  vec, vec, full(g), full(w_bf), full(hcw), full(hcb), full(lcw), full(lcb)],
        out_specs=[hy_spec] * 3 + [tspec(NA_WIDTH), k_spec, tspec(NA_WIDTH)] + [tspec(LRU_WIDTH)] * 2,
        scratch_shapes=[pltpu.VMEM((tm + 2 * HALO, _HY_END + LRU_WIDTH), F32)],
        compiler_params=pltpu.CompilerParams(vmem_limit_bytes=VMEM_LIMIT),
        name="inproj",
    )(x, x, x, sc, sh, g, w_bf, hcw, hcb, lcw, lcb)


FF_CHUNK = 256


def _out_ffn_kernel(yh_ref, yn_ref, yl_ref, wo_ref, x_ref, gm_ref, gtm_ref, sc_ref, sh_ref, gt_ref, gpre_ref,
                    gpost_ref, wg_ref, wu_ref, wd_ref, o_ref, *, chan_major):
    yh = _load_chan_major(yh_ref) if chan_major else yh_ref[0]
    y = jnp.dot(yh.astype(BF16), wo_ref[0:HY_WIDTH], preferred_element_type=F32)
    y = y + jnp.dot(yn_ref[0], wo_ref[HY_WIDTH:HY_WIDTH + NA_WIDTH], preferred_element_type=F32)
    y = y + jnp.dot(yl_ref[0], wo_ref[HY_WIDTH + NA_WIDTH:], preferred_element_type=F32)
    x = x_ref[0] + gtm_ref[0] * _rms(y, gm_ref[...])
    h = (_rms(x, gpre_ref[...]) * (1.0 + sc_ref[0]) + sh_ref[0]).astype(BF16)
    acc = jnp.zeros(x.shape, F32)
    for j in range(D_FF // FF_CHUNK):
        g = jnp.dot(h, wg_ref[j], preferred_element_type=F32)
        u = jnp.dot(h, wu_ref[j], preferred_element_type=F32)
        a = (g * jax.nn.sigmoid(g) * u).astype(BF16)
        acc = acc + jnp.dot(a, wd_ref[j], preferred_element_type=F32)
    o_ref[0] = x + gt_ref[0] * _rms(acc, gpost_ref[...])


def _out_ffn(yh, yn, yl, wo, x, gm, gtm, sc, sh, gt, gpre, gpost, wg, wu, wd, *, tm, chan_major):
    B, L, D = x.shape
    tspec = lambda w: pl.BlockSpec((1, tm, w), lambda b, i: (b, i, 0))
    vec = pl.BlockSpec((1, 1, D), lambda b, i: (b, 0, 0))
    full = lambda a: pl.BlockSpec(a.shape, lambda b, i: (0,) * a.ndim)
    res = lambda a: pl.BlockSpec(a.shape, lambda b, i: (0,) * a.ndim, pipeline_mode=pl.Buffered(1))
    hy_spec = _chan_major_spec(tm, HY_WIDTH) if chan_major else tspec(HY_WIDTH)
    return pl.pallas_call(
        functools.partial(_out_ffn_kernel, chan_major=chan_major),
        out_shape=jax.ShapeDtypeStruct((B, L, D), F32),
        grid=(B, L // tm),
        in_specs=[hy_spec, tspec(NA_WIDTH), tspec(LRU_WIDTH), res(wo), tspec(D), full(gm), vec,
                  vec, vec, vec, full(gpre), full(gpost), res(wg), res(wu), res(wd)],
        out_specs=tspec(D),
        compiler_params=pltpu.CompilerParams(vmem_limit_bytes=VMEM_LIMIT),
        name="out_ffn",
    )(yh, yn, yl, wo, x, gm, gtm, sc, sh, gt, gpre, gpost, wg, wu, wd)


HY_N1 = CM_LANES
HY_N2 = 128
HY_FILT_COLS = 2048


def _block_cplx(re, im):
    return np.block([[re, -im], [im, re]])


@functools.lru_cache(maxsize=None)
def _dft_consts():
    n = HY_N1 * HY_N2
    h = HY_N2 // 2
    i1 = np.arange(HY_N1, dtype=np.float64)
    i2 = np.arange(HY_N2, dtype=np.float64)
    a1 = 2.0 * np.pi * np.outer(i1, i1) / HY_N1
    a2 = 2.0 * np.pi * np.outer(i2, i2) / HY_N2
    at = 2.0 * np.pi * np.outer(i2, i1) / n
    r2, m2 = np.cos(a2), -np.sin(a2)
    r1, m1 = np.cos(a1), -np.sin(a1)
    f32 = lambda m: np.asarray(m, np.float32)
    return dict(
        rows_data=f32(_block_cplx(r2[:, :h], m2[:, :h])),
        rows_filt=f32(np.concatenate([r2, m2], axis=0)),
        rows_out=f32(_block_cplx(r2[:h, :], m2[:h, :])),
        lanes=f32(np.block([[r1, m1], [-m1, r1]])),
        twr=f32(np.cos(at)), twi=f32(-np.sin(at)))


def _dft_const(name):
    return jnp.asarray(_dft_consts()[name], F32)


def _cmul(ar, ai, br, bi):
    return ar * br - ai * bi, ar * bi + ai * br


def _hy_conv_kernel(v_ref, x1_ref, x2_ref, k0_ref, k1_ref, ks0_ref, ks1_ref, bias_ref, rd_ref, rf_ref, ro_ref,
                    ln_ref, twr_ref, twi_ref, o_ref, st_ref, ks_ref):
    g = pl.program_id(0)
    n_ch = SUBLANES
    h = HY_N2 // 2
    rows_data = rd_ref[...].astype(BF16)
    rows_filt = rf_ref[...].astype(BF16)
    rows_out = ro_ref[...].astype(BF16)
    lanes = ln_ref[...].astype(BF16)
    twr, twi = twr_ref[...], twi_ref[...]

    def rows_then_twiddle(mat, x, c):
        a = jnp.dot(mat, x.astype(BF16), preferred_element_type=F32)
        ar, ai = _cmul(a[:HY_N2], a[HY_N2:], twr, twi)
        st_ref[c * HY_N2:(c + 1) * HY_N2, 0:HY_N1] = ar.astype(BF16)
        st_ref[c * HY_N2:(c + 1) * HY_N2, HY_N1:] = ai.astype(BF16)

    for o, (k_ref, s_ref) in enumerate(((k0_ref, ks0_ref), (k1_ref, ks1_ref))):
        for c in range(n_ch):
            rows_then_twiddle(rows_filt, _chan_rows(k_ref, (0,), c, HY_N2), c)
        spec = jnp.dot(st_ref[...], lanes, preferred_element_type=F32)
        inv = 1.0 / (jnp.sum(s_ref[...], axis=1, keepdims=True) * float(HY_N1 * HY_N2))
        for c in range(n_ch):
            ks_ref[o, c * HY_N2:(c + 1) * HY_N2, :] = spec[c * HY_N2:(c + 1) * HY_N2] * inv[c:c + 1, :]

    z = [[_chan_rows(v_ref, (b, 0), c, h) for c in range(n_ch)] for b in range(2)]
    for o, gate_ref in enumerate((x1_ref, x2_ref)):
        for c in range(n_ch):
            rows_then_twiddle(rows_data, jnp.concatenate([z[0][c], z[1][c]], axis=0), c)
        x = jnp.dot(st_ref[...], lanes, preferred_element_type=F32)
        pr, pi = _cmul(x[:, :HY_N1], x[:, HY_N1:], ks_ref[o, :, 0:HY_N1], ks_ref[o, :, HY_N1:])
        st_ref[:, 0:HY_N1] = pr.astype(BF16)
        st_ref[:, HY_N1:] = (-pi).astype(BF16)
        y = jnp.dot(st_ref[...], lanes, preferred_element_type=F32)
        for c in range(n_ch):
            yc = y[c * HY_N2:(c + 1) * HY_N2]
            yr, yi = _cmul(yc[:, :HY_N1], yc[:, HY_N1:], twr, twi)
            out = jnp.dot(rows_out, jnp.concatenate([yr, yi], axis=0).astype(BF16),
                          preferred_element_type=F32)
            bias = bias_ref[o, g * n_ch + c]
            for b, conv in ((0, out[:h]), (1, -out[h:])):
                z[b][c] = _chan_rows(gate_ref, (b, 0), c, h) * (conv + bias * z[b][c])
    for b in range(2):
        for c in range(n_ch):
            for hh in range(CM_HALVES):
                o_ref.at[b, 0, hh][pl.ds(c, h, stride=SUBLANES), :] = z[b][c][:, hh * LANES:(hh + 1) * LANES]


def _hy_conv(v, x1, x2, kc, ksum, hy_bias):
    B, ng = v.shape[0], v.shape[1]
    assert B == 2 and v.shape[3] * 2 == HY_N2 * SUBLANES and kc.shape[2] == HY_N2 * SUBLANES
    consts = [_dft_const(n) for n in ("rows_data", "rows_filt", "rows_out", "lanes", "twr", "twi")]
    tok = pl.BlockSpec((B, 1) + v.shape[2:], lambda g: (0, g, 0, 0, 0))
    kspec = lambda o: pl.BlockSpec((1,) + kc.shape[1:], lambda g: (o * ng + g, 0, 0, 0))
    sspec = lambda o: pl.BlockSpec((SUBLANES, ksum.shape[1]), lambda g: (o * ng + g, 0))
    full = lambda a: pl.BlockSpec(a.shape, lambda g: (0,) * a.ndim)
    return pl.pallas_call(
        _hy_conv_kernel,
        out_shape=jax.ShapeDtypeStruct(v.shape, F32),
        grid=(ng,),
        in_specs=[tok, tok, tok, kspec(0), kspec(1), sspec(0), sspec(1),
                  pl.BlockSpec(memory_space=pltpu.SMEM)] + [full(a) for a in consts],
        out_specs=tok,
        scratch_shapes=[pltpu.VMEM((SUBLANES * HY_N2, 2 * HY_N1), BF16),
                        pltpu.VMEM((HY_ORDER, SUBLANES * HY_N2, 2 * HY_N1), F32)],
        compiler_params=pltpu.CompilerParams(vmem_limit_bytes=VMEM_LIMIT),
        name="hy_conv",
    )(v, x1, x2, kc, kc, ksum, ksum, hy_bias, *consts)


def _hy_filter_kernel(w1t_ref, w1c_ref, w1s_ref, b1_ref, f0_ref, w2_ref, b2_ref, f1_ref, w3_ref, dl_ref,
                      k_ref, s_ref, *, L, P, chan_major):
    i = pl.program_id(0)

    def pos_t(rows):
        n = i * P + lax.broadcasted_iota(jnp.int32, (rows, P), 1)
        return n, jnp.where(n < L, n, 2 * L - n).astype(F32) / L

    _, tb = pos_t(HY_BANDS)
    band = (lax.broadcasted_iota(jnp.int32, tb.shape, 0) + 1).astype(F32)
    ang = 2.0 * math.pi * tb * band
    _, t1 = pos_t(1)
    h = w1t_ref[...] * t1 + b1_ref[...]
    h = h + jnp.dot(w1c_ref[...], jnp.cos(ang).astype(BF16), preferred_element_type=F32)
    h = h + jnp.dot(w1s_ref[...], jnp.sin(ang).astype(BF16), preferred_element_type=F32)
    h = jnp.sin(f0_ref[...] * h)
    h = jnp.sin(f1_ref[...] * (jnp.dot(w2_ref[...], h.astype(BF16), preferred_element_type=F32) + b2_ref[...]))
    k = jnp.dot(w3_ref[0], h.astype(BF16), preferred_element_type=F32)
    n, t = pos_t(1)
    k = jnp.where(n == L, 0.0, k * jnp.exp(-t * dl_ref[...]))
    if chan_major:
        for j in range(P // CM_LANES):
            for hh in range(CM_HALVES):
                lane0 = j * CM_LANES + hh * LANES
                k_ref[:, hh, j * SUBLANES:(j + 1) * SUBLANES, :] = k[:, lane0:lane0 + LANES].reshape(
                    k_ref.shape[0], SUBLANES, LANES)
    else:
        k_ref[...] = k
    ka = jnp.abs(k)
    tot = ka[:, 0:128]
    for j in range(1, P // 128):
        tot = tot + ka[:, j * 128:(j + 1) * 128]

    @pl.when(i == 0)
    def _():
        s_ref[...] = jnp.zeros_like(s_ref)

    s_ref[...] += tot


def _hy_filter(L, P, chan_major, f_w1, f_b1, f_w2, f_b2, f_w3, f_freq):
    nb = 2 * L // P
    assert nb % 2 == 0 and P % CM_LANES == 0
    hid = f_w2.shape[0]
    oc = HY_ORDER * HY_WIDTH
    col = lambda v: v.reshape(-1, 1)
    w3 = f_w3.reshape(hid, HY_ORDER, 2, HY_WIDTH)
    w3 = jnp.stack([w3[:, :, d].reshape(hid, oc).T for d in range(2)]).astype(BF16)
    deltas = np.abs(np.linspace(math.log(HY_DECAY_TARGET) / HY_FAST_DECAY,
                                math.log(HY_DECAY_TARGET) / HY_SLOW_DECAY, HY_WIDTH, dtype=np.float32))
    ins = [col(f_w1[0]), f_w1[1:1 + HY_BANDS].T.astype(BF16), f_w1[1 + HY_BANDS:].T.astype(BF16), col(f_b1),
           col(f_freq[0]), f_w2.T.astype(BF16), col(f_b2), col(f_freq[1]), w3,
           jnp.asarray(np.tile(deltas, HY_ORDER)[:, None])]
    full = lambda a: pl.BlockSpec(a.shape, lambda i: (0,) * a.ndim)
    specs = [full(a) for a in ins]
    specs[8] = pl.BlockSpec((1,) + w3.shape[1:], lambda i: (i // (nb // 2), 0, 0))
    if chan_major:
        kshape = (oc // SUBLANES, CM_HALVES, 2 * L // CM_LANES * SUBLANES, LANES)
        kspec = pl.BlockSpec((oc // SUBLANES, CM_HALVES, P // CM_LANES * SUBLANES, LANES), lambda i: (0, 0, i, 0))
    else:
        kshape = (oc, 2 * L)
        kspec = pl.BlockSpec((oc, P), lambda i: (0, i))
    return pl.pallas_call(
        functools.partial(_hy_filter_kernel, L=L, P=P, chan_major=chan_major),
        out_shape=[jax.ShapeDtypeStruct(kshape, F32), jax.ShapeDtypeStruct((oc, 128), F32)],
        grid=(nb,),
        in_specs=specs,
        out_specs=[kspec, pl.BlockSpec((oc, 128), lambda i: (0, 0))],
        compiler_params=pltpu.CompilerParams(dimension_semantics=("arbitrary",), vmem_limit_bytes=VMEM_LIMIT),
        name="hy_filter",
    )(*ins)


def _hyena(v, x1, x2, filt, hy_bias):
    L = v.shape[3] // SUBLANES * CM_LANES
    assert 2 * L == HY_N1 * HY_N2
    kc, ksum = _hy_filter(L, HY_FILT_COLS, True, *filt)
    return _hy_conv(v, x1, x2, kc, ksum, hy_bias)


@functools.lru_cache(maxsize=None)
def _ctx_dft_consts(n):
    i = np.arange(n, dtype=np.float64)
    a = 2.0 * np.pi * np.outer(i, i) / n
    f = np.concatenate([np.cos(a), -np.sin(a)], axis=0)
    return np.asarray(f, np.float32), np.asarray(f.T[:n // 2], np.float32)


def _hyena_ctx_kernel(v_ref, x1_ref, x2_ref, kc_ref, ks_ref, bias_ref, ff_ref, fi_ref, o_ref):
    n = ff_ref.shape[1]
    lc = n // 2
    C = v_ref.shape[2]
    ff = ff_ref[...].astype(BF16)
    inv = 1.0 / (jnp.sum(ks_ref[...], axis=1, keepdims=True) * float(n))
    kn = (kc_ref[...] * inv).T
    kspec = jnp.dot(ff, kn.astype(BF16), preferred_element_type=F32)
    z = jnp.concatenate([v_ref[0], v_ref[1]], axis=1)
    for o, g_ref in enumerate((x1_ref, x2_ref)):
        x = jnp.dot(ff[:, :lc], z.astype(BF16), preferred_element_type=F32)
        k = kspec[:, o * C:(o + 1) * C]
        kr = jnp.concatenate([k[:n], k[:n]], axis=1)
        ki = jnp.concatenate([k[n:], k[n:]], axis=1)
        yr, yi = _cmul(x[:n], x[n:], kr, ki)
        y = jnp.concatenate([yr, yi], axis=0)
        conv = jnp.dot(fi_ref[...].astype(BF16), y.astype(BF16), preferred_element_type=F32)
        bias = jnp.concatenate([bias_ref[o:o + 1, :], bias_ref[o:o + 1, :]], axis=1)
        z = jnp.concatenate([g_ref[0], g_ref[1]], axis=1) * (conv + bias * z)
    o_ref[0] = z[:, :C]
    o_ref[1] = z[:, C:]


def _hyena_ctx(v, x1, x2, filt, hy_bias):
    B, Lc, C = v.shape
    assert B == 2
    kc, ksum = _hy_filter(Lc, Lc, False, *filt)
    ff, fi = (jnp.asarray(m, F32) for m in _ctx_dft_consts(2 * Lc))
    return pl.pallas_call(
        _hyena_ctx_kernel,
        out_shape=jax.ShapeDtypeStruct((B, Lc, C), F32),
        compiler_params=pltpu.CompilerParams(vmem_limit_bytes=VMEM_LIMIT),
        name="hyena_ctx",
    )(v, x1, x2, kc, ksum, hy_bias, ff, fi)


NA_ROWS_PER_STEP = 8
NA_SOFTMAX_ROWS = 32
_NT = (((1,), (1,)), ((), ()))


def _na_bias_table(rpb):
    qc = np.arange(GRID_W)[:, None]
    kc = np.arange(GRID_W)[None, :]
    start = np.clip(qc - NA_WIN_COLS // 2, 0, GRID_W - NA_WIN_COLS)
    valid = (kc >= start) & (kc < start + NA_WIN_COLS)
    pad = jnp.pad(rpb, ((0, 0), (0, 0), (GRID_W, GRID_W)))
    shift = GRID_W + NA_WIN_COLS - 1
    toep = jnp.stack([pad[:, :, shift - c:shift - c + GRID_W] for c in range(GRID_W)], axis=2)
    full = jnp.where(jnp.asarray(valid)[None, None], toep, NEG_INF)
    t = jnp.stack([full[:, d:d + NA_WIN_ROWS] for d in range(NA_WIN_ROWS)], axis=1)
    t = t.reshape(NA_HEADS // 2, 2, NA_WIN_ROWS, NA_WIN_ROWS, GRID_W, GRID_W).transpose(0, 2, 1, 4, 3, 5)
    return t.reshape(NA_HEADS // 2, NA_WIN_ROWS, 2 * GRID_W, NA_WIN_ROWS * GRID_W).astype(F32)


def _na_kernel(q_ref, kp_ref, kc_ref, kn_ref, vp_ref, vc_ref, vn_ref, ck_ref, cv_ref, bias_ref, o_ref,
               wk_ref, ws_ref, wv_ref, sc_ref, pc_ref, ol_ref, li_ref, sl_ref, pl_ref):
    i = pl.program_id(2)
    last = pl.num_programs(2) - 1
    blk = NA_ROWS_PER_STEP * GRID_W
    for n, (kr, vr) in enumerate(((kp_ref, vp_ref), (kc_ref, vc_ref), (kn_ref, vn_ref))):
        wk_ref[:, n * blk:(n + 1) * blk] = kr[0]
        wv_ref[n * blk:(n + 1) * blk] = vr[0]
    ws_ref[:, 0:3 * blk - GRID_W] = wk_ref[:, GRID_W:3 * blk]

    def key_window(off):
        if off % 2 == 0:
            return wk_ref[:, off * GRID_W:(off + NA_WIN_ROWS) * GRID_W]
        return ws_ref[:, (off - 1) * GRID_W:(off - 1 + NA_WIN_ROWS) * GRID_W]
    first_head = lax.broadcasted_iota(jnp.int32, (GRID_W, 2 * NA_HEAD_DIM), 1) < NA_HEAD_DIM
    q = q_ref[0]
    zero = jnp.zeros((GRID_W, 2 * NA_HEAD_DIM), q.dtype)
    pieces = []
    for j in range(NA_ROWS_PER_STEP):
        qj = q[j * GRID_W:(j + 1) * GRID_W]
        pieces += [jnp.where(first_head, qj, zero), jnp.where(first_head, zero, qj)]
    qs = jnp.concatenate(pieces, axis=0)
    sc_ref[...] = lax.dot_general(qs, ck_ref[0], _NT, preferred_element_type=F32)
    half = NA_WIN_ROWS // 2
    rows = 2 * GRID_W

    def window_row(j):
        return jnp.where(i == 0, max(j + half, NA_ROWS_PER_STEP),
                         jnp.where(i == last, min(j + half, NA_ROWS_PER_STEP), j + half))

    for j in range(NA_ROWS_PER_STEP):
        kw = key_window(j + half)
        if j < half:
            kw = jnp.where(i == 0, key_window(NA_ROWS_PER_STEP), kw)
        elif j > half:
            kw = jnp.where(i == last, key_window(NA_ROWS_PER_STEP), kw)
        sl_ref[j] = jnp.dot(qs[j * rows:(j + 1) * rows], kw, preferred_element_type=F32)
    for j in range(NA_ROWS_PER_STEP):
        d = window_row(j) - j - 1
        for c0 in range(0, rows, NA_SOFTMAX_ROWS):
            rs = slice(c0, c0 + NA_SOFTMAX_ROWS)
            ra = slice(j * rows + c0, j * rows + c0 + NA_SOFTMAX_ROWS)
            s_loc = sl_ref[j, rs, :] + bias_ref[0, d, rs, :]
            s_ctx = sc_ref[ra, :]
            m = jnp.maximum(jnp.max(s_loc, axis=1, keepdims=True), jnp.max(s_ctx, axis=1, keepdims=True))
            p_loc = jnp.exp(s_loc - m)
            p_ctx = jnp.exp(s_ctx - m)
            l = jnp.sum(p_loc, axis=1, keepdims=True) + jnp.sum(p_ctx, axis=1, keepdims=True)
            pl_ref[j, rs, :] = p_loc.astype(BF16)
            pc_ref[ra, :] = p_ctx.astype(pc_ref.dtype)
            li_ref[ra, :] = jnp.broadcast_to(1.0 / l, (NA_SOFTMAX_ROWS, 2 * NA_HEAD_DIM))
    for j in range(NA_ROWS_PER_STEP):
        start = pl.multiple_of(window_row(j) * GRID_W, GRID_W)
        vw = wv_ref[pl.ds(start, NA_WIN_ROWS * GRID_W), :]
        ol_ref[j * rows:(j + 1) * rows, :] = jnp.dot(pl_ref[j], vw, preferred_element_type=F32)
    o = (ol_ref[...] + jnp.dot(pc_ref[...], cv_ref[0], preferred_element_type=F32)) * li_ref[...]
    for j in range(NA_ROWS_PER_STEP):
        r0 = j * rows
        oj = jnp.where(first_head, o[r0:r0 + GRID_W], o[r0 + GRID_W:r0 + rows])
        o_ref[0, j * GRID_W:(j + 1) * GRID_W, :] = oj.astype(o_ref.dtype)


def _na(q, k, v, ck, cv, bias):
    B, L, W = q.shape
    Lc = ck.shape[1]
    blk = NA_ROWS_PER_STEP * GRID_W
    nblk = L // blk
    assert NA_ROWS_PER_STEP == NA_WIN_ROWS and nblk >= 2 and GRID_W * 2 == LANES
    pw = 2 * NA_HEAD_DIM
    cur = pl.BlockSpec((1, blk, pw), lambda b, h, i: (b, i, h))
    prev = pl.BlockSpec((1, blk, pw), lambda b, h, i: (b, jnp.maximum(i - 1, 0), h))
    nxt = pl.BlockSpec((1, blk, pw), lambda b, h, i: (b, jnp.minimum(i + 1, nblk - 1), h))
    kcur = pl.BlockSpec((1, pw, blk), lambda b, h, i: (b, h, i))
    kprev = pl.BlockSpec((1, pw, blk), lambda b, h, i: (b, h, jnp.maximum(i - 1, 0)))
    knxt = pl.BlockSpec((1, pw, blk), lambda b, h, i: (b, h, jnp.minimum(i + 1, nblk - 1)))
    cspec = pl.BlockSpec((1, Lc, pw), lambda b, h, i: (b, 0, h))
    bspec = pl.BlockSpec((1,) + bias.shape[1:], lambda b, h, i: (h, 0, 0, 0))
    stacked = 2 * blk
    return pl.pallas_call(
        _na_kernel,
        out_shape=jax.ShapeDtypeStruct((B, L, W), BF16),
        grid=(B, W // pw, nblk),
        in_specs=[cur, kprev, kcur, knxt, prev, cur, nxt, cspec, cspec, bspec],
        out_specs=cur,
        scratch_shapes=[pltpu.VMEM((pw, 3 * blk), BF16), pltpu.VMEM((pw, 3 * blk), BF16),
                        pltpu.VMEM((3 * blk, pw), BF16),
                        pltpu.VMEM((stacked, Lc), F32), pltpu.VMEM((stacked, Lc), BF16),
                        pltpu.VMEM((stacked, pw), F32), pltpu.VMEM((stacked, pw), F32),
                        pltpu.VMEM((NA_ROWS_PER_STEP, 2 * GRID_W, NA_WIN_ROWS * GRID_W), F32),
                        pltpu.VMEM((NA_ROWS_PER_STEP, 2 * GRID_W, NA_WIN_ROWS * GRID_W), BF16)],
        compiler_params=pltpu.CompilerParams(vmem_limit_bytes=VMEM_LIMIT),
        name="nattn",
    )(q, k, k, k, v, v, v, ck, cv, bias)


def _ctx_attn_kernel(q_ref, k_ref, v_ref, o_ref):
    q = q_ref[0]
    k = k_ref[0]
    v = v_ref[0]
    first_head = lax.broadcasted_iota(jnp.int32, q.shape, 1) < NA_HEAD_DIM
    outs = []
    for h in range(2):
        qm = jnp.where(first_head if h == 0 else jnp.logical_not(first_head), q, jnp.zeros_like(q))
        s = lax.dot_general(qm, k, _NT, preferred_element_type=F32)
        p = jnp.exp(s - jnp.max(s, axis=1, keepdims=True))
        o = jnp.dot(p.astype(BF16), v, preferred_element_type=F32)
        outs.append(o / jnp.sum(p, axis=1, keepdims=True))
    o_ref[0] = jnp.where(first_head, outs[0], outs[1]).astype(o_ref.dtype)


def _ctx_attn(q, k, v):
    B, Lc, W = q.shape
    pw = 2 * NA_HEAD_DIM
    spec = pl.BlockSpec((1, Lc, pw), lambda b, h: (b, 0, h))
    return pl.pallas_call(
        _ctx_attn_kernel,
        out_shape=jax.ShapeDtypeStruct((B, Lc, W), BF16),
        grid=(B, W // pw),
        in_specs=[spec, spec, spec],
        out_specs=spec,
        name="ctx_attn",
    )(q, k, v)


LRU_CHUNK = 512


def _lru_gate_weights(wa, ba, wi, bi, lam):
    def bd(w):
        return jax.scipy.linalg.block_diag(*[w[n] for n in range(LRU_BLOCKS)])
    wg = jnp.stack([jnp.concatenate([bd(wa[d]), bd(wi[d])], axis=1) for d in range(2)]).astype(BF16)
    bg = jnp.stack([jnp.concatenate([ba[d], bi[d]])[None, :] for d in range(2)])
    return wg, bg, lam[:, None, :]


def _lru_coeffs(u, wg, bg, lam):
    C = u.shape[1]
    g = jnp.dot(u.astype(BF16), wg, preferred_element_type=F32) + bg
    sig = 0.5 + 0.5 * jnp.tanh(0.5 * g)
    r, ig = sig[:, :C], sig[:, C:]
    nl = -lam
    softplus = jnp.maximum(nl, 0.0) + jnp.log(1.0 + jnp.exp(-jnp.abs(nl)))
    log_a = (-LRU_C * softplus) * r
    a = jnp.exp(log_a)
    t = jnp.tanh(log_a)
    b = jnp.sqrt(-2.0 * t / (1.0 - t)) * (ig * u)
    return a, b


def _lru_scan(a, b, h0, reverse, ac_ref, bc_ref, h_ref):
    T, C = a.shape
    row = lax.broadcasted_iota(jnp.int32, a.shape, 0) % SUBLANES
    for s in (1, 2, 4):
        shift = T - s if reverse else s
        keep = (row < SUBLANES - s) if reverse else (row >= s)
        b = jnp.where(keep, a * pltpu.roll(b, shift, 0) + b, b)
        a = jnp.where(keep, a * pltpu.roll(a, shift, 0), a)
    ac_ref[...] = a
    bc_ref[...] = b
    ng = T // SUBLANES

    def group(g, h):
        r0 = pl.multiple_of((ng - 1 - g if reverse else g) * SUBLANES, SUBLANES)
        hr = ac_ref[pl.ds(r0, SUBLANES), :] * h + bc_ref[pl.ds(r0, SUBLANES), :]
        h_ref[pl.ds(r0, SUBLANES), :] = hr
        edge = hr[0:1] if reverse else hr[SUBLANES - 1:SUBLANES]
        return jnp.broadcast_to(edge, (SUBLANES, C))

    return lax.fori_loop(0, ng, group, h0, unroll=4)


def _gelu_tanh(x):
    return 0.5 * x * (1.0 + jnp.tanh(math.sqrt(2.0 / math.pi) * (x + 0.044715 * (x * x * x))))


def _lru_ctx_kernel(u_ref, xg_ref, wg_ref, bg_ref, lam_ref, hend_ref, yc_ref, ac_ref, bc_ref, h_ref):
    u = u_ref[0]
    C = u.shape[1]
    total = jnp.zeros_like(u)
    for d, rev in enumerate((False, True)):
        a, b = _lru_coeffs(u, wg_ref[d], bg_ref[d], lam_ref[d])
        hl = _lru_scan(a, b, jnp.zeros((SUBLANES, C), F32), rev, ac_ref, bc_ref, h_ref)
        hend_ref[0, d:d + 1, :] = hl[0:1]
        total = total + h_ref[...]
    yc_ref[0] = (total * _gelu_tanh(xg_ref[0])).astype(yc_ref.dtype)


def _lru_ctx(u, xg, wg, bg, lam):
    B, Lc, C = u.shape
    tok = pl.BlockSpec((1, Lc, C), lambda b: (b, 0, 0))
    full = lambda a: pl.BlockSpec(a.shape, lambda b: (0,) * a.ndim)
    return pl.pallas_call(
        _lru_ctx_kernel,
        out_shape=[jax.ShapeDtypeStruct((B, 2, C), F32), jax.ShapeDtypeStruct((B, Lc, C), BF16)],
        grid=(B,),
        in_specs=[tok, tok, full(wg), full(bg), full(lam)],
        out_specs=[pl.BlockSpec((1, 2, C), lambda b: (b, 0, 0)), tok],
        scratch_shapes=[pltpu.VMEM((Lc, C), F32)] * 3,
        name="lru_ctx",
    )(u, xg, wg, bg, lam)


def _lru_dir_kernel(*refs, d, reverse):
    if reverse:
        u_ref, hend_ref, wg_ref, bg_ref, lam_ref, hf_ref, xg_ref, o_ref, ac_ref, bc_ref, h_ref, carry_ref = refs
    else:
        u_ref, hend_ref, wg_ref, bg_ref, lam_ref, o_ref, ac_ref, bc_ref, carry_ref = refs
        h_ref = o_ref.at[0]
    C = u_ref.shape[2]

    @pl.when(pl.program_id(1) == 0)
    def _():
        carry_ref[...] = jnp.broadcast_to(hend_ref[0, d:d + 1, :], (SUBLANES, C))

    a, b = _lru_coeffs(u_ref[0], wg_ref[d], bg_ref[d], lam_ref[d])
    carry_ref[...] = _lru_scan(a, b, carry_ref[...], reverse, ac_ref, bc_ref, h_ref)
    if reverse:
        o_ref[0] = ((hf_ref[0] + h_ref[...]) * _gelu_tanh(xg_ref[0])).astype(o_ref.dtype)


def _lru_dir(u, hend, wg, bg, lam, hf=None, xg=None):
    B, L, C = u.shape
    reverse = hf is not None
    T = LRU_CHUNK
    nb = L // T
    tok = pl.BlockSpec((1, T, C), (lambda b, i: (b, nb - 1 - i, 0)) if reverse else (lambda b, i: (b, i, 0)))
    full = lambda a: pl.BlockSpec(a.shape, lambda b, i: (0,) * a.ndim)
    ins = [u, hend, wg, bg, lam] + ([hf, xg] if reverse else [])
    specs = [tok, pl.BlockSpec((1, 2, C), lambda b, i: (b, 0, 0)), full(wg), full(bg), full(lam)]
    specs += [tok, tok] if reverse else []
    scratch = [pltpu.VMEM((T, C), F32)] * (3 if reverse else 2) + [pltpu.VMEM((SUBLANES, C), F32)]
    return pl.pallas_call(
        functools.partial(_lru_dir_kernel, d=int(reverse), reverse=reverse),
        out_shape=jax.ShapeDtypeStruct((B, L, C), BF16 if reverse else F32),
        grid=(B, nb),
        in_specs=specs,
        out_specs=tok,
        scratch_shapes=scratch,
        compiler_params=pltpu.CompilerParams(dimension_semantics=("arbitrary", "arbitrary")),
        name="lru_bwd" if reverse else "lru_fwd",
    )(*ins)


def _lru(u, xg, u_c, xg_c, wa, ba, wi, bi, lam):
    wg, bg, lam3 = _lru_gate_weights(wa, ba, wi, bi, lam)
    hend, yc = _lru_ctx(u_c, xg_c, wg, bg, lam3)
    hf = _lru_dir(u, hend, wg, bg, lam3)
    return _lru_dir(u, hend, wg, bg, lam3, hf, xg), yc


def kernel(x, c, ctx, c_ctx, ada_w, ada_b, g_mix_pre, g_mix_post, g_ffn_pre, g_ffn_post, w_in, w_out, hy_conv_w,
           hy_conv_b, hy_f_w1, hy_f_b1, hy_f_w2, hy_f_b2, hy_f_w3, hy_f_freq, hy_bias, na_rpb, lru_conv_w,
           lru_conv_b, lru_wa, lru_ba, lru_wi, lru_bi, lru_lam, ffn_w_gu, ffn_w_down):
    B, L, D = x.shape
    Lc = ctx.shape[1]
    tm = 512

    assert B + 1 <= SUBLANES
    cond_t = jnp.zeros((D, SUBLANES), F32).at[:, 0:B].set(c.T).at[:, B].set(c_ctx)
    mods = _modulation(cond_t, B + 1, ada_w, ada_b)

    xc = ctx
    for l in range(DEPTH):
        with_ctx_out = l < DEPTH - 1
        m = mods[l].reshape(8, 6, D)
        lat = [m[0:B, j][:, None, :] for j in range(6)]
        cx = [jnp.broadcast_to(m[B, j][None, None, :], (B, 1, D)) for j in range(6)]
        row = lambda a: a.reshape(1, -1)

        w_in_bf = w_in[l].astype(BF16)
        w_out_bf = w_out[l].astype(BF16)
        nchunk = D_FF // FF_CHUNK
        wg = ffn_w_gu[l][:, :D_FF].astype(BF16).reshape(D, nchunk, FF_CHUNK).transpose(1, 0, 2)
        wu = ffn_w_gu[l][:, D_FF:].astype(BF16).reshape(D, nchunk, FF_CHUNK).transpose(1, 0, 2)
        wd = ffn_w_down[l].astype(BF16).reshape(nchunk, FF_CHUNK, D)

        conv_args = (hy_conv_w[l], row(hy_conv_b[l]), lru_conv_w[l], row(lru_conv_b[l]))
        hv, hx1, hx2, q, k, v, lu, lg = _inproj(x, lat[1], lat[0], row(g_mix_pre[l]), w_in_bf, *conv_args, tm=tm,
                                                chan_major=True)
        cv, cx1, cx2, cq, ck, cvv, clu, clg = _inproj(xc, cx[1], cx[0], row(g_mix_pre[l]), w_in_bf, *conv_args,
                                                      tm=Lc, chan_major=False)

        filt = (hy_f_w1[l], hy_f_b1[l], hy_f_w2[l], hy_f_b2[l], hy_f_w3[l], hy_f_freq[l])
        y_hy = _hyena(hv, hx1, hx2, filt, hy_bias[l])
        y_na = _na(q, k, v, ck, cvv, _na_bias_table(na_rpb[l]))
        y_lru, yc_lru = _lru(lu, lg, clu, clg, lru_wa[l], lru_ba[l], lru_wi[l], lru_bi[l], lru_lam[l])

        ffn_args = (row(g_ffn_pre[l]), row(g_ffn_post[l]), wg, wu, wd)
        x = _out_ffn(y_hy, y_na, y_lru, w_out_bf, x, row(g_mix_post[l]), lat[2], lat[4], lat[3], lat[5], *ffn_args,
                     tm=tm, chan_major=True)

        if with_ctx_out:
            yc_hy = _hyena_ctx(cv, cx1, cx2, filt, hy_bias[l])
            yc_na = _ctx_attn(cq, ck, cvv)
            xc = _out_ffn(yc_hy, yc_na, yc_lru, w_out_bf, xc, row(g_mix_post[l]), cx[2], cx[4], cx[3], cx[5],
                          *ffn_args, tm=Lc, chan_major=False)
    return x
```

```python
import functools
import math

import jax
import jax.numpy as jnp
import numpy as np
from jax import lax
from jax.experimental import pallas as pl
from jax.experimental.pallas import tpu as pltpu

F32 = jnp.float32
BF16 = jnp.bfloat16

D_MODEL = 1024
DEPTH = 2
GRID_W = 64
HY_WIDTH = D_MODEL // 4
NA_HEAD_DIM = 64
NA_WIDTH = D_MODEL // 2
NA_HEADS = NA_WIDTH // NA_HEAD_DIM
LRU_WIDTH = D_MODEL // 4
LRU_BLOCKS = 4
IN_WIDTH = 3 * HY_WIDTH + 3 * NA_WIDTH + 2 * LRU_WIDTH
HY_ORDER = 2
HY_BANDS = 16
HY_FAST_DECAY = 0.3
HY_SLOW_DECAY = 1.5
HY_DECAY_TARGET = 1e-2
NA_WIN_ROWS = 8
NA_WIN_COLS = 16
LRU_C = 8.0
D_FF = -(-8 * D_MODEL // (3 * 256)) * 256
RMS_EPS = 1e-6
NEG_INF = -1e30

_HY_END = 3 * HY_WIDTH
_NA_END = _HY_END + 3 * NA_WIDTH
_LRU_MID = _NA_END + LRU_WIDTH

HALO = 8
SUBLANES = 8
VMEM_LIMIT = 56 * 1024 * 1024


def _rms(x, g):
    return x * lax.rsqrt(jnp.mean(x * x, axis=-1, keepdims=True) + RMS_EPS) * g


def _mod_kernel(ct_ref, w_ref, b_ref, o_ref, *, n_cond):
    ct = ct_ref[...]
    st = ct * jax.nn.sigmoid(ct)
    w = w_ref[0]
    rows = [jnp.sum(w * st[:, r:r + 1], axis=0, keepdims=True) for r in range(n_cond)]
    rows.append(jnp.zeros((SUBLANES - n_cond, w.shape[1]), F32))
    o_ref[0] = jnp.concatenate(rows, axis=0) + b_ref[0]


def _modulation(cond_t, n_cond, ada_w, ada_b):
    tn = 768
    n = ada_w.shape[-1]
    return pl.pallas_call(
        functools.partial(_mod_kernel, n_cond=n_cond),
        out_shape=jax.ShapeDtypeStruct((DEPTH, SUBLANES, n), F32),
        grid=(DEPTH, n // tn),
        in_specs=[pl.BlockSpec((D_MODEL, SUBLANES), lambda l, j: (0, 0)),
                  pl.BlockSpec((1, D_MODEL, tn), lambda l, j: (l, 0, j)),
                  pl.BlockSpec((1, 1, tn), lambda l, j: (l, 0, j))],
        out_specs=pl.BlockSpec((1, SUBLANES, tn), lambda l, j: (l, 0, j)),
        compiler_params=pltpu.CompilerParams(vmem_limit_bytes=VMEM_LIMIT),
        name="adaln_modulation",
    )(cond_t, ada_w, ada_b.reshape(DEPTH, 1, n))


CM_LANES = 256
LANES = 128
CM_HALVES = CM_LANES // LANES


def _chan_major_shape(B, L, C):
    return (B, C // SUBLANES, CM_HALVES, L // CM_LANES * SUBLANES, LANES)


def _chan_major_spec(tm, C):
    return pl.BlockSpec((1, C // SUBLANES, CM_HALVES, tm // CM_LANES * SUBLANES, LANES),
                        lambda b, i: (b, 0, 0, i, 0))


def _store_chan_major(ref, u):
    ut = u.T
    for g in range(u.shape[1] // SUBLANES):
        for j in range(u.shape[0] // CM_LANES):
            for h in range(CM_HALVES):
                lane0 = j * CM_LANES + h * LANES
                ref[0, g, h, j * SUBLANES:(j + 1) * SUBLANES, :] = ut[g * SUBLANES:(g + 1) * SUBLANES,
                                                                      lane0:lane0 + LANES]


def _load_chan_major(ref):
    _, ng, _, nr, _ = ref.shape
    rows = [jnp.concatenate([ref[0, g, h, j * SUBLANES:(j + 1) * SUBLANES, :]
                             for j in range(nr // SUBLANES) for h in range(CM_HALVES)], axis=1) for g in range(ng)]
    return jnp.concatenate(rows, axis=0).T


def _chan_rows(ref, lead, c, n):
    return jnp.concatenate([ref.at[lead + (h,)][pl.ds(c, n, stride=SUBLANES), :] for h in range(CM_HALVES)], axis=1)


def _inproj_kernel(xp_ref, xc_ref, xn_ref, sc_ref, sh_ref, g_ref, w_ref, hcw_ref, hcb_ref, lcw_ref, lcb_ref,
                   hv_ref, hx1_ref, hx2_ref, q_ref, k_ref, v_ref, lu_ref, lg_ref, pe_ref, *, tm, chan_major):
    i = pl.program_id(1)
    last = pl.num_programs(1) - 1
    g = g_ref[...]
    sc1 = 1.0 + sc_ref[0]
    sh = sh_ref[0]

    def norm_mod(xv):
        return _rms(xv, g) * sc1 + sh

    hp = norm_mod(xp_ref[0]) * (i > 0).astype(F32)
    hn = norm_mod(xn_ref[0]) * (i < last).astype(F32)
    he = jnp.concatenate([hp, norm_mod(xc_ref[0]), hn], axis=0).astype(BF16)

    pe_ref[:, 0:_HY_END] = jnp.dot(he, w_ref[:, 0:_HY_END], preferred_element_type=F32)
    pe_ref[:, _HY_END:] = jnp.dot(he, w_ref[:, _NA_END:_LRU_MID], preferred_element_type=F32)
    hc = he[HALO:HALO + tm]
    qkv = jnp.dot(hc, w_ref[:, _HY_END:_NA_END], preferred_element_type=F32)
    q_ref[0] = (qkv[:, 0:NA_WIDTH] * (NA_HEAD_DIM ** -0.5)).astype(BF16)
    kk = qkv[:, NA_WIDTH:2 * NA_WIDTH]
    k_ref[0] = (kk.T if chan_major else kk).astype(BF16)
    v_ref[0] = qkv[:, 2 * NA_WIDTH:].astype(BF16)
    lg_ref[0] = jnp.dot(hc, w_ref[:, _LRU_MID:], preferred_element_type=F32)

    u = hcb_ref[...]
    for kk in range(3):
        u = u + hcw_ref[kk:kk + 1, :] * pe_ref[pl.ds(HALO - 1 + kk, tm), 0:_HY_END]
    for n, ref in enumerate((hv_ref, hx1_ref, hx2_ref)):
        un = u[:, n * HY_WIDTH:(n + 1) * HY_WIDTH]
        if chan_major:
            _store_chan_major(ref, un)
        else:
            ref[0] = un
    ul = lcb_ref[...]
    for kk in range(4):
        ul = ul + lcw_ref[kk:kk + 1, :] * pe_ref[pl.ds(HALO - 2 + kk, tm), _HY_END:]
    lu_ref[0] = ul


def _inproj(x, sc, sh, g, w_bf, hcw, hcb, lcw, lcb, *, tm, chan_major):
    B, L, D = x.shape
    nb = tm // HALO
    nh = L // HALO
    tok = lambda w, dt: jax.ShapeDtypeStruct((B, L, w), dt)
    tspec = lambda w: pl.BlockSpec((1, tm, w), lambda b, i: (b, i, 0))
    full = lambda a: pl.BlockSpec(a.shape, lambda b, i: (0,) * a.ndim)
    vec = pl.BlockSpec((1, 1, D), lambda b, i: (b, 0, 0))
    if chan_major:
        hy_shape = jax.ShapeDtypeStruct(_chan_major_shape(B, L, HY_WIDTH), F32)
        hy_spec = _chan_major_spec(tm, HY_WIDTH)
        k_shape = jax.ShapeDtypeStruct((B, NA_WIDTH, L), BF16)
        k_spec = pl.BlockSpec((1, NA_WIDTH, tm), lambda b, i: (b, 0, i))
    else:
        hy_shape, hy_spec = tok(HY_WIDTH, F32), tspec(HY_WIDTH)
        k_shape, k_spec = tok(NA_WIDTH, BF16), tspec(NA_WIDTH)
    return pl.pallas_call(
        functools.partial(_inproj_kernel, tm=tm, chan_major=chan_major),
        out_shape=[hy_shape] * 3 + [tok(NA_WIDTH, BF16), k_shape, tok(NA_WIDTH, BF16)] + [tok(LRU_WIDTH, F32)] * 2,
        grid=(B, L // tm),
        in_specs=[pl.BlockSpec((1, HALO, D), lambda b, i: (b, jnp.maximum(i * nb - 1, 0), 0)),
                  pl.BlockSpec((1, tm, D), lambda b, i: (b, i, 0)),
                  pl.BlockSpec((1, HALO, D), lambda b, i: (b, jnp.minimum((i + 1) * nb, nh - 1), 0)),
                  vec, vec, full(g), full(w_bf), full(hcw), full(hcb), full(lcw), full(lcb)],
        out_specs=[hy_spec] * 3 + [tspec(NA_WIDTH), k_spec, tspec(NA_WIDTH)] + [tspec(LRU_WIDTH)] * 2,
        scratch_shapes=[pltpu.VMEM((tm + 2 * HALO, _HY_END + LRU_WIDTH), F32)],
        compiler_params=pltpu.CompilerParams(vmem_limit_bytes=VMEM_LIMIT),
        name="inproj",
    )(x, x, x, sc, sh, g, w_bf, hcw, hcb, lcw, lcb)


FF_CHUNK = 256


def _out_ffn_kernel(yh_ref, yn_ref, yl_ref, wo_ref, x_ref, gm_ref, gtm_ref, sc_ref, sh_ref, gt_ref, gpre_ref,
                    gpost_ref, wgu_ref, wd_ref, o_ref, *, chan_major):
    yh = _load_chan_major(yh_ref) if chan_major else yh_ref[0]
    y = jnp.dot(yh.astype(BF16), wo_ref[0:HY_WIDTH], preferred_element_type=F32)
    y = y + jnp.dot(yn_ref[0], wo_ref[HY_WIDTH:HY_WIDTH + NA_WIDTH], preferred_element_type=F32)
    y = y + jnp.dot(yl_ref[0], wo_ref[HY_WIDTH + NA_WIDTH:], preferred_element_type=F32)
    x = x_ref[0] + gtm_ref[0] * _rms(y, gm_ref[...])
    h = (_rms(x, gpre_ref[...]) * (1.0 + sc_ref[0]) + sh_ref[0]).astype(BF16)
    acc = jnp.zeros(x.shape, F32)
    for c0 in range(0, D_FF, FF_CHUNK):
        g = jnp.dot(h, wgu_ref[:, c0:c0 + FF_CHUNK], preferred_element_type=F32)
        u = jnp.dot(h, wgu_ref[:, D_FF + c0:D_FF + c0 + FF_CHUNK], preferred_element_type=F32)
        a = (g * jax.nn.sigmoid(g) * u).astype(BF16)
        acc = acc + jnp.dot(a, wd_ref[c0:c0 + FF_CHUNK, :], preferred_element_type=F32)
    o_ref[0] = x + gt_ref[0] * _rms(acc, gpost_ref[...])


def _out_ffn(yh, yn, yl, wo, x, gm, gtm, sc, sh, gt, gpre, gpost, wgu, wd, *, tm, chan_major):
    B, L, D = x.shape
    tspec = lambda w: pl.BlockSpec((1, tm, w), lambda b, i: (b, i, 0))
    vec = pl.BlockSpec((1, 1, D), lambda b, i: (b, 0, 0))
    full = lambda a: pl.BlockSpec(a.shape, lambda b, i: (0,) * a.ndim)
    res = lambda a: pl.BlockSpec(a.shape, lambda b, i: (0,) * a.ndim, pipeline_mode=pl.Buffered(1))
    hy_spec = _chan_major_spec(tm, HY_WIDTH) if chan_major else tspec(HY_WIDTH)
    return pl.pallas_call(
        functools.partial(_out_ffn_kernel, chan_major=chan_major),
        out_shape=jax.ShapeDtypeStruct((B, L, D), F32),
        grid=(B, L // tm),
        in_specs=[hy_spec, tspec(NA_WIDTH), tspec(LRU_WIDTH), res(wo), tspec(D), full(gm), vec,
                  vec, vec, vec, full(gpre), full(gpost), res(wgu), res(wd)],
        out_specs=tspec(D),
        compiler_params=pltpu.CompilerParams(vmem_limit_bytes=VMEM_LIMIT),
        name="out_ffn",
    )(yh, yn, yl, wo, x, gm, gtm, sc, sh, gt, gpre, gpost, wgu, wd)


HY_N1 = CM_LANES
HY_N2 = 128
HY_FILT_COLS = 2048


def _block_cplx(re, im):
    return np.block([[re, -im], [im, re]])


@functools.lru_cache(maxsize=None)
def _dft_consts():
    n = HY_N1 * HY_N2
    h = HY_N2 // 2
    i1 = np.arange(HY_N1, dtype=np.float64)
    i2 = np.arange(HY_N2, dtype=np.float64)
    a1 = 2.0 * np.pi * np.outer(i1, i1) / HY_N1
    a2 = 2.0 * np.pi * np.outer(i2, i2) / HY_N2
    at = 2.0 * np.pi * np.outer(i2, i1) / n
    r2, m2 = np.cos(a2), -np.sin(a2)
    r1, m1 = np.cos(a1), -np.sin(a1)
    f32 = lambda m: np.asarray(m, np.float32)
    return dict(
        rows_data=f32(_block_cplx(r2[:, :h], m2[:, :h])),
        rows_filt=f32(np.concatenate([r2, m2], axis=0)),
        rows_out=f32(_block_cplx(r2[:h, :], m2[:h, :])),
        lanes=f32(np.block([[r1, m1], [-m1, r1]])),
        twr=f32(np.cos(at)), twi=f32(-np.sin(at)))


def _dft_const(name):
    return jnp.asarray(_dft_consts()[name], F32)


def _cmul(ar, ai, br, bi):
    return ar * br - ai * bi, ar * bi + ai * br


def _hy_conv_kernel(v_ref, x1_ref, x2_ref, k0_ref, k1_ref, ks0_ref, ks1_ref, bias_ref, rd_ref, rf_ref, ro_ref,
                    ln_ref, twr_ref, twi_ref, o_ref, st_ref, ks_ref):
    g = pl.program_id(0)
    n_ch = SUBLANES
    h = HY_N2 // 2
    rows_data = rd_ref[...].astype(BF16)
    rows_filt = rf_ref[...].astype(BF16)
    rows_out = ro_ref[...].astype(BF16)
    lanes = ln_ref[...].astype(BF16)
    twr, twi = twr_ref[...], twi_ref[...]

    def rows_then_twiddle(mat, x, c):
        a = jnp.dot(mat, x.astype(BF16), preferred_element_type=F32)
        ar, ai = _cmul(a[:HY_N2], a[HY_N2:], twr, twi)
        st_ref[c * HY_N2:(c + 1) * HY_N2, 0:HY_N1] = ar.astype(BF16)
        st_ref[c * HY_N2:(c + 1) * HY_N2, HY_N1:] = ai.astype(BF16)

    for o, (k_ref, s_ref) in enumerate(((k0_ref, ks0_ref), (k1_ref, ks1_ref))):
        for c in range(n_ch):
            rows_then_twiddle(rows_filt, _chan_rows(k_ref, (0,), c, HY_N2), c)
        spec = jnp.dot(st_ref[...], lanes, preferred_element_type=F32)
        inv = 1.0 / (jnp.sum(s_ref[...], axis=1, keepdims=True) * float(HY_N1 * HY_N2))
        for c in range(n_ch):
            ks_ref[o, c * HY_N2:(c + 1) * HY_N2, :] = spec[c * HY_N2:(c + 1) * HY_N2] * inv[c:c + 1, :]

    z = [[_chan_rows(v_ref, (b, 0), c, h) for c in range(n_ch)] for b in range(2)]
    for o, gate_ref in enumerate((x1_ref, x2_ref)):
        for c in range(n_ch):
            rows_then_twiddle(rows_data, jnp.concatenate([z[0][c], z[1][c]], axis=0), c)
        x = jnp.dot(st_ref[...], lanes, preferred_element_type=F32)
        pr, pi = _cmul(x[:, :HY_N1], x[:, HY_N1:], ks_ref[o, :, 0:HY_N1], ks_ref[o, :, HY_N1:])
        st_ref[:, 0:HY_N1] = pr.astype(BF16)
        st_ref[:, HY_N1:] = (-pi).astype(BF16)
        y = jnp.dot(st_ref[...], lanes, preferred_element_type=F32)
        for c in range(n_ch):
            yc = y[c * HY_N2:(c + 1) * HY_N2]
            yr, yi = _cmul(yc[:, :HY_N1], yc[:, HY_N1:], twr, twi)
            out = jnp.dot(rows_out, jnp.concatenate([yr, yi], axis=0).astype(BF16),
                          preferred_element_type=F32)
            bias = bias_ref[o, g * n_ch + c]
            for b, conv in ((0, out[:h]), (1, -out[h:])):
                z[b][c] = _chan_rows(gate_ref, (b, 0), c, h) * (conv + bias * z[b][c])
    for b in range(2):
        for c in range(n_ch):
            for hh in range(CM_HALVES):
                o_ref.at[b, 0, hh][pl.ds(c, h, stride=SUBLANES), :] = z[b][c][:, hh * LANES:(hh + 1) * LANES]


def _hy_conv(v, x1, x2, kc, ksum, hy_bias):
    B, ng = v.shape[0], v.shape[1]
    assert B == 2 and v.shape[3] * 2 == HY_N2 * SUBLANES and kc.shape[2] == HY_N2 * SUBLANES
    consts = [_dft_const(n) for n in ("rows_data", "rows_filt", "rows_out", "lanes", "twr", "twi")]
    tok = pl.BlockSpec((B, 1) + v.shape[2:], lambda g: (0, g, 0, 0, 0))
    kspec = lambda o: pl.BlockSpec((1,) + kc.shape[1:], lambda g: (o * ng + g, 0, 0, 0))
    sspec = lambda o: pl.BlockSpec((SUBLANES, ksum.shape[1]), lambda g: (o * ng + g, 0))
    full = lambda a: pl.BlockSpec(a.shape, lambda g: (0,) * a.ndim)
    return pl.pallas_call(
        _hy_conv_kernel,
        out_shape=jax.ShapeDtypeStruct(v.shape, F32),
        grid=(ng,),
        in_specs=[tok, tok, tok, kspec(0), kspec(1), sspec(0), sspec(1),
                  pl.BlockSpec(memory_space=pltpu.SMEM)] + [full(a) for a in consts],
        out_specs=tok,
        scratch_shapes=[pltpu.VMEM((SUBLANES * HY_N2, 2 * HY_N1), BF16),
                        pltpu.VMEM((HY_ORDER, SUBLANES * HY_N2, 2 * HY_N1), F32)],
        compiler_params=pltpu.CompilerParams(vmem_limit_bytes=VMEM_LIMIT),
        name="hy_conv",
    )(v, x1, x2, kc, kc, ksum, ksum, hy_bias, *consts)


def _hy_filter_kernel(w1t_ref, w1c_ref, w1s_ref, b1_ref, f0_ref, w2_ref, b2_ref, f1_ref, w3_ref, dl_ref,
                      k_ref, s_ref, *, L, P, chan_major):
    i = pl.program_id(0)

    def pos_t(rows):
        n = i * P + lax.broadcasted_iota(jnp.int32, (rows, P), 1)
        return n, jnp.where(n < L, n, 2 * L - n).astype(F32) / L

    _, tb = pos_t(HY_BANDS)
    band = (lax.broadcasted_iota(jnp.int32, tb.shape, 0) + 1).astype(F32)
    ang = 2.0 * math.pi * tb * band
    _, t1 = pos_t(1)
    h = w1t_ref[...] * t1 + b1_ref[...]
    h = h + jnp.dot(w1c_ref[...], jnp.cos(ang).astype(BF16), preferred_element_type=F32)
    h = h + jnp.dot(w1s_ref[...], jnp.sin(ang).astype(BF16), preferred_element_type=F32)
    h = jnp.sin(f0_ref[...] * h)
    h = jnp.sin(f1_ref[...] * (jnp.dot(w2_ref[...], h.astype(BF16), preferred_element_type=F32) + b2_ref[...]))
    k = jnp.dot(w3_ref[0], h.astype(BF16), preferred_element_type=F32)
    n, t = pos_t(1)
    k = jnp.where(n == L, 0.0, k * jnp.exp(-t * dl_ref[...]))
    if chan_major:
        for j in range(P // CM_LANES):
            for hh in range(CM_HALVES):
                lane0 = j * CM_LANES + hh * LANES
                k_ref[:, hh, j * SUBLANES:(j + 1) * SUBLANES, :] = k[:, lane0:lane0 + LANES].reshape(
                    k_ref.shape[0], SUBLANES, LANES)
    else:
        k_ref[...] = k
    ka = jnp.abs(k)
    tot = ka[:, 0:128]
    for j in range(1, P // 128):
        tot = tot + ka[:, j * 128:(j + 1) * 128]

    @pl.when(i == 0)
    def _():
        s_ref[...] = jnp.zeros_like(s_ref)

    s_ref[...] += tot


def _hy_filter(L, P, chan_major, f_w1, f_b1, f_w2, f_b2, f_w3, f_freq):
    nb = 2 * L // P
    assert nb % 2 == 0 and P % CM_LANES == 0
    hid = f_w2.shape[0]
    oc = HY_ORDER * HY_WIDTH
    col = lambda v: v.reshape(-1, 1)
    w3 = f_w3.reshape(hid, HY_ORDER, 2, HY_WIDTH)
    w3 = jnp.stack([w3[:, :, d].reshape(hid, oc).T for d in range(2)]).astype(BF16)
    deltas = np.abs(np.linspace(math.log(HY_DECAY_TARGET) / HY_FAST_DECAY,
                                math.log(HY_DECAY_TARGET) / HY_SLOW_DECAY, HY_WIDTH, dtype=np.float32))
    ins = [col(f_w1[0]), f_w1[1:1 + HY_BANDS].T.astype(BF16), f_w1[1 + HY_BANDS:].T.astype(BF16), col(f_b1),
           col(f_freq[0]), f_w2.T.astype(BF16), col(f_b2), col(f_freq[1]), w3,
           jnp.asarray(np.tile(deltas, HY_ORDER)[:, None])]
    full = lambda a: pl.BlockSpec(a.shape, lambda i: (0,) * a.ndim)
    specs = [full(a) for a in ins]
    specs[8] = pl.BlockSpec((1,) + w3.shape[1:], lambda i: (i // (nb // 2), 0, 0))
    if chan_major:
        kshape = (oc // SUBLANES, CM_HALVES, 2 * L // CM_LANES * SUBLANES, LANES)
        kspec = pl.BlockSpec((oc // SUBLANES, CM_HALVES, P // CM_LANES * SUBLANES, LANES), lambda i: (0, 0, i, 0))
    else:
        kshape = (oc, 2 * L)
        kspec = pl.BlockSpec((oc, P), lambda i: (0, i))
    return pl.pallas_call(
        functools.partial(_hy_filter_kernel, L=L, P=P, chan_major=chan_major),
        out_shape=[jax.ShapeDtypeStruct(kshape, F32), jax.ShapeDtypeStruct((oc, 128), F32)],
        grid=(nb,),
        in_specs=specs,
        out_specs=[kspec, pl.BlockSpec((oc, 128), lambda i: (0, 0))],
        compiler_params=pltpu.CompilerParams(dimension_semantics=("arbitrary",), vmem_limit_bytes=VMEM_LIMIT),
        name="hy_filter",
    )(*ins)


def _hyena(v, x1, x2, filt, hy_bias):
    L = v.shape[3] // SUBLANES * CM_LANES
    assert 2 * L == HY_N1 * HY_N2
    kc, ksum = _hy_filter(L, HY_FILT_COLS, True, *filt)
    return _hy_conv(v, x1, x2, kc, ksum, hy_bias)


@functools.lru_cache(maxsize=None)
def _ctx_dft_consts(n):
    i = np.arange(n, dtype=np.float64)
    a = 2.0 * np.pi * np.outer(i, i) / n
    f = np.concatenate([np.cos(a), -np.sin(a)], axis=0)
    return np.asarray(f, np.float32), np.asarray(f.T[:n // 2], np.float32)


def _hyena_ctx_kernel(v_ref, x1_ref, x2_ref, kc_ref, ks_ref, bias_ref, ff_ref, fi_ref, o_ref):
    n = ff_ref.shape[1]
    lc = n // 2
    C = v_ref.shape[2]
    ff = ff_ref[...].astype(BF16)
    inv = 1.0 / (jnp.sum(ks_ref[...], axis=1, keepdims=True) * float(n))
    kn = (kc_ref[...] * inv).T
    kspec = jnp.dot(ff, kn.astype(BF16), preferred_element_type=F32)
    z = jnp.concatenate([v_ref[0], v_ref[1]], axis=1)
    for o, g_ref in enumerate((x1_ref, x2_ref)):
        x = jnp.dot(ff[:, :lc], z.astype(BF16), preferred_element_type=F32)
        k = kspec[:, o * C:(o + 1) * C]
        kr = jnp.concatenate([k[:n], k[:n]], axis=1)
        ki = jnp.concatenate([k[n:], k[n:]], axis=1)
        yr, yi = _cmul(x[:n], x[n:], kr, ki)
        y = jnp.concatenate([yr, yi], axis=0)
        conv = jnp.dot(fi_ref[...].astype(BF16), y.astype(BF16), preferred_element_type=F32)
        bias = jnp.concatenate([bias_ref[o:o + 1, :], bias_ref[o:o + 1, :]], axis=1)
        z = jnp.concatenate([g_ref[0], g_ref[1]], axis=1) * (conv + bias * z)
    o_ref[0] = z[:, :C]
    o_ref[1] = z[:, C:]


def _hyena_ctx(v, x1, x2, filt, hy_bias):
    B, Lc, C = v.shape
    assert B == 2
    kc, ksum = _hy_filter(Lc, Lc, False, *filt)
    ff, fi = (jnp.asarray(m, F32) for m in _ctx_dft_consts(2 * Lc))
    return pl.pallas_call(
        _hyena_ctx_kernel,
        out_shape=jax.ShapeDtypeStruct((B, Lc, C), F32),
        compiler_params=pltpu.CompilerParams(vmem_limit_bytes=VMEM_LIMIT),
        name="hyena_ctx",
    )(v, x1, x2, kc, ksum, hy_bias, ff, fi)


NA_ROWS_PER_STEP = 8
NA_SOFTMAX_ROWS = 32
_NT = (((1,), (1,)), ((), ()))


def _na_bias_table(rpb):
    qc = np.arange(GRID_W)[:, None]
    kc = np.arange(GRID_W)[None, :]
    start = np.clip(qc - NA_WIN_COLS // 2, 0, GRID_W - NA_WIN_COLS)
    valid = (kc >= start) & (kc < start + NA_WIN_COLS)
    pad = jnp.pad(rpb, ((0, 0), (0, 0), (GRID_W, GRID_W)))
    shift = GRID_W + NA_WIN_COLS - 1
    toep = jnp.stack([pad[:, :, shift - c:shift - c + GRID_W] for c in range(GRID_W)], axis=2)
    full = jnp.where(jnp.asarray(valid)[None, None], toep, NEG_INF)
    t = jnp.stack([full[:, d:d + NA_WIN_ROWS] for d in range(NA_WIN_ROWS)], axis=1)
    t = t.reshape(NA_HEADS // 2, 2, NA_WIN_ROWS, NA_WIN_ROWS, GRID_W, GRID_W).transpose(0, 2, 1, 4, 3, 5)
    return t.reshape(NA_HEADS // 2, NA_WIN_ROWS, 2 * GRID_W, NA_WIN_ROWS * GRID_W).astype(F32)


def _na_kernel(q_ref, kp_ref, kc_ref, kn_ref, vp_ref, vc_ref, vn_ref, ck_ref, cv_ref, bias_ref, o_ref,
               wk_ref, ws_ref, wv_ref, sc_ref, pc_ref, ol_ref, li_ref, sl_ref, pl_ref):
    i = pl.program_id(2)
    last = pl.num_programs(2) - 1
    blk = NA_ROWS_PER_STEP * GRID_W
    for n, (kr, vr) in enumerate(((kp_ref, vp_ref), (kc_ref, vc_ref), (kn_ref, vn_ref))):
        wk_ref[:, n * blk:(n + 1) * blk] = kr[0]
        wv_ref[n * blk:(n + 1) * blk] = vr[0]
    ws_ref[:, 0:3 * blk - GRID_W] = wk_ref[:, GRID_W:3 * blk]

    def key_window(off):
        if off % 2 == 0:
            return wk_ref[:, off * GRID_W:(off + NA_WIN_ROWS) * GRID_W]
        return ws_ref[:, (off - 1) * GRID_W:(off - 1 + NA_WIN_ROWS) * GRID_W]
    first_head = lax.broadcasted_iota(jnp.int32, (GRID_W, 2 * NA_HEAD_DIM), 1) < NA_HEAD_DIM
    q = q_ref[0]
    zero = jnp.zeros((GRID_W, 2 * NA_HEAD_DIM), q.dtype)
    pieces = []
    for j in range(NA_ROWS_PER_STEP):
        qj = q[j * GRID_W:(j + 1) * GRID_W]
        pieces += [jnp.where(first_head, qj, zero), jnp.where(first_head, zero, qj)]
    qs = jnp.concatenate(pieces, axis=0)
    sc_ref[...] = lax.dot_general(qs, ck_ref[0], _NT, preferred_element_type=F32)
    half = NA_WIN_ROWS // 2
    rows = 2 * GRID_W

    def window_row(j):
        return jnp.where(i == 0, max(j + half, NA_ROWS_PER_STEP),
                         jnp.where(i == last, min(j + half, NA_ROWS_PER_STEP), j + half))

    for j in range(NA_ROWS_PER_STEP):
        kw = key_window(j + half)
        if j < half:
            kw = jnp.where(i == 0, key_window(NA_ROWS_PER_STEP), kw)
        elif j > half:
            kw = jnp.where(i == last, key_window(NA_ROWS_PER_STEP), kw)
        sl_ref[j] = jnp.dot(qs[j * rows:(j + 1) * rows], kw, preferred_element_type=F32)
    for j in range(NA_ROWS_PER_STEP):
        d = window_row(j) - j - 1
        for c0 in range(0, rows, NA_SOFTMAX_ROWS):
            rs = slice(c0, c0 + NA_SOFTMAX_ROWS)
            ra = slice(j * rows + c0, j * rows + c0 + NA_SOFTMAX_ROWS)
            s_loc = sl_ref[j, rs, :] + bias_ref[0, d, rs, :]
            s_ctx = sc_ref[ra, :]
            m = jnp.maximum(jnp.max(s_loc, axis=1, keepdims=True), jnp.max(s_ctx, axis=1, keepdims=True))
            p_loc = jnp.exp(s_loc - m)
            p_ctx = jnp.exp(s_ctx - m)
            l = jnp.sum(p_loc, axis=1, keepdims=True) + jnp.sum(p_ctx, axis=1, keepdims=True)
            pl_ref[j, rs, :] = p_loc.astype(BF16)
            pc_ref[ra, :] = p_ctx.astype(pc_ref.dtype)
            li_ref[ra, :] = jnp.broadcast_to(1.0 / l, (NA_SOFTMAX_ROWS, 2 * NA_HEAD_DIM))
    for j in range(NA_ROWS_PER_STEP):
        start = pl.multiple_of(window_row(j) * GRID_W, GRID_W)
        vw = wv_ref[pl.ds(start, NA_WIN_ROWS * GRID_W), :]
        ol_ref[j * rows:(j + 1) * rows, :] = jnp.dot(pl_ref[j], vw, preferred_element_type=F32)
    o = (ol_ref[...] + jnp.dot(pc_ref[...], cv_ref[0], preferred_element_type=F32)) * li_ref[...]
    for j in range(NA_ROWS_PER_STEP):
        r0 = j * rows
        oj = jnp.where(first_head, o[r0:r0 + GRID_W], o[r0 + GRID_W:r0 + rows])
        o_ref[0, j * GRID_W:(j + 1) * GRID_W, :] = oj.astype(o_ref.dtype)


def _na(q, k, v, ck, cv, bias):
    B, L, W = q.shape
    Lc = ck.shape[1]
    blk = NA_ROWS_PER_STEP * GRID_W
    nblk = L // blk
    assert NA_ROWS_PER_STEP == NA_WIN_ROWS and nblk >= 2 and GRID_W * 2 == LANES
    pw = 2 * NA_HEAD_DIM
    cur = pl.BlockSpec((1, blk, pw), lambda b, h, i: (b, i, h))
    prev = pl.BlockSpec((1, blk, pw), lambda b, h, i: (b, jnp.maximum(i - 1, 0), h))
    nxt = pl.BlockSpec((1, blk, pw), lambda b, h, i: (b, jnp.minimum(i + 1, nblk - 1), h))
    kcur = pl.BlockSpec((1, pw, blk), lambda b, h, i: (b, h, i))
    kprev = pl.BlockSpec((1, pw, blk), lambda b, h, i: (b, h, jnp.maximum(i - 1, 0)))
    knxt = pl.BlockSpec((1, pw, blk), lambda b, h, i: (b, h, jnp.minimum(i + 1, nblk - 1)))
    cspec = pl.BlockSpec((1, Lc, pw), lambda b, h, i: (b, 0, h))
    bspec = pl.BlockSpec((1,) + bias.shape[1:], lambda b, h, i: (h, 0, 0, 0))
    stacked = 2 * blk
    return pl.pallas_call(
        _na_kernel,
        out_shape=jax.ShapeDtypeStruct((B, L, W), BF16),
        grid=(B, W // pw, nblk),
        in_specs=[cur, kprev, kcur, knxt, prev, cur, nxt, cspec, cspec, bspec],
        out_specs=cur,
        scratch_shapes=[pltpu.VMEM((pw, 3 * blk), BF16), pltpu.VMEM((pw, 3 * blk), BF16),
                        pltpu.VMEM((3 * blk, pw), BF16),
                        pltpu.VMEM((stacked, Lc), F32), pltpu.VMEM((stacked, Lc), BF16),
                        pltpu.VMEM((stacked, pw), F32), pltpu.VMEM((stacked, pw), F32),
                        pltpu.VMEM((NA_ROWS_PER_STEP, 2 * GRID_W, NA_WIN_ROWS * GRID_W), F32),
                        pltpu.VMEM((NA_ROWS_PER_STEP, 2 * GRID_W, NA_WIN_ROWS * GRID_W), BF16)],
        compiler_params=pltpu.CompilerParams(vmem_limit_bytes=VMEM_LIMIT),
        name="nattn",
    )(q, k, k, k, v, v, v, ck, cv, bias)


def _ctx_attn_kernel(q_ref, k_ref, v_ref, o_ref):
    q = q_ref[0]
    k = k_ref[0]
    v = v_ref[0]
    first_head = lax.broadcasted_iota(jnp.int32, q.shape, 1) < NA_HEAD_DIM
    outs = []
    for h in range(2):
        qm = jnp.where(first_head if h == 0 else jnp.logical_not(first_head), q, jnp.zeros_like(q))
        s = lax.dot_general(qm, k, _NT, preferred_element_type=F32)
        p = jnp.exp(s - jnp.max(s, axis=1, keepdims=True))
        o = jnp.dot(p.astype(BF16), v, preferred_element_type=F32)
        outs.append(o / jnp.sum(p, axis=1, keepdims=True))
    o_ref[0] = jnp.where(first_head, outs[0], outs[1]).astype(o_ref.dtype)


def _ctx_attn(q, k, v):
    B, Lc, W = q.shape
    pw = 2 * NA_HEAD_DIM
    spec = pl.BlockSpec((1, Lc, pw), lambda b, h: (b, 0, h))
    return pl.pallas_call(
        _ctx_attn_kernel,
        out_shape=jax.ShapeDtypeStruct((B, Lc, W), BF16),
        grid=(B, W // pw),
        in_specs=[spec, spec, spec],
        out_specs=spec,
        name="ctx_attn",
    )(q, k, v)


LRU_CHUNK = 512


def _lru_gate_weights(wa, ba, wi, bi, lam):
    def bd(w):
        return jax.scipy.linalg.block_diag(*[w[n] for n in range(LRU_BLOCKS)])
    wg = jnp.stack([jnp.concatenate([bd(wa[d]), bd(wi[d])], axis=1) for d in range(2)]).astype(BF16)
    bg = jnp.stack([jnp.concatenate([ba[d], bi[d]])[None, :] for d in range(2)])
    return wg, bg, lam[:, None, :]


def _lru_coeffs(u, wg, bg, lam):
    C = u.shape[1]
    g = jnp.dot(u.astype(BF16), wg, preferred_element_type=F32) + bg
    sig = 0.5 + 0.5 * jnp.tanh(0.5 * g)
    r, ig = sig[:, :C], sig[:, C:]
    nl = -lam
    softplus = jnp.maximum(nl, 0.0) + jnp.log(1.0 + jnp.exp(-jnp.abs(nl)))
    log_a = (-LRU_C * softplus) * r
    a = jnp.exp(log_a)
    t = jnp.tanh(log_a)
    b = jnp.sqrt(-2.0 * t / (1.0 - t)) * (ig * u)
    return a, b


def _lru_scan(a, b, h0, reverse, ac_ref, bc_ref, h_ref):
    T, C = a.shape
    row = lax.broadcasted_iota(jnp.int32, a.shape, 0) % SUBLANES
    for s in (1, 2, 4):
        shift = T - s if reverse else s
        keep = (row < SUBLANES - s) if reverse else (row >= s)
        b = jnp.where(keep, a * pltpu.roll(b, shift, 0) + b, b)
        a = jnp.where(keep, a * pltpu.roll(a, shift, 0), a)
    ac_ref[...] = a
    bc_ref[...] = b
    ng = T // SUBLANES

    def group(g, h):
        r0 = pl.multiple_of((ng - 1 - g if reverse else g) * SUBLANES, SUBLANES)
        hr = ac_ref[pl.ds(r0, SUBLANES), :] * h + bc_ref[pl.ds(r0, SUBLANES), :]
        h_ref[pl.ds(r0, SUBLANES), :] = hr
        edge = hr[0:1] if reverse else hr[SUBLANES - 1:SUBLANES]
        return jnp.broadcast_to(edge, (SUBLANES, C))

    return lax.fori_loop(0, ng, group, h0, unroll=4)


def _gelu_tanh(x):
    return 0.5 * x * (1.0 + jnp.tanh(math.sqrt(2.0 / math.pi) * (x + 0.044715 * (x * x * x))))


def _lru_ctx_kernel(u_ref, xg_ref, wg_ref, bg_ref, lam_ref, hend_ref, yc_ref, ac_ref, bc_ref, h_ref):
    u = u_ref[0]
    C = u.shape[1]
    total = jnp.zeros_like(u)
    for d, rev in enumerate((False, True)):
        a, b = _lru_coeffs(u, wg_ref[d], bg_ref[d], lam_ref[d])
        hl = _lru_scan(a, b, jnp.zeros((SUBLANES, C), F32), rev, ac_ref, bc_ref, h_ref)
        hend_ref[0, d:d + 1, :] = hl[0:1]
        total = total + h_ref[...]
    yc_ref[0] = (total * _gelu_tanh(xg_ref[0])).astype(yc_ref.dtype)


def _lru_ctx(u, xg, wg, bg, lam):
    B, Lc, C = u.shape
    tok = pl.BlockSpec((1, Lc, C), lambda b: (b, 0, 0))
    full = lambda a: pl.BlockSpec(a.shape, lambda b: (0,) * a.ndim)
    return pl.pallas_call(
        _lru_ctx_kernel,
        out_shape=[jax.ShapeDtypeStruct((B, 2, C), F32), jax.ShapeDtypeStruct((B, Lc, C), BF16)],
        grid=(B,),
        in_specs=[tok, tok, full(wg), full(bg), full(lam)],
        out_specs=[pl.BlockSpec((1, 2, C), lambda b: (b, 0, 0)), tok],
        scratch_shapes=[pltpu.VMEM((Lc, C), F32)] * 3,
        name="lru_ctx",
    )(u, xg, wg, bg, lam)


def _lru_dir_kernel(*refs, d, reverse):
    if reverse:
        u_ref, hend_ref, wg_ref, bg_ref, lam_ref, hf_ref, xg_ref, o_ref, ac_ref, bc_ref, h_ref, carry_ref = refs
    else:
        u_ref, hend_ref, wg_ref, bg_ref, lam_ref, o_ref, ac_ref, bc_ref, carry_ref = refs
        h_ref = o_ref.at[0]
    C = u_ref.shape[2]

    @pl.when(pl.program_id(1) == 0)
    def _():
        carry_ref[...] = jnp.broadcast_to(hend_ref[0, d:d + 1, :], (SUBLANES, C))

    a, b = _lru_coeffs(u_ref[0], wg_ref[d], bg_ref[d], lam_ref[d])
    carry_ref[...] = _lru_scan(a, b, carry_ref[...], reverse, ac_ref, bc_ref, h_ref)
    if reverse:
        o_ref[0] = ((hf_ref[0] + h_ref[...]) * _gelu_tanh(xg_ref[0])).astype(o_ref.dtype)


def _lru_dir(u, hend, wg, bg, lam, hf=None, xg=None):
    B, L, C = u.shape
    reverse = hf is not None
    T = LRU_CHUNK
    nb = L // T
    tok = pl.BlockSpec((1, T, C), (lambda b, i: (b, nb - 1 - i, 0)) if reverse else (lambda b, i: (b, i, 0)))
    full = lambda a: pl.BlockSpec(a.shape, lambda b, i: (0,) * a.ndim)
    ins = [u, hend, wg, bg, lam] + ([hf, xg] if reverse else [])
    specs = [tok, pl.BlockSpec((1, 2, C), lambda b, i: (b, 0, 0)), full(wg), full(bg), full(lam)]
    specs += [tok, tok] if reverse else []
    scratch = [pltpu.VMEM((T, C), F32)] * (3 if reverse else 2) + [pltpu.VMEM((SUBLANES, C), F32)]
    return pl.pallas_call(
        functools.partial(_lru_dir_kernel, d=int(reverse), reverse=reverse),
        out_shape=jax.ShapeDtypeStruct((B, L, C), BF16 if reverse else F32),
        grid=(B, nb),
        in_specs=specs,
        out_specs=tok,
        scratch_shapes=scratch,
        compiler_params=pltpu.CompilerParams(dimension_semantics=("arbitrary", "arbitrary")),
        name="lru_bwd" if reverse else "lru_fwd",
    )(*ins)


def _lru(u, xg, u_c, xg_c, wa, ba, wi, bi, lam):
    wg, bg, lam3 = _lru_gate_weights(wa, ba, wi, bi, lam)
    hend, yc = _lru_ctx(u_c, xg_c, wg, bg, lam3)
    hf = _lru_dir(u, hend, wg, bg, lam3)
    return _lru_dir(u, hend, wg, bg, lam3, hf, xg), yc


def kernel(x, c, ctx, c_ctx, ada_w, ada_b, g_mix_pre, g_mix_post, g_ffn_pre, g_ffn_post, w_in, w_out, hy_conv_w,
           hy_conv_b, hy_f_w1, hy_f_b1, hy_f_w2, hy_f_b2, hy_f_w3, hy_f_freq, hy_bias, na_rpb, lru_conv_w,
           lru_conv_b, lru_wa, lru_ba, lru_wi, lru_bi, lru_lam, ffn_w_gu, ffn_w_down):
    B, L, D = x.shape
    Lc = ctx.shape[1]
    tm = 512

    assert B + 1 <= SUBLANES
    cond_t = jnp.zeros((D, SUBLANES), F32).at[:, 0:B].set(c.T).at[:, B].set(c_ctx)
    mods = _modulation(cond_t, B + 1, ada_w, ada_b)

    xc = ctx
    for l in range(DEPTH):
        with_ctx_out = l < DEPTH - 1
        m = mods[l].reshape(8, 6, D)
        lat = [m[0:B, j][:, None, :] for j in range(6)]
        cx = [jnp.broadcast_to(m[B, j][None, None, :], (B, 1, D)) for j in range(6)]
        row = lambda a: a.reshape(1, -1)

        w_in_bf = w_in[l].astype(BF16)
        w_out_bf = w_out[l].astype(BF16)
        wgu = ffn_w_gu[l].astype(BF16)
        wd = ffn_w_down[l].astype(BF16)

        conv_args = (hy_conv_w[l], row(hy_conv_b[l]), lru_conv_w[l], row(lru_conv_b[l]))
        hv, hx1, hx2, q, k, v, lu, lg = _inproj(x, lat[1], lat[0], row(g_mix_pre[l]), w_in_bf, *conv_args, tm=tm,
                                                chan_major=True)
        cv, cx1, cx2, cq, ck, cvv, clu, clg = _inproj(xc, cx[1], cx[0], row(g_mix_pre[l]), w_in_bf, *conv_args,
                                                      tm=Lc, chan_major=False)

        filt = (hy_f_w1[l], hy_f_b1[l], hy_f_w2[l], hy_f_b2[l], hy_f_w3[l], hy_f_freq[l])
        y_hy = _hyena(hv, hx1, hx2, filt, hy_bias[l])
        y_na = _na(q, k, v, ck, cvv, _na_bias_table(na_rpb[l]))
        y_lru, yc_lru = _lru(lu, lg, clu, clg, lru_wa[l], lru_ba[l], lru_wi[l], lru_bi[l], lru_lam[l])

        ffn_args = (row(g_ffn_pre[l]), row(g_ffn_post[l]), wgu, wd)
        x = _out_ffn(y_hy, y_na, y_lru, w_out_bf, x, row(g_mix_post[l]), lat[2], lat[4], lat[3], lat[5], *ffn_args,
                     tm=2 * tm, chan_major=True)

        if with_ctx_out:
            yc_hy = _hyena_ctx(cv, cx1, cx2, filt, hy_bias[l])
            yc_na = _ctx_attn(cq, ck, cvv)
            xc = _out_ffn(yc_hy, yc_na, yc_lru, w_out_bf, xc, row(g_mix_post[l]), cx[2], cx[4], cx[3], cx[5],
                          *ffn_args, tm=Lc, chan_major=False)
    return x
```

```python
import functools
import math

import jax
import jax.numpy as jnp
import numpy as np
from jax import lax
from jax.experimental import pallas as pl
from jax.experimental.pallas import tpu as pltpu

F32 = jnp.float32
BF16 = jnp.bfloat16

D_MODEL = 1024
DEPTH = 2
GRID_W = 64
HY_WIDTH = D_MODEL // 4
NA_HEAD_DIM = 64
NA_WIDTH = D_MODEL // 2
NA_HEADS = NA_WIDTH // NA_HEAD_DIM
LRU_WIDTH = D_MODEL // 4
LRU_BLOCKS = 4
IN_WIDTH = 3 * HY_WIDTH + 3 * NA_WIDTH + 2 * LRU_WIDTH
HY_ORDER = 2
HY_BANDS = 16
HY_FAST_DECAY = 0.3
HY_SLOW_DECAY = 1.5
HY_DECAY_TARGET = 1e-2
NA_WIN_ROWS = 8
NA_WIN_COLS = 16
LRU_C = 8.0
D_FF = -(-8 * D_MODEL // (3 * 256)) * 256
RMS_EPS = 1e-6
NEG_INF = -1e30
LOG2E = math.log2(math.e)

_HY_END = 3 * HY_WIDTH
_NA_END = _HY_END + 3 * NA_WIDTH
_LRU_MID = _NA_END + LRU_WIDTH

HALO = 8
SUBLANES = 8
VMEM_LIMIT = 56 * 1024 * 1024


def _rms(x, g):
    return x * lax.rsqrt(jnp.mean(x * x, axis=-1, keepdims=True) + RMS_EPS) * g


def _mod_kernel(ct_ref, w_ref, b_ref, o_ref, *, n_cond):
    ct = ct_ref[...]
    st = ct * jax.nn.sigmoid(ct)
    w = w_ref[0]
    rows = [jnp.sum(w * st[:, r:r + 1], axis=0, keepdims=True) for r in range(n_cond)]
    rows.append(jnp.zeros((SUBLANES - n_cond, w.shape[1]), F32))
    o_ref[0] = jnp.concatenate(rows, axis=0) + b_ref[0]


def _modulation(cond_t, n_cond, ada_w, ada_b):
    tn = 768
    n = ada_w.shape[-1]
    return pl.pallas_call(
        functools.partial(_mod_kernel, n_cond=n_cond),
        out_shape=jax.ShapeDtypeStruct((DEPTH, SUBLANES, n), F32),
        grid=(DEPTH, n // tn),
        in_specs=[pl.BlockSpec((D_MODEL, SUBLANES), lambda l, j: (0, 0)),
                  pl.BlockSpec((1, D_MODEL, tn), lambda l, j: (l, 0, j)),
                  pl.BlockSpec((1, 1, tn), lambda l, j: (l, 0, j))],
        out_specs=pl.BlockSpec((1, SUBLANES, tn), lambda l, j: (l, 0, j)),
        compiler_params=pltpu.CompilerParams(vmem_limit_bytes=VMEM_LIMIT),
        name="adaln_modulation",
    )(cond_t, ada_w, ada_b.reshape(DEPTH, 1, n))


CM_LANES = 256
LANES = 128
CM_HALVES = CM_LANES // LANES


def _chan_major_shape(B, L, C):
    return (B, C // SUBLANES, CM_HALVES, L // CM_LANES * SUBLANES, LANES)


def _chan_major_spec(tm, C):
    return pl.BlockSpec((1, C // SUBLANES, CM_HALVES, tm // CM_LANES * SUBLANES, LANES),
                        lambda b, i: (b, 0, 0, i, 0))


def _store_chan_major(ref, u):
    ut = u.T
    for g in range(u.shape[1] // SUBLANES):
        for j in range(u.shape[0] // CM_LANES):
            for h in range(CM_HALVES):
                lane0 = j * CM_LANES + h * LANES
                ref[0, g, h, j * SUBLANES:(j + 1) * SUBLANES, :] = ut[g * SUBLANES:(g + 1) * SUBLANES,
                                                                      lane0:lane0 + LANES]


def _load_chan_major(ref):
    _, ng, _, nr, _ = ref.shape
    rows = [jnp.concatenate([ref[0, g, h, j * SUBLANES:(j + 1) * SUBLANES, :]
                             for j in range(nr // SUBLANES) for h in range(CM_HALVES)], axis=1) for g in range(ng)]
    return jnp.concatenate(rows, axis=0).T


def _chan_rows(ref, lead, c, n):
    return jnp.concatenate([ref.at[lead + (h,)][pl.ds(c, n, stride=SUBLANES), :] for h in range(CM_HALVES)], axis=1)


def _inproj_kernel(xp_ref, xc_ref, xn_ref, sc_ref, sh_ref, g_ref, w_ref, hcw_ref, hcb_ref, lcw_ref, lcb_ref,
                   hv_ref, hx1_ref, hx2_ref, q_ref, k_ref, v_ref, lu_ref, lg_ref, pe_ref, *, tm, chan_major):
    i = pl.program_id(1)
    last = pl.num_programs(1) - 1
    g = g_ref[...]
    sc1 = 1.0 + sc_ref[0]
    sh = sh_ref[0]

    def norm_mod(xv):
        return _rms(xv, g) * sc1 + sh

    hp = norm_mod(xp_ref[0]) * (i > 0).astype(F32)
    hn = norm_mod(xn_ref[0]) * (i < last).astype(F32)
    he = jnp.concatenate([hp, norm_mod(xc_ref[0]), hn], axis=0).astype(BF16)

    pe_ref[:, 0:_HY_END] = jnp.dot(he, w_ref[:, 0:_HY_END], preferred_element_type=F32)
    pe_ref[:, _HY_END:] = jnp.dot(he, w_ref[:, _NA_END:_LRU_MID], preferred_element_type=F32)
    hc = he[HALO:HALO + tm]
    qkv = jnp.dot(hc, w_ref[:, _HY_END:_NA_END], preferred_element_type=F32)
    q_ref[0] = (qkv[:, 0:NA_WIDTH] * (NA_HEAD_DIM ** -0.5 * LOG2E)).astype(BF16)
    kk = qkv[:, NA_WIDTH:2 * NA_WIDTH]
    k_ref[0] = (kk.T if chan_major else kk).astype(BF16)
    v_ref[0] = qkv[:, 2 * NA_WIDTH:].astype(BF16)
    lg_ref[0] = jnp.dot(hc, w_ref[:, _LRU_MID:], preferred_element_type=F32)

    u = hcb_ref[...]
    for kk in range(3):
        u = u + hcw_ref[kk:kk + 1, :] * pe_ref[pl.ds(HALO - 1 + kk, tm), 0:_HY_END]
    for n, ref in enumerate((hv_ref, hx1_ref, hx2_ref)):
        un = u[:, n * HY_WIDTH:(n + 1) * HY_WIDTH]
        if chan_major:
            _store_chan_major(ref, un)
        else:
            ref[0] = un
    ul = lcb_ref[...]
    for kk in range(4):
        ul = ul + lcw_ref[kk:kk + 1, :] * pe_ref[pl.ds(HALO - 2 + kk, tm), _HY_END:]
    lu_ref[0] = ul


def _inproj(x, sc, sh, g, w_bf, hcw, hcb, lcw, lcb, *, tm, chan_major):
    B, L, D = x.shape
    nb = tm // HALO
    nh = L // HALO
    tok = lambda w, dt: jax.ShapeDtypeStruct((B, L, w), dt)
    tspec = lambda w: pl.BlockSpec((1, tm, w), lambda b, i: (b, i, 0))
    full = lambda a: pl.BlockSpec(a.shape, lambda b, i: (0,) * a.ndim)
    vec = pl.BlockSpec((1, 1, D), lambda b, i: (b, 0, 0))
    if chan_major:
        hy_shape = jax.ShapeDtypeStruct(_chan_major_shape(B, L, HY_WIDTH), F32)
        hy_spec = _chan_major_spec(tm, HY_WIDTH)
        k_shape = jax.ShapeDtypeStruct((B, NA_WIDTH, L), BF16)
        k_spec = pl.BlockSpec((1, NA_WIDTH, tm), lambda b, i: (b, 0, i))
    else:
        hy_shape, hy_spec = tok(HY_WIDTH, F32), tspec(HY_WIDTH)
        k_shape, k_spec = tok(NA_WIDTH, BF16), tspec(NA_WIDTH)
    return pl.pallas_call(
        functools.partial(_inproj_kernel, tm=tm, chan_major=chan_major),
        out_shape=[hy_shape] * 3 + [tok(NA_WIDTH, BF16), k_shape, tok(NA_WIDTH, BF16)] + [tok(LRU_WIDTH, F32)] * 2,
        grid=(B, L // tm),
        in_specs=[pl.BlockSpec((1, HALO, D), lambda b, i: (b, jnp.maximum(i * nb - 1, 0), 0)),
                  pl.BlockSpec((1, tm, D), lambda b, i: (b, i, 0)),
                  pl.BlockSpec((1, HALO, D), lambda b, i: (b, jnp.minimum((i + 1) * nb, nh - 1), 0)),
                  vec, vec, full(g), pl.BlockSpec(w_bf.shape, lambda b, i: (0, 0), pipeline_mode=pl.Buffered(1)),
                  full(hcw), full(hcb), full(lcw), full(lcb)],
        out_specs=[hy_spec] * 3 + [tspec(NA_WIDTH), k_spec, tspec(NA_WIDTH)] + [tspec(LRU_WIDTH)] * 2,
        scratch_shapes=[pltpu.VMEM((tm + 2 * HALO, _HY_END + LRU_WIDTH), F32)],
        compiler_params=pltpu.CompilerParams(vmem_limit_bytes=VMEM_LIMIT),
        name="inproj",
    )(x, x, x, sc, sh, g, w_bf, hcw, hcb, lcw, lcb)


FF_CHUNK = 256


def _out_ffn_kernel(yh_ref, yn_ref, yl_ref, wo_ref, x_ref, gm_ref, gtm_ref, sc_ref, sh_ref, gt_ref, gpre_ref,
                    gpost_ref, wgu_ref, wd_ref, o_ref, *, chan_major):
    yh = _load_chan_major(yh_ref) if chan_major else yh_ref[0]
    y = jnp.dot(yh.astype(BF16), wo_ref[0:HY_WIDTH], preferred_element_type=F32)
    y = y + jnp.dot(yn_ref[0], wo_ref[HY_WIDTH:HY_WIDTH + NA_WIDTH], preferred_element_type=F32)
    y = y + jnp.dot(yl_ref[0], wo_ref[HY_WIDTH + NA_WIDTH:], preferred_element_type=F32)
    x = x_ref[0] + gtm_ref[0] * _rms(y, gm_ref[...])
    h = (_rms(x, gpre_ref[...]) * (1.0 + sc_ref[0]) + sh_ref[0]).astype(BF16)
    acc = jnp.zeros(x.shape, F32)
    for c0 in range(0, D_FF, FF_CHUNK):
        g = jnp.dot(h, wgu_ref[:, c0:c0 + FF_CHUNK], preferred_element_type=F32)
        u = jnp.dot(h, wgu_ref[:, D_FF + c0:D_FF + c0 + FF_CHUNK], preferred_element_type=F32)
        a = (g * jax.nn.sigmoid(g) * u).astype(BF16)
        acc = acc + jnp.dot(a, wd_ref[c0:c0 + FF_CHUNK, :], preferred_element_type=F32)
    o_ref[0] = x + gt_ref[0] * _rms(acc, gpost_ref[...])


def _out_ffn(yh, yn, yl, wo, x, gm, gtm, sc, sh, gt, gpre, gpost, wgu, wd, *, tm, chan_major):
    B, L, D = x.shape
    tspec = lambda w: pl.BlockSpec((1, tm, w), lambda b, i: (b, i, 0))
    vec = pl.BlockSpec((1, 1, D), lambda b, i: (b, 0, 0))
    full = lambda a: pl.BlockSpec(a.shape, lambda b, i: (0,) * a.ndim)
    res = lambda a: pl.BlockSpec(a.shape, lambda b, i: (0,) * a.ndim, pipeline_mode=pl.Buffered(1))
    hy_spec = _chan_major_spec(tm, HY_WIDTH) if chan_major else tspec(HY_WIDTH)
    return pl.pallas_call(
        functools.partial(_out_ffn_kernel, chan_major=chan_major),
        out_shape=jax.ShapeDtypeStruct((B, L, D), F32),
        grid=(B, L // tm),
        in_specs=[hy_spec, tspec(NA_WIDTH), tspec(LRU_WIDTH), res(wo), tspec(D), full(gm), vec,
                  vec, vec, vec, full(gpre), full(gpost), res(wgu), res(wd)],
        out_specs=tspec(D),
        compiler_params=pltpu.CompilerParams(vmem_limit_bytes=VMEM_LIMIT),
        name="out_ffn",
    )(yh, yn, yl, wo, x, gm, gtm, sc, sh, gt, gpre, gpost, wgu, wd)


HY_N1 = CM_LANES
HY_N2 = 128
HY_FILT_COLS = 2048


def _block_cplx(re, im):
    return np.block([[re, -im], [im, re]])


@functools.lru_cache(maxsize=None)
def _dft_consts():
    n = HY_N1 * HY_N2
    h = HY_N2 // 2
    i1 = np.arange(HY_N1, dtype=np.float64)
    i2 = np.arange(HY_N2, dtype=np.float64)
    a1 = 2.0 * np.pi * np.outer(i1, i1) / HY_N1
    a2 = 2.0 * np.pi * np.outer(i2, i2) / HY_N2
    at = 2.0 * np.pi * np.outer(i2, i1) / n
    r2, m2 = np.cos(a2), -np.sin(a2)
    r1, m1 = np.cos(a1), -np.sin(a1)
    f32 = lambda m: np.asarray(m, np.float32)
    return dict(
        rows_data=f32(_block_cplx(r2[:, :h], m2[:, :h])),
        rows_filt=f32(np.concatenate([r2, m2], axis=0)),
        rows_out=f32(_block_cplx(r2[:h, :], m2[:h, :])),
        lanes=f32(np.block([[r1, m1], [-m1, r1]])),
        twr=f32(np.cos(at)), twi=f32(-np.sin(at)))


def _dft_const(name):
    return jnp.asarray(_dft_consts()[name], F32)


def _cmul(ar, ai, br, bi):
    return ar * br - ai * bi, ar * bi + ai * br


def _hy_conv_kernel(v_ref, x1_ref, x2_ref, k0_ref, k1_ref, ks0_ref, ks1_ref, bias_ref, rd_ref, rf_ref, ro_ref,
                    ln_ref, twr_ref, twi_ref, o_ref, st_ref, ks_ref):
    g = pl.program_id(0)
    n_ch = SUBLANES
    h = HY_N2 // 2
    rows_data = rd_ref[...].astype(BF16)
    rows_filt = rf_ref[...].astype(BF16)
    rows_out = ro_ref[...].astype(BF16)
    lanes = ln_ref[...].astype(BF16)
    twr, twi = twr_ref[...], twi_ref[...]

    def rows_then_twiddle(mat, x, c):
        a = jnp.dot(mat, x.astype(BF16), preferred_element_type=F32)
        ar, ai = _cmul(a[:HY_N2], a[HY_N2:], twr, twi)
        st_ref[c * HY_N2:(c + 1) * HY_N2, 0:HY_N1] = ar.astype(BF16)
        st_ref[c * HY_N2:(c + 1) * HY_N2, HY_N1:] = ai.astype(BF16)

    for o, (k_ref, s_ref) in enumerate(((k0_ref, ks0_ref), (k1_ref, ks1_ref))):
        for c in range(n_ch):
            rows_then_twiddle(rows_filt, _chan_rows(k_ref, (0,), c, HY_N2), c)
        spec = jnp.dot(st_ref[...], lanes, preferred_element_type=F32)
        inv = 1.0 / (jnp.sum(s_ref[...], axis=1, keepdims=True) * float(HY_N1 * HY_N2))
        for c in range(n_ch):
            ks_ref[o, c * HY_N2:(c + 1) * HY_N2, :] = spec[c * HY_N2:(c + 1) * HY_N2] * inv[c:c + 1, :]

    z = [[_chan_rows(v_ref, (b, 0), c, h) for c in range(n_ch)] for b in range(2)]
    for o, gate_ref in enumerate((x1_ref, x2_ref)):
        for c in range(n_ch):
            rows_then_twiddle(rows_data, jnp.concatenate([z[0][c], z[1][c]], axis=0), c)
        x = jnp.dot(st_ref[...], lanes, preferred_element_type=F32)
        pr, pi = _cmul(x[:, :HY_N1], x[:, HY_N1:], ks_ref[o, :, 0:HY_N1], ks_ref[o, :, HY_N1:])
        st_ref[:, 0:HY_N1] = pr.astype(BF16)
        st_ref[:, HY_N1:] = (-pi).astype(BF16)
        y = jnp.dot(st_ref[...], lanes, preferred_element_type=F32)
        for c in range(n_ch):
            yc = y[c * HY_N2:(c + 1) * HY_N2]
            yr, yi = _cmul(yc[:, :HY_N1], yc[:, HY_N1:], twr, twi)
            out = jnp.dot(rows_out, jnp.concatenate([yr, yi], axis=0).astype(BF16),
                          preferred_element_type=F32)
            bias = bias_ref[o, g * n_ch + c]
            for b, conv in ((0, out[:h]), (1, -out[h:])):
                z[b][c] = _chan_rows(gate_ref, (b, 0), c, h) * (conv + bias * z[b][c])
    for b in range(2):
        for c in range(n_ch):
            for hh in range(CM_HALVES):
                o_ref.at[b, 0, hh][pl.ds(c, h, stride=SUBLANES), :] = z[b][c][:, hh * LANES:(hh + 1) * LANES]


def _hy_conv(v, x1, x2, kc, ksum, hy_bias):
    B, ng = v.shape[0], v.shape[1]
    assert B == 2 and v.shape[3] * 2 == HY_N2 * SUBLANES and kc.shape[2] == HY_N2 * SUBLANES
    consts = [_dft_const(n) for n in ("rows_data", "rows_filt", "rows_out", "lanes", "twr", "twi")]
    tok = pl.BlockSpec((B, 1) + v.shape[2:], lambda g: (0, g, 0, 0, 0))
    kspec = lambda o: pl.BlockSpec((1,) + kc.shape[1:], lambda g: (o * ng + g, 0, 0, 0))
    sspec = lambda o: pl.BlockSpec((SUBLANES, ksum.shape[1]), lambda g: (o * ng + g, 0))
    full = lambda a: pl.BlockSpec(a.shape, lambda g: (0,) * a.ndim)
    return pl.pallas_call(
        _hy_conv_kernel,
        out_shape=jax.ShapeDtypeStruct(v.shape, F32),
        grid=(ng,),
        in_specs=[tok, tok, tok, kspec(0), kspec(1), sspec(0), sspec(1),
                  pl.BlockSpec(memory_space=pltpu.SMEM)] + [full(a) for a in consts],
        out_specs=tok,
        scratch_shapes=[pltpu.VMEM((SUBLANES * HY_N2, 2 * HY_N1), BF16),
                        pltpu.VMEM((HY_ORDER, SUBLANES * HY_N2, 2 * HY_N1), F32)],
        compiler_params=pltpu.CompilerParams(vmem_limit_bytes=VMEM_LIMIT),
        name="hy_conv",
    )(v, x1, x2, kc, kc, ksum, ksum, hy_bias, *consts)


def _hy_filter_kernel(w1t_ref, w1c_ref, w1s_ref, b1_ref, f0_ref, w2_ref, b2_ref, f1_ref, w3_ref, dl_ref,
                      k_ref, s_ref, *, L, P, chan_major):
    i = pl.program_id(0)

    def pos_t(rows):
        n = i * P + lax.broadcasted_iota(jnp.int32, (rows, P), 1)
        return n, jnp.where(n < L, n, 2 * L - n).astype(F32) / L

    _, tb = pos_t(HY_BANDS)
    band = (lax.broadcasted_iota(jnp.int32, tb.shape, 0) + 1).astype(F32)
    ang = 2.0 * math.pi * tb * band
    _, t1 = pos_t(1)
    h = w1t_ref[...] * t1 + b1_ref[...]
    h = h + jnp.dot(w1c_ref[...], jnp.cos(ang).astype(BF16), preferred_element_type=F32)
    h = h + jnp.dot(w1s_ref[...], jnp.sin(ang).astype(BF16), preferred_element_type=F32)
    h = jnp.sin(f0_ref[...] * h)
    h = jnp.sin(f1_ref[...] * (jnp.dot(w2_ref[...], h.astype(BF16), preferred_element_type=F32) + b2_ref[...]))
    k = jnp.dot(w3_ref[0], h.astype(BF16), preferred_element_type=F32)
    n, t = pos_t(1)
    k = jnp.where(n == L, 0.0, k * jnp.exp(-t * dl_ref[...]))
    if chan_major:
        for j in range(P // CM_LANES):
            for hh in range(CM_HALVES):
                lane0 = j * CM_LANES + hh * LANES
                k_ref[:, hh, j * SUBLANES:(j + 1) * SUBLANES, :] = k[:, lane0:lane0 + LANES].reshape(
                    k_ref.shape[0], SUBLANES, LANES)
    else:
        k_ref[...] = k
    ka = jnp.abs(k)
    tot = ka[:, 0:128]
    for j in range(1, P // 128):
        tot = tot + ka[:, j * 128:(j + 1) * 128]

    @pl.when(i == 0)
    def _():
        s_ref[...] = jnp.zeros_like(s_ref)

    s_ref[...] += tot


def _hy_filter(L, P, chan_major, f_w1, f_b1, f_w2, f_b2, f_w3, f_freq):
    nb = 2 * L // P
    assert nb % 2 == 0 and P % CM_LANES == 0
    hid = f_w2.shape[0]
    oc = HY_ORDER * HY_WIDTH
    col = lambda v: v.reshape(-1, 1)
    w3 = f_w3.reshape(hid, HY_ORDER, 2, HY_WIDTH)
    w3 = jnp.stack([w3[:, :, d].reshape(hid, oc).T for d in range(2)]).astype(BF16)
    deltas = np.abs(np.linspace(math.log(HY_DECAY_TARGET) / HY_FAST_DECAY,
                                math.log(HY_DECAY_TARGET) / HY_SLOW_DECAY, HY_WIDTH, dtype=np.float32))
    ins = [col(f_w1[0]), f_w1[1:1 + HY_BANDS].T.astype(BF16), f_w1[1 + HY_BANDS:].T.astype(BF16), col(f_b1),
           col(f_freq[0]), f_w2.T.astype(BF16), col(f_b2), col(f_freq[1]), w3,
           jnp.asarray(np.tile(deltas, HY_ORDER)[:, None])]
    full = lambda a: pl.BlockSpec(a.shape, lambda i: (0,) * a.ndim)
    specs = [full(a) for a in ins]
    specs[8] = pl.BlockSpec((1,) + w3.shape[1:], lambda i: (i // (nb // 2), 0, 0))
    if chan_major:
        kshape = (oc // SUBLANES, CM_HALVES, 2 * L // CM_LANES * SUBLANES, LANES)
        kspec = pl.BlockSpec((oc // SUBLANES, CM_HALVES, P // CM_LANES * SUBLANES, LANES), lambda i: (0, 0, i, 0))
    else:
        kshape = (oc, 2 * L)
        kspec = pl.BlockSpec((oc, P), lambda i: (0, i))
    return pl.pallas_call(
        functools.partial(_hy_filter_kernel, L=L, P=P, chan_major=chan_major),
        out_shape=[jax.ShapeDtypeStruct(kshape, F32), jax.ShapeDtypeStruct((oc, 128), F32)],
        grid=(nb,),
        in_specs=specs,
        out_specs=[kspec, pl.BlockSpec((oc, 128), lambda i: (0, 0))],
        compiler_params=pltpu.CompilerParams(dimension_semantics=("arbitrary",), vmem_limit_bytes=VMEM_LIMIT),
        name="hy_filter",
    )(*ins)


def _hyena(v, x1, x2, filt, hy_bias):
    L = v.shape[3] // SUBLANES * CM_LANES
    assert 2 * L == HY_N1 * HY_N2
    kc, ksum = _hy_filter(L, HY_FILT_COLS, True, *filt)
    return _hy_conv(v, x1, x2, kc, ksum, hy_bias)


@functools.lru_cache(maxsize=None)
def _ctx_dft_consts(n):
    i = np.arange(n, dtype=np.float64)
    a = 2.0 * np.pi * np.outer(i, i) / n
    f = np.concatenate([np.cos(a), -np.sin(a)], axis=0)
    return np.asarray(f, np.float32), np.asarray(f.T[:n // 2], np.float32)


def _hyena_ctx_kernel(v_ref, x1_ref, x2_ref, kc_ref, ks_ref, bias_ref, ff_ref, fi_ref, o_ref):
    n = ff_ref.shape[1]
    lc = n // 2
    C = v_ref.shape[2]
    ff = ff_ref[...].astype(BF16)
    inv = 1.0 / (jnp.sum(ks_ref[...], axis=1, keepdims=True) * float(n))
    kn = (kc_ref[...] * inv).T
    kspec = jnp.dot(ff, kn.astype(BF16), preferred_element_type=F32)
    z = jnp.concatenate([v_ref[0], v_ref[1]], axis=1)
    for o, g_ref in enumerate((x1_ref, x2_ref)):
        x = jnp.dot(ff[:, :lc], z.astype(BF16), preferred_element_type=F32)
        k = kspec[:, o * C:(o + 1) * C]
        kr = jnp.concatenate([k[:n], k[:n]], axis=1)
        ki = jnp.concatenate([k[n:], k[n:]], axis=1)
        yr, yi = _cmul(x[:n], x[n:], kr, ki)
        y = jnp.concatenate([yr, yi], axis=0)
        conv = jnp.dot(fi_ref[...].astype(BF16), y.astype(BF16), preferred_element_type=F32)
        bias = jnp.concatenate([bias_ref[o:o + 1, :], bias_ref[o:o + 1, :]], axis=1)
        z = jnp.concatenate([g_ref[0], g_ref[1]], axis=1) * (conv + bias * z)
    o_ref[0] = z[:, :C]
    o_ref[1] = z[:, C:]


def _hyena_ctx(v, x1, x2, filt, hy_bias):
    B, Lc, C = v.shape
    assert B == 2
    kc, ksum = _hy_filter(Lc, Lc, False, *filt)
    ff, fi = (jnp.asarray(m, F32) for m in _ctx_dft_consts(2 * Lc))
    return pl.pallas_call(
        _hyena_ctx_kernel,
        out_shape=jax.ShapeDtypeStruct((B, Lc, C), F32),
        compiler_params=pltpu.CompilerParams(vmem_limit_bytes=VMEM_LIMIT),
        name="hyena_ctx",
    )(v, x1, x2, kc, ksum, hy_bias, ff, fi)


NA_ROWS_PER_STEP = 8
NA_SOFTMAX_ROWS = 32
_NT = (((1,), (1,)), ((), ()))


def _na_bias_table(rpb):
    qc = np.arange(GRID_W)[:, None]
    kc = np.arange(GRID_W)[None, :]
    start = np.clip(qc - NA_WIN_COLS // 2, 0, GRID_W - NA_WIN_COLS)
    valid = (kc >= start) & (kc < start + NA_WIN_COLS)
    pad = jnp.pad(rpb, ((0, 0), (0, 0), (GRID_W, GRID_W)))
    shift = GRID_W + NA_WIN_COLS - 1
    toep = jnp.stack([pad[:, :, shift - c:shift - c + GRID_W] for c in range(GRID_W)], axis=2)
    full = jnp.where(jnp.asarray(valid)[None, None], toep * LOG2E, NEG_INF)
    t = jnp.stack([full[:, d:d + NA_WIN_ROWS] for d in range(NA_WIN_ROWS)], axis=1)
    t = t.reshape(NA_HEADS // 2, 2, NA_WIN_ROWS, NA_WIN_ROWS, GRID_W, GRID_W).transpose(0, 2, 1, 4, 3, 5)
    return t.reshape(NA_HEADS // 2, NA_WIN_ROWS, 2 * GRID_W, NA_WIN_ROWS * GRID_W).astype(F32)


def _na_kernel(q_ref, kp_ref, kc_ref, kn_ref, vp_ref, vc_ref, vn_ref, ck_ref, cv_ref, bias_ref, o_ref,
               wk_ref, ws_ref, wv_ref, sc_ref, pc_ref, ol_ref, li_ref, sl_ref, pl_ref):
    i = pl.program_id(2)
    last = pl.num_programs(2) - 1
    blk = NA_ROWS_PER_STEP * GRID_W
    for n, (kr, vr) in enumerate(((kp_ref, vp_ref), (kc_ref, vc_ref), (kn_ref, vn_ref))):
        wk_ref[:, n * blk:(n + 1) * blk] = kr[0]
        wv_ref[n * blk:(n + 1) * blk] = vr[0]
    ws_ref[:, 0:3 * blk - GRID_W] = wk_ref[:, GRID_W:3 * blk]

    def key_window(off):
        if off % 2 == 0:
            return wk_ref[:, off * GRID_W:(off + NA_WIN_ROWS) * GRID_W]
        return ws_ref[:, (off - 1) * GRID_W:(off - 1 + NA_WIN_ROWS) * GRID_W]
    first_head = lax.broadcasted_iota(jnp.int32, (GRID_W, 2 * NA_HEAD_DIM), 1) < NA_HEAD_DIM
    q = q_ref[0]
    zero = jnp.zeros((GRID_W, 2 * NA_HEAD_DIM), q.dtype)
    pieces = []
    for j in range(NA_ROWS_PER_STEP):
        qj = q[j * GRID_W:(j + 1) * GRID_W]
        pieces += [jnp.where(first_head, qj, zero), jnp.where(first_head, zero, qj)]
    qs = jnp.concatenate(pieces, axis=0)
    sc_ref[...] = lax.dot_general(qs, ck_ref[0], _NT, preferred_element_type=F32)
    half = NA_WIN_ROWS // 2
    rows = 2 * GRID_W

    def window_row(j):
        return jnp.where(i == 0, max(j + half, NA_ROWS_PER_STEP),
                         jnp.where(i == last, min(j + half, NA_ROWS_PER_STEP), j + half))

    for j in range(NA_ROWS_PER_STEP):
        kw = key_window(j + half)
        if j < half:
            kw = jnp.where(i == 0, key_window(NA_ROWS_PER_STEP), kw)
        elif j > half:
            kw = jnp.where(i == last, key_window(NA_ROWS_PER_STEP), kw)
        sl_ref[j] = jnp.dot(qs[j * rows:(j + 1) * rows], kw, preferred_element_type=F32)
    for j in range(NA_ROWS_PER_STEP):
        d = window_row(j) - j - 1
        for c0 in range(0, rows, NA_SOFTMAX_ROWS):
            rs = slice(c0, c0 + NA_SOFTMAX_ROWS)
            ra = slice(j * rows + c0, j * rows + c0 + NA_SOFTMAX_ROWS)
            s_loc = sl_ref[j, rs, :] + bias_ref[0, d, rs, :]
            s_ctx = sc_ref[ra, :]
            m = jnp.maximum(jnp.max(s_loc, axis=1, keepdims=True), jnp.max(s_ctx, axis=1, keepdims=True))
            p_loc = jnp.exp2(s_loc - m)
            p_ctx = jnp.exp2(s_ctx - m)
            l = jnp.sum(p_loc, axis=1, keepdims=True) + jnp.sum(p_ctx, axis=1, keepdims=True)
            pl_ref[j, rs, :] = p_loc.astype(BF16)
            pc_ref[ra, :] = p_ctx.astype(pc_ref.dtype)
            li_ref[ra, :] = jnp.broadcast_to(1.0 / l, (NA_SOFTMAX_ROWS, 2 * NA_HEAD_DIM))
    for j in range(NA_ROWS_PER_STEP):
        start = pl.multiple_of(window_row(j) * GRID_W, GRID_W)
        vw = wv_ref[pl.ds(start, NA_WIN_ROWS * GRID_W), :]
        ol_ref[j * rows:(j + 1) * rows, :] = jnp.dot(pl_ref[j], vw, preferred_element_type=F32)
    o = (ol_ref[...] + jnp.dot(pc_ref[...], cv_ref[0], preferred_element_type=F32)) * li_ref[...]
    for j in range(NA_ROWS_PER_STEP):
        r0 = j * rows
        oj = jnp.where(first_head, o[r0:r0 + GRID_W], o[r0 + GRID_W:r0 + rows])
        o_ref[0, j * GRID_W:(j + 1) * GRID_W, :] = oj.astype(o_ref.dtype)


def _na(q, k, v, ck, cv, bias):
    B, L, W = q.shape
    Lc = ck.shape[1]
    blk = NA_ROWS_PER_STEP * GRID_W
    nblk = L // blk
    assert NA_ROWS_PER_STEP == NA_WIN_ROWS and nblk >= 2 and GRID_W * 2 == LANES
    pw = 2 * NA_HEAD_DIM
    cur = pl.BlockSpec((1, blk, pw), lambda b, h, i: (b, i, h))
    prev = pl.BlockSpec((1, blk, pw), lambda b, h, i: (b, jnp.maximum(i - 1, 0), h))
    nxt = pl.BlockSpec((1, blk, pw), lambda b, h, i: (b, jnp.minimum(i + 1, nblk - 1), h))
    kcur = pl.BlockSpec((1, pw, blk), lambda b, h, i: (b, h, i))
    kprev = pl.BlockSpec((1, pw, blk), lambda b, h, i: (b, h, jnp.maximum(i - 1, 0)))
    knxt = pl.BlockSpec((1, pw, blk), lambda b, h, i: (b, h, jnp.minimum(i + 1, nblk - 1)))
    cspec = pl.BlockSpec((1, Lc, pw), lambda b, h, i: (b, 0, h))
    bspec = pl.BlockSpec((1,) + bias.shape[1:], lambda b, h, i: (h, 0, 0, 0))
    stacked = 2 * blk
    return pl.pallas_call(
        _na_kernel,
        out_shape=jax.ShapeDtypeStruct((B, L, W), BF16),
        grid=(B, W // pw, nblk),
        in_specs=[cur, kprev, kcur, knxt, prev, cur, nxt, cspec, cspec, bspec],
        out_specs=cur,
        scratch_shapes=[pltpu.VMEM((pw, 3 * blk), BF16), pltpu.VMEM((pw, 3 * blk), BF16),
                        pltpu.VMEM((3 * blk, pw), BF16),
                        pltpu.VMEM((stacked, Lc), F32), pltpu.VMEM((stacked, Lc), BF16),
                        pltpu.VMEM((stacked, pw), F32), pltpu.VMEM((stacked, pw), F32),
                        pltpu.VMEM((NA_ROWS_PER_STEP, 2 * GRID_W, NA_WIN_ROWS * GRID_W), F32),
                        pltpu.VMEM((NA_ROWS_PER_STEP, 2 * GRID_W, NA_WIN_ROWS * GRID_W), BF16)],
        compiler_params=pltpu.CompilerParams(vmem_limit_bytes=VMEM_LIMIT),
        name="nattn",
    )(q, k, k, k, v, v, v, ck, cv, bias)


def _ctx_attn_kernel(q_ref, k_ref, v_ref, o_ref):
    q = q_ref[0]
    k = k_ref[0]
    v = v_ref[0]
    first_head = lax.broadcasted_iota(jnp.int32, q.shape, 1) < NA_HEAD_DIM
    outs = []
    for h in range(2):
        qm = jnp.where(first_head if h == 0 else jnp.logical_not(first_head), q, jnp.zeros_like(q))
        s = lax.dot_general(qm, k, _NT, preferred_element_type=F32)
        p = jnp.exp2(s - jnp.max(s, axis=1, keepdims=True))
        o = jnp.dot(p.astype(BF16), v, preferred_element_type=F32)
        outs.append(o / jnp.sum(p, axis=1, keepdims=True))
    o_ref[0] = jnp.where(first_head, outs[0], outs[1]).astype(o_ref.dtype)


def _ctx_attn(q, k, v):
    B, Lc, W = q.shape
    pw = 2 * NA_HEAD_DIM
    spec = pl.BlockSpec((1, Lc, pw), lambda b, h: (b, 0, h))
    return pl.pallas_call(
        _ctx_attn_kernel,
        out_shape=jax.ShapeDtypeStruct((B, Lc, W), BF16),
        grid=(B, W // pw),
        in_specs=[spec, spec, spec],
        out_specs=spec,
        name="ctx_attn",
    )(q, k, v)


LRU_CHUNK = 1024


def _lru_gate_weights(wa, ba, wi, bi, lam):
    def bd(w):
        return jax.scipy.linalg.block_diag(*[w[n] for n in range(LRU_BLOCKS)])
    wg = jnp.stack([jnp.concatenate([bd(wa[d]), bd(wi[d])], axis=1) for d in range(2)]).astype(BF16)
    bg = jnp.stack([jnp.concatenate([ba[d], bi[d]])[None, :] for d in range(2)])
    return wg, bg, lam[:, None, :]


def _lru_coeffs(u, wg, bg, lam):
    C = u.shape[1]
    g = jnp.dot(u.astype(BF16), wg, preferred_element_type=F32) + bg
    sig = 0.5 + 0.5 * jnp.tanh(0.5 * g)
    r, ig = sig[:, :C], sig[:, C:]
    nl = -lam
    softplus = jnp.maximum(nl, 0.0) + jnp.log(1.0 + jnp.exp(-jnp.abs(nl)))
    log_a = (-LRU_C * softplus) * r
    a = jnp.exp(log_a)
    t = jnp.tanh(log_a)
    b = jnp.sqrt(-2.0 * t / (1.0 - t)) * (ig * u)
    return a, b


def _lru_scan(a, b, h0, reverse, ac_ref, bc_ref, h_ref):
    T, C = a.shape
    row = lax.broadcasted_iota(jnp.int32, a.shape, 0) % SUBLANES
    for s in (1, 2, 4):
        shift = T - s if reverse else s
        keep = (row < SUBLANES - s) if reverse else (row >= s)
        b = jnp.where(keep, a * pltpu.roll(b, shift, 0) + b, b)
        a = jnp.where(keep, a * pltpu.roll(a, shift, 0), a)
    ac_ref[...] = a
    bc_ref[...] = b
    ng = T // SUBLANES

    def group(g, h):
        r0 = pl.multiple_of((ng - 1 - g if reverse else g) * SUBLANES, SUBLANES)
        hr = ac_ref[pl.ds(r0, SUBLANES), :] * h + bc_ref[pl.ds(r0, SUBLANES), :]
        h_ref[pl.ds(r0, SUBLANES), :] = hr
        edge = hr[0:1] if reverse else hr[SUBLANES - 1:SUBLANES]
        return jnp.broadcast_to(edge, (SUBLANES, C))

    return lax.fori_loop(0, ng, group, h0, unroll=4)


def _gelu_tanh(x):
    return 0.5 * x * (1.0 + jnp.tanh(math.sqrt(2.0 / math.pi) * (x + 0.044715 * (x * x * x))))


def _lru_ctx_kernel(u_ref, xg_ref, wg_ref, bg_ref, lam_ref, hend_ref, yc_ref, ac_ref, bc_ref, h_ref):
    u = u_ref[0]
    C = u.shape[1]
    total = jnp.zeros_like(u)
    for d, rev in enumerate((False, True)):
        a, b = _lru_coeffs(u, wg_ref[d], bg_ref[d], lam_ref[d])
        hl = _lru_scan(a, b, jnp.zeros((SUBLANES, C), F32), rev, ac_ref, bc_ref, h_ref)
        hend_ref[0, d:d + 1, :] = hl[0:1]
        total = total + h_ref[...]
    yc_ref[0] = (total * _gelu_tanh(xg_ref[0])).astype(yc_ref.dtype)


def _lru_ctx(u, xg, wg, bg, lam):
    B, Lc, C = u.shape
    tok = pl.BlockSpec((1, Lc, C), lambda b: (b, 0, 0))
    full = lambda a: pl.BlockSpec(a.shape, lambda b: (0,) * a.ndim)
    return pl.pallas_call(
        _lru_ctx_kernel,
        out_shape=[jax.ShapeDtypeStruct((B, 2, C), F32), jax.ShapeDtypeStruct((B, Lc, C), BF16)],
        grid=(B,),
        in_specs=[tok, tok, full(wg), full(bg), full(lam)],
        out_specs=[pl.BlockSpec((1, 2, C), lambda b: (b, 0, 0)), tok],
        scratch_shapes=[pltpu.VMEM((Lc, C), F32)] * 3,
        name="lru_ctx",
    )(u, xg, wg, bg, lam)


def _lru_dir_kernel(*refs, d, reverse):
    if reverse:
        u_ref, hend_ref, wg_ref, bg_ref, lam_ref, hf_ref, xg_ref, o_ref, ac_ref, bc_ref, h_ref, carry_ref = refs
    else:
        u_ref, hend_ref, wg_ref, bg_ref, lam_ref, o_ref, ac_ref, bc_ref, carry_ref = refs
        h_ref = o_ref.at[0]
    C = u_ref.shape[2]

    @pl.when(pl.program_id(1) == 0)
    def _():
        carry_ref[...] = jnp.broadcast_to(hend_ref[0, d:d + 1, :], (SUBLANES, C))

    a, b = _lru_coeffs(u_ref[0], wg_ref[d], bg_ref[d], lam_ref[d])
    carry_ref[...] = _lru_scan(a, b, carry_ref[...], reverse, ac_ref, bc_ref, h_ref)
    if reverse:
        o_ref[0] = ((hf_ref[0] + h_ref[...]) * _gelu_tanh(xg_ref[0])).astype(o_ref.dtype)


def _lru_dir(u, hend, wg, bg, lam, hf=None, xg=None):
    B, L, C = u.shape
    reverse = hf is not None
    T = LRU_CHUNK
    nb = L // T
    tok = pl.BlockSpec((1, T, C), (lambda b, i: (b, nb - 1 - i, 0)) if reverse else (lambda b, i: (b, i, 0)))
    full = lambda a: pl.BlockSpec(a.shape, lambda b, i: (0,) * a.ndim)
    ins = [u, hend, wg, bg, lam] + ([hf, xg] if reverse else [])
    specs = [tok, pl.BlockSpec((1, 2, C), lambda b, i: (b, 0, 0)), full(wg), full(bg), full(lam)]
    specs += [tok, tok] if reverse else []
    scratch = [pltpu.VMEM((T, C), F32)] * (3 if reverse else 2) + [pltpu.VMEM((SUBLANES, C), F32)]
    return pl.pallas_call(
        functools.partial(_lru_dir_kernel, d=int(reverse), reverse=reverse),
        out_shape=jax.ShapeDtypeStruct((B, L, C), BF16 if reverse else F32),
        grid=(B, nb),
        in_specs=specs,
        out_specs=tok,
        scratch_shapes=scratch,
        compiler_params=pltpu.CompilerParams(dimension_semantics=("arbitrary", "arbitrary")),
        name="lru_bwd" if reverse else "lru_fwd",
    )(*ins)


def _lru(u, xg, u_c, xg_c, wa, ba, wi, bi, lam):
    wg, bg, lam3 = _lru_gate_weights(wa, ba, wi, bi, lam)
    hend, yc = _lru_ctx(u_c, xg_c, wg, bg, lam3)
    hf = _lru_dir(u, hend, wg, bg, lam3)
    return _lru_dir(u, hend, wg, bg, lam3, hf, xg), yc


def kernel(x, c, ctx, c_ctx, ada_w, ada_b, g_mix_pre, g_mix_post, g_ffn_pre, g_ffn_post, w_in, w_out, hy_conv_w,
           hy_conv_b, hy_f_w1, hy_f_b1, hy_f_w2, hy_f_b2, hy_f_w3, hy_f_freq, hy_bias, na_rpb, lru_conv_w,
           lru_conv_b, lru_wa, lru_ba, lru_wi, lru_bi, lru_lam, ffn_w_gu, ffn_w_down):
    B, L, D = x.shape
    Lc = ctx.shape[1]
    tm = 1024

    assert B + 1 <= SUBLANES
    cond_t = jnp.zeros((D, SUBLANES), F32).at[:, 0:B].set(c.T).at[:, B].set(c_ctx)
    mods = _modulation(cond_t, B + 1, ada_w, ada_b)

    xc = ctx
    for l in range(DEPTH):
        with_ctx_out = l < DEPTH - 1
        m = mods[l].reshape(8, 6, D)
        lat = [m[0:B, j][:, None, :] for j in range(6)]
        cx = [jnp.broadcast_to(m[B, j][None, None, :], (B, 1, D)) for j in range(6)]
        row = lambda a: a.reshape(1, -1)

        w_in_bf = w_in[l].astype(BF16)
        w_out_bf = w_out[l].astype(BF16)
        wgu = ffn_w_gu[l].astype(BF16)
        wd = ffn_w_down[l].astype(BF16)

        conv_args = (hy_conv_w[l], row(hy_conv_b[l]), lru_conv_w[l], row(lru_conv_b[l]))
        hv, hx1, hx2, q, k, v, lu, lg = _inproj(x, lat[1], lat[0], row(g_mix_pre[l]), w_in_bf, *conv_args, tm=tm,
                                                chan_major=True)
        cv, cx1, cx2, cq, ck, cvv, clu, clg = _inproj(xc, cx[1], cx[0], row(g_mix_pre[l]), w_in_bf, *conv_args,
                                                      tm=Lc, chan_major=False)

        filt = (hy_f_w1[l], hy_f_b1[l], hy_f_w2[l], hy_f_b2[l], hy_f_w3[l], hy_f_freq[l])
        y_hy = _hyena(hv, hx1, hx2, filt, hy_bias[l])
        y_na = _na(q, k, v, ck, cvv, _na_bias_table(na_rpb[l]))
        y_lru, yc_lru = _lru(lu, lg, clu, clg, lru_wa[l], lru_ba[l], lru_wi[l], lru_bi[l], lru_lam[l])

        ffn_args = (row(g_ffn_pre[l]), row(g_ffn_post[l]), wgu, wd)
        x = _out_ffn(y_hy, y_na, y_lru, w_out_bf, x, row(g_mix_post[l]), lat[2], lat[4], lat[3], lat[5], *ffn_args,
                     tm=tm, chan_major=True)

        if with_ctx_out:
            yc_hy = _hyena_ctx(cv, cx1, cx2, filt, hy_bias[l])
            yc_na = _ctx_attn(cq, ck, cvv)
            xc = _out_ffn(yc_hy, yc_na, yc_lru, w_out_bf, xc, row(g_mix_post[l]), cx[2], cx[4], cx[3], cx[5],
                          *ffn_args, tm=Lc, chan_major=False)
    return x
```

```python
import functools
import math

import jax
import jax.numpy as jnp
import numpy as np
from jax import lax
from jax.experimental import pallas as pl
from jax.experimental.pallas import tpu as pltpu

F32 = jnp.float32
BF16 = jnp.bfloat16

D_MODEL = 1024
DEPTH = 2
GRID_W = 64
HY_WIDTH = D_MODEL // 4
NA_HEAD_DIM = 64
NA_WIDTH = D_MODEL // 2
NA_HEADS = NA_WIDTH // NA_HEAD_DIM
LRU_WIDTH = D_MODEL // 4
LRU_BLOCKS = 4
IN_WIDTH = 3 * HY_WIDTH + 3 * NA_WIDTH + 2 * LRU_WIDTH
HY_ORDER = 2
HY_BANDS = 16
HY_FAST_DECAY = 0.3
HY_SLOW_DECAY = 1.5
HY_DECAY_TARGET = 1e-2
NA_WIN_ROWS = 8
NA_WIN_COLS = 16
LRU_C = 8.0
D_FF = -(-8 * D_MODEL // (3 * 256)) * 256
RMS_EPS = 1e-6
NEG_INF = -1e30
LOG2E = math.log2(math.e)

_HY_END = 3 * HY_WIDTH
_NA_END = _HY_END + 3 * NA_WIDTH
_LRU_MID = _NA_END + LRU_WIDTH

HALO = 8
SUBLANES = 8
VMEM_LIMIT = 56 * 1024 * 1024


def _rms(x, g):
    return x * lax.rsqrt(jnp.mean(x * x, axis=-1, keepdims=True) + RMS_EPS) * g


def _mod_kernel(ct_ref, w_ref, b_ref, o_ref, *, n_cond):
    ct = ct_ref[...]
    st = ct * jax.nn.sigmoid(ct)
    w = w_ref[0]
    rows = [jnp.sum(w * st[:, r:r + 1], axis=0, keepdims=True) for r in range(n_cond)]
    rows.append(jnp.zeros((SUBLANES - n_cond, w.shape[1]), F32))
    o_ref[0] = jnp.concatenate(rows, axis=0) + b_ref[0]


def _modulation(cond_t, n_cond, ada_w, ada_b):
    tn = 768
    n = ada_w.shape[-1]
    return pl.pallas_call(
        functools.partial(_mod_kernel, n_cond=n_cond),
        out_shape=jax.ShapeDtypeStruct((DEPTH, SUBLANES, n), F32),
        grid=(DEPTH, n // tn),
        in_specs=[pl.BlockSpec((D_MODEL, SUBLANES), lambda l, j: (0, 0)),
                  pl.BlockSpec((1, D_MODEL, tn), lambda l, j: (l, 0, j)),
                  pl.BlockSpec((1, 1, tn), lambda l, j: (l, 0, j))],
        out_specs=pl.BlockSpec((1, SUBLANES, tn), lambda l, j: (l, 0, j)),
        compiler_params=pltpu.CompilerParams(vmem_limit_bytes=VMEM_LIMIT),
        name="adaln_modulation",
    )(cond_t, ada_w, ada_b.reshape(DEPTH, 1, n))


CM_LANES = 256
LANES = 128
CM_HALVES = CM_LANES // LANES


def _chan_major_shape(B, L, C):
    return (B, C // SUBLANES, CM_HALVES, L // CM_LANES * SUBLANES, LANES)


def _chan_major_spec(tm, C):
    return pl.BlockSpec((1, C // SUBLANES, CM_HALVES, tm // CM_LANES * SUBLANES, LANES),
                        lambda b, i: (b, 0, 0, i, 0))


def _store_chan_major(ref, u):
    ut = u.T
    for g in range(u.shape[1] // SUBLANES):
        for j in range(u.shape[0] // CM_LANES):
            for h in range(CM_HALVES):
                lane0 = j * CM_LANES + h * LANES
                ref[0, g, h, j * SUBLANES:(j + 1) * SUBLANES, :] = ut[g * SUBLANES:(g + 1) * SUBLANES,
                                                                      lane0:lane0 + LANES]


def _load_chan_major(ref):
    _, ng, _, nr, _ = ref.shape
    rows = [jnp.concatenate([ref[0, g, h, j * SUBLANES:(j + 1) * SUBLANES, :]
                             for j in range(nr // SUBLANES) for h in range(CM_HALVES)], axis=1) for g in range(ng)]
    return jnp.concatenate(rows, axis=0).T


def _chan_rows(ref, lead, c, n):
    return jnp.concatenate([ref.at[lead + (h,)][pl.ds(c, n, stride=SUBLANES), :] for h in range(CM_HALVES)], axis=1)


def _inproj_kernel(xp_ref, xc_ref, xn_ref, sc_ref, sh_ref, g_ref, w_ref, hcw_ref, hcb_ref, lcw_ref, lcb_ref,
                   hv_ref, hx1_ref, hx2_ref, q_ref, k_ref, v_ref, lu_ref, lg_ref, pe_ref, *, tm, chan_major):
    i = pl.program_id(1)
    last = pl.num_programs(1) - 1
    g = g_ref[...]
    sc1 = 1.0 + sc_ref[0]
    sh = sh_ref[0]

    def norm_mod(xv):
        return _rms(xv, g) * sc1 + sh

    hp = norm_mod(xp_ref[0]) * (i > 0).astype(F32)
    hn = norm_mod(xn_ref[0]) * (i < last).astype(F32)
    he = jnp.concatenate([hp, norm_mod(xc_ref[0]), hn], axis=0).astype(BF16)

    pe_ref[:, 0:_HY_END] = jnp.dot(he, w_ref[:, 0:_HY_END], preferred_element_type=F32)
    pe_ref[:, _HY_END:] = jnp.dot(he, w_ref[:, _NA_END:_LRU_MID], preferred_element_type=F32)
    hc = he[HALO:HALO + tm]
    qkv = jnp.dot(hc, w_ref[:, _HY_END:_NA_END], preferred_element_type=F32)
    q_ref[0] = (qkv[:, 0:NA_WIDTH] * (NA_HEAD_DIM ** -0.5 * LOG2E)).astype(BF16)
    kk = qkv[:, NA_WIDTH:2 * NA_WIDTH]
    k_ref[0] = (kk.T if chan_major else kk).astype(BF16)
    v_ref[0] = qkv[:, 2 * NA_WIDTH:].astype(BF16)
    lg_ref[0] = jnp.dot(hc, w_ref[:, _LRU_MID:], preferred_element_type=F32)

    u = hcb_ref[...]
    for kk in range(3):
        u = u + hcw_ref[kk:kk + 1, :] * pe_ref[pl.ds(HALO - 1 + kk, tm), 0:_HY_END]
    for n, ref in enumerate((hv_ref, hx1_ref, hx2_ref)):
        un = u[:, n * HY_WIDTH:(n + 1) * HY_WIDTH]
        if chan_major:
            _store_chan_major(ref, un)
        else:
            ref[0] = un
    ul = lcb_ref[...]
    for kk in range(4):
        ul = ul + lcw_ref[kk:kk + 1, :] * pe_ref[pl.ds(HALO - 2 + kk, tm), _HY_END:]
    lu_ref[0] = ul


def _inproj(x, sc, sh, g, w_bf, hcw, hcb, lcw, lcb, *, tm, chan_major):
    B, L, D = x.shape
    nb = tm // HALO
    nh = L // HALO
    tok = lambda w, dt: jax.ShapeDtypeStruct((B, L, w), dt)
    tspec = lambda w: pl.BlockSpec((1, tm, w), lambda b, i: (b, i, 0))
    full = lambda a: pl.BlockSpec(a.shape, lambda b, i: (0,) * a.ndim)
    vec = pl.BlockSpec((1, 1, D), lambda b, i: (b, 0, 0))
    if chan_major:
        hy_shape = jax.ShapeDtypeStruct(_chan_major_shape(B, L, HY_WIDTH), F32)
        hy_spec = _chan_major_spec(tm, HY_WIDTH)
        k_shape = jax.ShapeDtypeStruct((B, NA_WIDTH, L), BF16)
        k_spec = pl.BlockSpec((1, NA_WIDTH, tm), lambda b, i: (b, 0, i))
    else:
        hy_shape, hy_spec = tok(HY_WIDTH, F32), tspec(HY_WIDTH)
        k_shape, k_spec = tok(NA_WIDTH, BF16), tspec(NA_WIDTH)
    return pl.pallas_call(
        functools.partial(_inproj_kernel, tm=tm, chan_major=chan_major),
        out_shape=[hy_shape] * 3 + [tok(NA_WIDTH, BF16), k_shape, tok(NA_WIDTH, BF16)] + [tok(LRU_WIDTH, F32)] * 2,
        grid=(B, L // tm),
        in_specs=[pl.BlockSpec((1, HALO, D), lambda b, i: (b, jnp.maximum(i * nb - 1, 0), 0)),
                  pl.BlockSpec((1, tm, D), lambda b, i: (b, i, 0)),
                  pl.BlockSpec((1, HALO, D), lambda b, i: (b, jnp.minimum((i + 1) * nb, nh - 1), 0)),
                  vec, vec, full(g), pl.BlockSpec(w_bf.shape, lambda b, i: (0, 0), pipeline_mode=pl.Buffered(1)),
                  full(hcw), full(hcb), full(lcw), full(lcb)],
        out_specs=[hy_spec] * 3 + [tspec(NA_WIDTH), k_spec, tspec(NA_WIDTH)] + [tspec(LRU_WIDTH)] * 2,
        scratch_shapes=[pltpu.VMEM((tm + 2 * HALO, _HY_END + LRU_WIDTH), F32)],
        compiler_params=pltpu.CompilerParams(vmem_limit_bytes=VMEM_LIMIT),
        name="inproj",
    )(x, x, x, sc, sh, g, w_bf, hcw, hcb, lcw, lcb)


FF_CHUNK = 256


def _out_ffn_kernel(yh_ref, yn_ref, yl_ref, wo_ref, x_ref, gm_ref, gtm_ref, sc_ref, sh_ref, gt_ref, gpre_ref,
                    gpost_ref, wgu_ref, wd_ref, o_ref, x1_ref, h_ref, *, chan_major, halves):
    yh = (_load_chan_major(yh_ref) if chan_major else yh_ref[0]).astype(BF16)
    tm = x_ref.shape[1]
    for r0 in range(0, tm, tm // halves):
        rs = slice(r0, r0 + tm // halves)
        y = jnp.dot(yh[rs], wo_ref[0:HY_WIDTH], preferred_element_type=F32)
        y = y + jnp.dot(yn_ref[0, rs, :], wo_ref[HY_WIDTH:HY_WIDTH + NA_WIDTH], preferred_element_type=F32)
        y = y + jnp.dot(yl_ref[0, rs, :], wo_ref[HY_WIDTH + NA_WIDTH:], preferred_element_type=F32)
        x = x_ref[0, rs, :] + gtm_ref[0] * _rms(y, gm_ref[...])
        x1_ref[rs, :] = x
        h_ref[rs, :] = (_rms(x, gpre_ref[...]) * (1.0 + sc_ref[0]) + sh_ref[0]).astype(BF16)
    for r0 in range(0, tm, tm // halves):
        rs = slice(r0, r0 + tm // halves)
        h = h_ref[rs, :]
        acc = jnp.zeros((tm // halves, x_ref.shape[2]), F32)
        for c0 in range(0, D_FF, FF_CHUNK):
            g = jnp.dot(h, wgu_ref[:, c0:c0 + FF_CHUNK], preferred_element_type=F32)
            u = jnp.dot(h, wgu_ref[:, D_FF + c0:D_FF + c0 + FF_CHUNK], preferred_element_type=F32)
            a = (g * jax.nn.sigmoid(g) * u).astype(BF16)
            acc = acc + jnp.dot(a, wd_ref[c0:c0 + FF_CHUNK, :], preferred_element_type=F32)
        o_ref[0, rs, :] = x1_ref[rs, :] + gt_ref[0] * _rms(acc, gpost_ref[...])


def _out_ffn(yh, yn, yl, wo, x, gm, gtm, sc, sh, gt, gpre, gpost, wgu, wd, *, tm, chan_major):
    B, L, D = x.shape
    tspec = lambda w: pl.BlockSpec((1, tm, w), lambda b, i: (b, i, 0))
    vec = pl.BlockSpec((1, 1, D), lambda b, i: (b, 0, 0))
    full = lambda a: pl.BlockSpec(a.shape, lambda b, i: (0,) * a.ndim)
    res = lambda a: pl.BlockSpec(a.shape, lambda b, i: (0,) * a.ndim, pipeline_mode=pl.Buffered(1))
    hy_spec = _chan_major_spec(tm, HY_WIDTH) if chan_major else tspec(HY_WIDTH)
    return pl.pallas_call(
        functools.partial(_out_ffn_kernel, chan_major=chan_major, halves=2 if tm >= 1024 else 1),
        out_shape=jax.ShapeDtypeStruct((B, L, D), F32),
        grid=(B, L // tm),
        in_specs=[hy_spec, tspec(NA_WIDTH), tspec(LRU_WIDTH), res(wo), tspec(D), full(gm), vec,
                  vec, vec, vec, full(gpre), full(gpost), res(wgu), res(wd)],
        out_specs=tspec(D),
        scratch_shapes=[pltpu.VMEM((tm, D), F32), pltpu.VMEM((tm, D), BF16)],
        compiler_params=pltpu.CompilerParams(vmem_limit_bytes=VMEM_LIMIT),
        name="out_ffn",
    )(yh, yn, yl, wo, x, gm, gtm, sc, sh, gt, gpre, gpost, wgu, wd)


HY_N1 = CM_LANES
HY_N2 = 128
HY_FILT_COLS = 2048


def _block_cplx(re, im):
    return np.block([[re, -im], [im, re]])


@functools.lru_cache(maxsize=None)
def _dft_consts():
    n = HY_N1 * HY_N2
    h = HY_N2 // 2
    i1 = np.arange(HY_N1, dtype=np.float64)
    i2 = np.arange(HY_N2, dtype=np.float64)
    a1 = 2.0 * np.pi * np.outer(i1, i1) / HY_N1
    a2 = 2.0 * np.pi * np.outer(i2, i2) / HY_N2
    at = 2.0 * np.pi * np.outer(i2, i1) / n
    r2, m2 = np.cos(a2), -np.sin(a2)
    r1, m1 = np.cos(a1), -np.sin(a1)
    f32 = lambda m: np.asarray(m, np.float32)
    return dict(
        rows_data=f32(_block_cplx(r2[:, :h], m2[:, :h])),
        rows_filt=f32(np.concatenate([r2, m2], axis=0)),
        rows_out=f32(_block_cplx(r2[:h, :], m2[:h, :])),
        lanes=f32(np.block([[r1, m1], [-m1, r1]])),
        twr=f32(np.cos(at)), twi=f32(-np.sin(at)))


def _dft_const(name):
    return jnp.asarray(_dft_consts()[name], F32)


def _cmul(ar, ai, br, bi):
    return ar * br - ai * bi, ar * bi + ai * br


def _hy_conv_kernel(v_ref, x1_ref, x2_ref, k0_ref, k1_ref, ks0_ref, ks1_ref, bias_ref, rd_ref, rf_ref, ro_ref,
                    ln_ref, twr_ref, twi_ref, o_ref, st_ref, ks_ref):
    g = pl.program_id(0)
    n_ch = SUBLANES
    h = HY_N2 // 2
    rows_data = rd_ref[...].astype(BF16)
    rows_filt = rf_ref[...].astype(BF16)
    rows_out = ro_ref[...].astype(BF16)
    lanes = ln_ref[...].astype(BF16)
    twr, twi = twr_ref[...], twi_ref[...]

    def rows_then_twiddle(mat, x, c):
        a = jnp.dot(mat, x.astype(BF16), preferred_element_type=F32)
        ar, ai = _cmul(a[:HY_N2], a[HY_N2:], twr, twi)
        st_ref[c * HY_N2:(c + 1) * HY_N2, 0:HY_N1] = ar.astype(BF16)
        st_ref[c * HY_N2:(c + 1) * HY_N2, HY_N1:] = ai.astype(BF16)

    for o, (k_ref, s_ref) in enumerate(((k0_ref, ks0_ref), (k1_ref, ks1_ref))):
        for c in range(n_ch):
            rows_then_twiddle(rows_filt, _chan_rows(k_ref, (0,), c, HY_N2), c)
        spec = jnp.dot(st_ref[...], lanes, preferred_element_type=F32)
        inv = 1.0 / (jnp.sum(s_ref[...], axis=1, keepdims=True) * float(HY_N1 * HY_N2))
        for c in range(n_ch):
            ks_ref[o, c * HY_N2:(c + 1) * HY_N2, :] = spec[c * HY_N2:(c + 1) * HY_N2] * inv[c:c + 1, :]

    z = [[_chan_rows(v_ref, (b, 0), c, h) for c in range(n_ch)] for b in range(2)]
    for o, gate_ref in enumerate((x1_ref, x2_ref)):
        for c in range(n_ch):
            rows_then_twiddle(rows_data, jnp.concatenate([z[0][c], z[1][c]], axis=0), c)
        x = jnp.dot(st_ref[...], lanes, preferred_element_type=F32)
        pr, pi = _cmul(x[:, :HY_N1], x[:, HY_N1:], ks_ref[o, :, 0:HY_N1], ks_ref[o, :, HY_N1:])
        st_ref[:, 0:HY_N1] = pr.astype(BF16)
        st_ref[:, HY_N1:] = (-pi).astype(BF16)
        y = jnp.dot(st_ref[...], lanes, preferred_element_type=F32)
        for c in range(n_ch):
            yc = y[c * HY_N2:(c + 1) * HY_N2]
            yr, yi = _cmul(yc[:, :HY_N1], yc[:, HY_N1:], twr, twi)
            out = jnp.dot(rows_out, jnp.concatenate([yr, yi], axis=0).astype(BF16),
                          preferred_element_type=F32)
            bias = bias_ref[o, g * n_ch + c]
            for b, conv in ((0, out[:h]), (1, -out[h:])):
                z[b][c] = _chan_rows(gate_ref, (b, 0), c, h) * (conv + bias * z[b][c])
    for b in range(2):
        for c in range(n_ch):
            for hh in range(CM_HALVES):
                o_ref.at[b, 0, hh][pl.ds(c, h, stride=SUBLANES), :] = z[b][c][:, hh * LANES:(hh + 1) * LANES]


def _hy_conv(v, x1, x2, kc, ksum, hy_bias):
    B, ng = v.shape[0], v.shape[1]
    assert B == 2 and v.shape[3] * 2 == HY_N2 * SUBLANES and kc.shape[2] == HY_N2 * SUBLANES
    consts = [_dft_const(n) for n in ("rows_data", "rows_filt", "rows_out", "lanes", "twr", "twi")]
    tok = pl.BlockSpec((B, 1) + v.shape[2:], lambda g: (0, g, 0, 0, 0))
    kspec = lambda o: pl.BlockSpec((1,) + kc.shape[1:], lambda g: (o * ng + g, 0, 0, 0))
    sspec = lambda o: pl.BlockSpec((SUBLANES, ksum.shape[1]), lambda g: (o * ng + g, 0))
    full = lambda a: pl.BlockSpec(a.shape, lambda g: (0,) * a.ndim)
    return pl.pallas_call(
        _hy_conv_kernel,
        out_shape=jax.ShapeDtypeStruct(v.shape, F32),
        grid=(ng,),
        in_specs=[tok, tok, tok, kspec(0), kspec(1), sspec(0), sspec(1),
                  pl.BlockSpec(memory_space=pltpu.SMEM)] + [full(a) for a in consts],
        out_specs=tok,
        scratch_shapes=[pltpu.VMEM((SUBLANES * HY_N2, 2 * HY_N1), BF16),
                        pltpu.VMEM((HY_ORDER, SUBLANES * HY_N2, 2 * HY_N1), F32)],
        compiler_params=pltpu.CompilerParams(vmem_limit_bytes=VMEM_LIMIT),
        name="hy_conv",
    )(v, x1, x2, kc, kc, ksum, ksum, hy_bias, *consts)


def _hy_filter_kernel(w1t_ref, w1c_ref, w1s_ref, b1_ref, f0_ref, w2_ref, b2_ref, f1_ref, w3_ref, dl_ref,
                      k_ref, s_ref, *, L, P, chan_major):
    i = pl.program_id(0)

    def pos_t(rows):
        n = i * P + lax.broadcasted_iota(jnp.int32, (rows, P), 1)
        return n, jnp.where(n < L, n, 2 * L - n).astype(F32) / L

    _, tb = pos_t(HY_BANDS)
    band = (lax.broadcasted_iota(jnp.int32, tb.shape, 0) + 1).astype(F32)
    ang = 2.0 * math.pi * tb * band
    _, t1 = pos_t(1)
    h = w1t_ref[...] * t1 + b1_ref[...]
    h = h + jnp.dot(w1c_ref[...], jnp.cos(ang).astype(BF16), preferred_element_type=F32)
    h = h + jnp.dot(w1s_ref[...], jnp.sin(ang).astype(BF16), preferred_element_type=F32)
    h = jnp.sin(f0_ref[...] * h)
    h = jnp.sin(f1_ref[...] * (jnp.dot(w2_ref[...], h.astype(BF16), preferred_element_type=F32) + b2_ref[...]))
    k = jnp.dot(w3_ref[0], h.astype(BF16), preferred_element_type=F32)
    n, t = pos_t(1)
    k = jnp.where(n == L, 0.0, k * jnp.exp(-t * dl_ref[...]))
    if chan_major:
        for j in range(P // CM_LANES):
            for hh in range(CM_HALVES):
                lane0 = j * CM_LANES + hh * LANES
                k_ref[:, hh, j * SUBLANES:(j + 1) * SUBLANES, :] = k[:, lane0:lane0 + LANES].reshape(
                    k_ref.shape[0], SUBLANES, LANES)
    else:
        k_ref[...] = k
    ka = jnp.abs(k)
    tot = ka[:, 0:128]
    for j in range(1, P // 128):
        tot = tot + ka[:, j * 128:(j + 1) * 128]

    @pl.when(i == 0)
    def _():
        s_ref[...] = jnp.zeros_like(s_ref)

    s_ref[...] += tot


def _hy_filter(L, P, chan_major, f_w1, f_b1, f_w2, f_b2, f_w3, f_freq):
    nb = 2 * L // P
    assert nb % 2 == 0 and P % CM_LANES == 0
    hid = f_w2.shape[0]
    oc = HY_ORDER * HY_WIDTH
    col = lambda v: v.reshape(-1, 1)
    w3 = f_w3.reshape(hid, HY_ORDER, 2, HY_WIDTH)
    w3 = jnp.stack([w3[:, :, d].reshape(hid, oc).T for d in range(2)]).astype(BF16)
    deltas = np.abs(np.linspace(math.log(HY_DECAY_TARGET) / HY_FAST_DECAY,
                                math.log(HY_DECAY_TARGET) / HY_SLOW_DECAY, HY_WIDTH, dtype=np.float32))
    ins = [col(f_w1[0]), f_w1[1:1 + HY_BANDS].T.astype(BF16), f_w1[1 + HY_BANDS:].T.astype(BF16), col(f_b1),
           col(f_freq[0]), f_w2.T.astype(BF16), col(f_b2), col(f_freq[1]), w3,
           jnp.asarray(np.tile(deltas, HY_ORDER)[:, None])]
    full = lambda a: pl.BlockSpec(a.shape, lambda i: (0,) * a.ndim)
    specs = [full(a) for a in ins]
    specs[8] = pl.BlockSpec((1,) + w3.shape[1:], lambda i: (i // (nb // 2), 0, 0))
    if chan_major:
        kshape = (oc // SUBLANES, CM_HALVES, 2 * L // CM_LANES * SUBLANES, LANES)
        kspec = pl.BlockSpec((oc // SUBLANES, CM_HALVES, P // CM_LANES * SUBLANES, LANES), lambda i: (0, 0, i, 0))
    else:
        kshape = (oc, 2 * L)
        kspec = pl.BlockSpec((oc, P), lambda i: (0, i))
    return pl.pallas_call(
        functools.partial(_hy_filter_kernel, L=L, P=P, chan_major=chan_major),
        out_shape=[jax.ShapeDtypeStruct(kshape, F32), jax.ShapeDtypeStruct((oc, 128), F32)],
        grid=(nb,),
        in_specs=specs,
        out_specs=[kspec, pl.BlockSpec((oc, 128), lambda i: (0, 0))],
        compiler_params=pltpu.CompilerParams(dimension_semantics=("arbitrary",), vmem_limit_bytes=VMEM_LIMIT),
        name="hy_filter",
    )(*ins)


def _hyena(v, x1, x2, filt, hy_bias):
    L = v.shape[3] // SUBLANES * CM_LANES
    assert 2 * L == HY_N1 * HY_N2
    kc, ksum = _hy_filter(L, HY_FILT_COLS, True, *filt)
    return _hy_conv(v, x1, x2, kc, ksum, hy_bias)


@functools.lru_cache(maxsize=None)
def _ctx_dft_consts(n):
    i = np.arange(n, dtype=np.float64)
    a = 2.0 * np.pi * np.outer(i, i) / n
    f = np.concatenate([np.cos(a), -np.sin(a)], axis=0)
    return np.asarray(f, np.float32), np.asarray(f.T[:n // 2], np.float32)


def _hyena_ctx_kernel(v_ref, x1_ref, x2_ref, kc_ref, ks_ref, bias_ref, ff_ref, fi_ref, o_ref):
    n = ff_ref.shape[1]
    lc = n // 2
    C = v_ref.shape[2]
    ff = ff_ref[...].astype(BF16)
    inv = 1.0 / (jnp.sum(ks_ref[...], axis=1, keepdims=True) * float(n))
    kn = (kc_ref[...] * inv).T
    kspec = jnp.dot(ff, kn.astype(BF16), preferred_element_type=F32)
    z = jnp.concatenate([v_ref[0], v_ref[1]], axis=1)
    for o, g_ref in enumerate((x1_ref, x2_ref)):
        x = jnp.dot(ff[:, :lc], z.astype(BF16), preferred_element_type=F32)
        k = kspec[:, o * C:(o + 1) * C]
        kr = jnp.concatenate([k[:n], k[:n]], axis=1)
        ki = jnp.concatenate([k[n:], k[n:]], axis=1)
        yr, yi = _cmul(x[:n], x[n:], kr, ki)
        y = jnp.concatenate([yr, yi], axis=0)
        conv = jnp.dot(fi_ref[...].astype(BF16), y.astype(BF16), preferred_element_type=F32)
        bias = jnp.concatenate([bias_ref[o:o + 1, :], bias_ref[o:o + 1, :]], axis=1)
        z = jnp.concatenate([g_ref[0], g_ref[1]], axis=1) * (conv + bias * z)
    o_ref[0] = z[:, :C]
    o_ref[1] = z[:, C:]


def _hyena_ctx(v, x1, x2, filt, hy_bias):
    B, Lc, C = v.shape
    assert B == 2
    kc, ksum = _hy_filter(Lc, Lc, False, *filt)
    ff, fi = (jnp.asarray(m, F32) for m in _ctx_dft_consts(2 * Lc))
    return pl.pallas_call(
        _hyena_ctx_kernel,
        out_shape=jax.ShapeDtypeStruct((B, Lc, C), F32),
        compiler_params=pltpu.CompilerParams(vmem_limit_bytes=VMEM_LIMIT),
        name="hyena_ctx",
    )(v, x1, x2, kc, ksum, hy_bias, ff, fi)


NA_ROWS_PER_STEP = 8
NA_SOFTMAX_ROWS = 32
_NT = (((1,), (1,)), ((), ()))


def _na_bias_table(rpb):
    qc = np.arange(GRID_W)[:, None]
    kc = np.arange(GRID_W)[None, :]
    start = np.clip(qc - NA_WIN_COLS // 2, 0, GRID_W - NA_WIN_COLS)
    valid = (kc >= start) & (kc < start + NA_WIN_COLS)
    pad = jnp.pad(rpb, ((0, 0), (0, 0), (GRID_W, GRID_W)))
    shift = GRID_W + NA_WIN_COLS - 1
    toep = jnp.stack([pad[:, :, shift - c:shift - c + GRID_W] for c in range(GRID_W)], axis=2)
    full = jnp.where(jnp.asarray(valid)[None, None], toep * LOG2E, NEG_INF)
    t = jnp.stack([full[:, d:d + NA_WIN_ROWS] for d in range(NA_WIN_ROWS)], axis=1)
    t = t.reshape(NA_HEADS // 2, 2, NA_WIN_ROWS, NA_WIN_ROWS, GRID_W, GRID_W).transpose(0, 2, 1, 4, 3, 5)
    return t.reshape(NA_HEADS // 2, NA_WIN_ROWS, 2 * GRID_W, NA_WIN_ROWS * GRID_W).astype(F32)


def _na_kernel(q_ref, kp_ref, kc_ref, kn_ref, vp_ref, vc_ref, vn_ref, ck_ref, cv_ref, bias_ref, o_ref,
               wk_ref, ws_ref, wv_ref, sc_ref, pc_ref, ol_ref, li_ref, sl_ref, pl_ref):
    i = pl.program_id(2)
    last = pl.num_programs(2) - 1
    blk = NA_ROWS_PER_STEP * GRID_W
    for n, (kr, vr) in enumerate(((kp_ref, vp_ref), (kc_ref, vc_ref), (kn_ref, vn_ref))):
        wk_ref[:, n * blk:(n + 1) * blk] = kr[0]
        wv_ref[n * blk:(n + 1) * blk] = vr[0]
    ws_ref[:, 0:3 * blk - GRID_W] = wk_ref[:, GRID_W:3 * blk]

    def key_window(off):
        if off % 2 == 0:
            return wk_ref[:, off * GRID_W:(off + NA_WIN_ROWS) * GRID_W]
        return ws_ref[:, (off - 1) * GRID_W:(off - 1 + NA_WIN_ROWS) * GRID_W]
    first_head = lax.broadcasted_iota(jnp.int32, (GRID_W, 2 * NA_HEAD_DIM), 1) < NA_HEAD_DIM
    q = q_ref[0]
    zero = jnp.zeros((GRID_W, 2 * NA_HEAD_DIM), q.dtype)
    pieces = []
    for j in range(NA_ROWS_PER_STEP):
        qj = q[j * GRID_W:(j + 1) * GRID_W]
        pieces += [jnp.where(first_head, qj, zero), jnp.where(first_head, zero, qj)]
    qs = jnp.concatenate(pieces, axis=0)
    sc_ref[...] = lax.dot_general(qs, ck_ref[0], _NT, preferred_element_type=F32)
    half = NA_WIN_ROWS // 2
    rows = 2 * GRID_W

    def window_row(j):
        return jnp.where(i == 0, max(j + half, NA_ROWS_PER_STEP),
                         jnp.where(i == last, min(j + half, NA_ROWS_PER_STEP), j + half))

    for j in range(NA_ROWS_PER_STEP):
        kw = key_window(j + half)
        if j < half:
            kw = jnp.where(i == 0, key_window(NA_ROWS_PER_STEP), kw)
        elif j > half:
            kw = jnp.where(i == last, key_window(NA_ROWS_PER_STEP), kw)
        sl_ref[j] = jnp.dot(qs[j * rows:(j + 1) * rows], kw, preferred_element_type=F32)
    for j in range(NA_ROWS_PER_STEP):
        d = window_row(j) - j - 1
        for c0 in range(0, rows, NA_SOFTMAX_ROWS):
            rs = slice(c0, c0 + NA_SOFTMAX_ROWS)
            ra = slice(j * rows + c0, j * rows + c0 + NA_SOFTMAX_ROWS)
            s_loc = sl_ref[j, rs, :] + bias_ref[0, d, rs, :]
            s_ctx = sc_ref[ra, :]
            m = jnp.maximum(jnp.max(s_loc, axis=1, keepdims=True), jnp.max(s_ctx, axis=1, keepdims=True))
            p_loc = jnp.exp2(s_loc - m)
            p_ctx = jnp.exp2(s_ctx - m)
            l = jnp.sum(p_loc, axis=1, keepdims=True) + jnp.sum(p_ctx, axis=1, keepdims=True)
            pl_ref[j, rs, :] = p_loc.astype(BF16)
            pc_ref[ra, :] = p_ctx.astype(pc_ref.dtype)
            li_ref[ra, :] = jnp.broadcast_to(1.0 / l, (NA_SOFTMAX_ROWS, 2 * NA_HEAD_DIM))
    for j in range(NA_ROWS_PER_STEP):
        start = pl.multiple_of(window_row(j) * GRID_W, GRID_W)
        vw = wv_ref[pl.ds(start, NA_WIN_ROWS * GRID_W), :]
        ol_ref[j * rows:(j + 1) * rows, :] = jnp.dot(pl_ref[j], vw, preferred_element_type=F32)
    o = (ol_ref[...] + jnp.dot(pc_ref[...], cv_ref[0], preferred_element_type=F32)) * li_ref[...]
    for j in range(NA_ROWS_PER_STEP):
        r0 = j * rows
        oj = jnp.where(first_head, o[r0:r0 + GRID_W], o[r0 + GRID_W:r0 + rows])
        o_ref[0, j * GRID_W:(j + 1) * GRID_W, :] = oj.astype(o_ref.dtype)


def _na(q, k, v, ck, cv, bias):
    B, L, W = q.shape
    Lc = ck.shape[1]
    blk = NA_ROWS_PER_STEP * GRID_W
    nblk = L // blk
    assert NA_ROWS_PER_STEP == NA_WIN_ROWS and nblk >= 2 and GRID_W * 2 == LANES
    pw = 2 * NA_HEAD_DIM
    cur = pl.BlockSpec((1, blk, pw), lambda b, h, i: (b, i, h))
    prev = pl.BlockSpec((1, blk, pw), lambda b, h, i: (b, jnp.maximum(i - 1, 0), h))
    nxt = pl.BlockSpec((1, blk, pw), lambda b, h, i: (b, jnp.minimum(i + 1, nblk - 1), h))
    kcur = pl.BlockSpec((1, pw, blk), lambda b, h, i: (b, h, i))
    kprev = pl.BlockSpec((1, pw, blk), lambda b, h, i: (b, h, jnp.maximum(i - 1, 0)))
    knxt = pl.BlockSpec((1, pw, blk), lambda b, h, i: (b, h, jnp.minimum(i + 1, nblk - 1)))
    cspec = pl.BlockSpec((1, Lc, pw), lambda b, h, i: (b, 0, h))
    bspec = pl.BlockSpec((1,) + bias.shape[1:], lambda b, h, i: (h, 0, 0, 0))
    stacked = 2 * blk
    return pl.pallas_call(
        _na_kernel,
        out_shape=jax.ShapeDtypeStruct((B, L, W), BF16),
        grid=(B, W // pw, nblk),
        in_specs=[cur, kprev, kcur, knxt, prev, cur, nxt, cspec, cspec, bspec],
        out_specs=cur,
        scratch_shapes=[pltpu.VMEM((pw, 3 * blk), BF16), pltpu.VMEM((pw, 3 * blk), BF16),
                        pltpu.VMEM((3 * blk, pw), BF16),
                        pltpu.VMEM((stacked, Lc), F32), pltpu.VMEM((stacked, Lc), BF16),
                        pltpu.VMEM((stacked, pw), F32), pltpu.VMEM((stacked, pw), F32),
                        pltpu.VMEM((NA_ROWS_PER_STEP, 2 * GRID_W, NA_WIN_ROWS * GRID_W), F32),
                        pltpu.VMEM((NA_ROWS_PER_STEP, 2 * GRID_W, NA_WIN_ROWS * GRID_W), BF16)],
        compiler_params=pltpu.CompilerParams(vmem_limit_bytes=VMEM_LIMIT),
        name="nattn",
    )(q, k, k, k, v, v, v, ck, cv, bias)


def _ctx_attn_kernel(q_ref, k_ref, v_ref, o_ref):
    q = q_ref[0]
    k = k_ref[0]
    v = v_ref[0]
    first_head = lax.broadcasted_iota(jnp.int32, q.shape, 1) < NA_HEAD_DIM
    outs = []
    for h in range(2):
        qm = jnp.where(first_head if h == 0 else jnp.logical_not(first_head), q, jnp.zeros_like(q))
        s = lax.dot_general(qm, k, _NT, preferred_element_type=F32)
        p = jnp.exp2(s - jnp.max(s, axis=1, keepdims=True))
        o = jnp.dot(p.astype(BF16), v, preferred_element_type=F32)
        outs.append(o / jnp.sum(p, axis=1, keepdims=True))
    o_ref[0] = jnp.where(first_head, outs[0], outs[1]).astype(o_ref.dtype)


def _ctx_attn(q, k, v):
    B, Lc, W = q.shape
    pw = 2 * NA_HEAD_DIM
    spec = pl.BlockSpec((1, Lc, pw), lambda b, h: (b, 0, h))
    return pl.pallas_call(
        _ctx_attn_kernel,
        out_shape=jax.ShapeDtypeStruct((B, Lc, W), BF16),
        grid=(B, W // pw),
        in_specs=[spec, spec, spec],
        out_specs=spec,
        name="ctx_attn",
    )(q, k, v)


LRU_CHUNK = 1024


def _lru_gate_weights(wa, ba, wi, bi, lam):
    def bd(w):
        return jax.scipy.linalg.block_diag(*[w[n] for n in range(LRU_BLOCKS)])
    wg = jnp.stack([jnp.concatenate([bd(wa[d]), bd(wi[d])], axis=1) for d in range(2)]).astype(BF16)
    bg = jnp.stack([jnp.concatenate([ba[d], bi[d]])[None, :] for d in range(2)])
    return wg, bg, lam[:, None, :]


def _lru_coeffs(u, wg, bg, lam):
    C = u.shape[1]
    g = jnp.dot(u.astype(BF16), wg, preferred_element_type=F32) + bg
    sig = 0.5 + 0.5 * jnp.tanh(0.5 * g)
    r, ig = sig[:, :C], sig[:, C:]
    nl = -lam
    softplus = jnp.maximum(nl, 0.0) + jnp.log(1.0 + jnp.exp(-jnp.abs(nl)))
    log_a = (-LRU_C * softplus) * r
    a = jnp.exp(log_a)
    t = jnp.tanh(log_a)
    b = jnp.sqrt(-2.0 * t / (1.0 - t)) * (ig * u)
    return a, b


def _lru_scan(a, b, h0, reverse, ac_ref, bc_ref, h_ref):
    T, C = a.shape
    row = lax.broadcasted_iota(jnp.int32, a.shape, 0) % SUBLANES
    for s in (1, 2, 4):
        shift = T - s if reverse else s
        keep = (row < SUBLANES - s) if reverse else (row >= s)
        b = jnp.where(keep, a * pltpu.roll(b, shift, 0) + b, b)
        a = jnp.where(keep, a * pltpu.roll(a, shift, 0), a)
    ac_ref[...] = a
    bc_ref[...] = b
    ng = T // SUBLANES

    def group(g, h):
        r0 = pl.multiple_of((ng - 1 - g if reverse else g) * SUBLANES, SUBLANES)
        hr = ac_ref[pl.ds(r0, SUBLANES), :] * h + bc_ref[pl.ds(r0, SUBLANES), :]
        h_ref[pl.ds(r0, SUBLANES), :] = hr
        edge = hr[0:1] if reverse else hr[SUBLANES - 1:SUBLANES]
        return jnp.broadcast_to(edge, (SUBLANES, C))

    return lax.fori_loop(0, ng, group, h0, unroll=4)


def _gelu_tanh(x):
    return 0.5 * x * (1.0 + jnp.tanh(math.sqrt(2.0 / math.pi) * (x + 0.044715 * (x * x * x))))


def _lru_ctx_kernel(u_ref, xg_ref, wg_ref, bg_ref, lam_ref, hend_ref, yc_ref, ac_ref, bc_ref, h_ref):
    u = u_ref[0]
    C = u.shape[1]
    total = jnp.zeros_like(u)
    for d, rev in enumerate((False, True)):
        a, b = _lru_coeffs(u, wg_ref[d], bg_ref[d], lam_ref[d])
        hl = _lru_scan(a, b, jnp.zeros((SUBLANES, C), F32), rev, ac_ref, bc_ref, h_ref)
        hend_ref[0, d:d + 1, :] = hl[0:1]
        total = total + h_ref[...]
    yc_ref[0] = (total * _gelu_tanh(xg_ref[0])).astype(yc_ref.dtype)


def _lru_ctx(u, xg, wg, bg, lam):
    B, Lc, C = u.shape
    tok = pl.BlockSpec((1, Lc, C), lambda b: (b, 0, 0))
    full = lambda a: pl.BlockSpec(a.shape, lambda b: (0,) * a.ndim)
    return pl.pallas_call(
        _lru_ctx_kernel,
        out_shape=[jax.ShapeDtypeStruct((B, 2, C), F32), jax.ShapeDtypeStruct((B, Lc, C), BF16)],
        grid=(B,),
        in_specs=[tok, tok, full(wg), full(bg), full(lam)],
        out_specs=[pl.BlockSpec((1, 2, C), lambda b: (b, 0, 0)), tok],
        scratch_shapes=[pltpu.VMEM((Lc, C), F32)] * 3,
        name="lru_ctx",
    )(u, xg, wg, bg, lam)


def _lru_dir_kernel(*refs, d, reverse):
    if reverse:
        u_ref, hend_ref, wg_ref, bg_ref, lam_ref, hf_ref, xg_ref, o_ref, ac_ref, bc_ref, h_ref, carry_ref = refs
    else:
        u_ref, hend_ref, wg_ref, bg_ref, lam_ref, o_ref, ac_ref, bc_ref, carry_ref = refs
        h_ref = o_ref.at[0]
    C = u_ref.shape[2]

    @pl.when(pl.program_id(1) == 0)
    def _():
        carry_ref[...] = jnp.broadcast_to(hend_ref[0, d:d + 1, :], (SUBLANES, C))

    a, b = _lru_coeffs(u_ref[0], wg_ref[d], bg_ref[d], lam_ref[d])
    carry_ref[...] = _lru_scan(a, b, carry_ref[...], reverse, ac_ref, bc_ref, h_ref)
    if reverse:
        o_ref[0] = ((hf_ref[0] + h_ref[...]) * _gelu_tanh(xg_ref[0])).astype(o_ref.dtype)


def _lru_dir(u, hend, wg, bg, lam, hf=None, xg=None):
    B, L, C = u.shape
    reverse = hf is not None
    T = LRU_CHUNK
    nb = L // T
    tok = pl.BlockSpec((1, T, C), (lambda b, i: (b, nb - 1 - i, 0)) if reverse else (lambda b, i: (b, i, 0)))
    full = lambda a: pl.BlockSpec(a.shape, lambda b, i: (0,) * a.ndim)
    ins = [u, hend, wg, bg, lam] + ([hf, xg] if reverse else [])
    specs = [tok, pl.BlockSpec((1, 2, C), lambda b, i: (b, 0, 0)), full(wg), full(bg), full(lam)]
    specs += [tok, tok] if reverse else []
    scratch = [pltpu.VMEM((T, C), F32)] * (3 if reverse else 2) + [pltpu.VMEM((SUBLANES, C), F32)]
    return pl.pallas_call(
        functools.partial(_lru_dir_kernel, d=int(reverse), reverse=reverse),
        out_shape=jax.ShapeDtypeStruct((B, L, C), BF16 if reverse else F32),
        grid=(B, nb),
        in_specs=specs,
        out_specs=tok,
        scratch_shapes=scratch,
        compiler_params=pltpu.CompilerParams(dimension_semantics=("arbitrary", "arbitrary")),
        name="lru_bwd" if reverse else "lru_fwd",
    )(*ins)


def _lru(u, xg, u_c, xg_c, wa, ba, wi, bi, lam):
    wg, bg, lam3 = _lru_gate_weights(wa, ba, wi, bi, lam)
    hend, yc = _lru_ctx(u_c, xg_c, wg, bg, lam3)
    hf = _lru_dir(u, hend, wg, bg, lam3)
    return _lru_dir(u, hend, wg, bg, lam3, hf, xg), yc


def kernel(x, c, ctx, c_ctx, ada_w, ada_b, g_mix_pre, g_mix_post, g_ffn_pre, g_ffn_post, w_in, w_out, hy_conv_w,
           hy_conv_b, hy_f_w1, hy_f_b1, hy_f_w2, hy_f_b2, hy_f_w3, hy_f_freq, hy_bias, na_rpb, lru_conv_w,
           lru_conv_b, lru_wa, lru_ba, lru_wi, lru_bi, lru_lam, ffn_w_gu, ffn_w_down):
    B, L, D = x.shape
    Lc = ctx.shape[1]
    tm = 1024

    assert B + 1 <= SUBLANES
    cond_t = jnp.zeros((D, SUBLANES), F32).at[:, 0:B].set(c.T).at[:, B].set(c_ctx)
    mods = _modulation(cond_t, B + 1, ada_w, ada_b)

    xc = ctx
    for l in range(DEPTH):
        with_ctx_out = l < DEPTH - 1
        m = mods[l].reshape(8, 6, D)
        lat = [m[0:B, j][:, None, :] for j in range(6)]
        cx = [jnp.broadcast_to(m[B, j][None, None, :], (B, 1, D)) for j in range(6)]
        row = lambda a: a.reshape(1, -1)

        w_in_bf = w_in[l].astype(BF16)
        w_out_bf = w_out[l].astype(BF16)
        wgu = ffn_w_gu[l].astype(BF16)
        wd = ffn_w_down[l].astype(BF16)

        conv_args = (hy_conv_w[l], row(hy_conv_b[l]), lru_conv_w[l], row(lru_conv_b[l]))
        hv, hx1, hx2, q, k, v, lu, lg = _inproj(x, lat[1], lat[0], row(g_mix_pre[l]), w_in_bf, *conv_args, tm=tm,
                                                chan_major=True)
        cv, cx1, cx2, cq, ck, cvv, clu, clg = _inproj(xc, cx[1], cx[0], row(g_mix_pre[l]), w_in_bf, *conv_args,
                                                      tm=Lc, chan_major=False)

        filt = (hy_f_w1[l], hy_f_b1[l], hy_f_w2[l], hy_f_b2[l], hy_f_w3[l], hy_f_freq[l])
        y_hy = _hyena(hv, hx1, hx2, filt, hy_bias[l])
        y_na = _na(q, k, v, ck, cvv, _na_bias_table(na_rpb[l]))
        y_lru, yc_lru = _lru(lu, lg, clu, clg, lru_wa[l], lru_ba[l], lru_wi[l], lru_bi[l], lru_lam[l])

        ffn_args = (row(g_ffn_pre[l]), row(g_ffn_post[l]), wgu, wd)
        x = _out_ffn(y_hy, y_na, y_lru, w_out_bf, x, row(g_mix_post[l]), lat[2], lat[4], lat[3], lat[5], *ffn_args,
                     tm=tm, chan_major=True)

        if with_ctx_out:
            yc_hy = _hyena_ctx(cv, cx1, cx2, filt, hy_bias[l])
            yc_na = _ctx_attn(cq, ck, cvv)
            xc = _out_ffn(yc_hy, yc_na, yc_lru, w_out_bf, xc, row(g_mix_post[l]), cx[2], cx[4], cx[3], cx[5],
                          *ffn_args, tm=Lc, chan_major=False)
    return x
```

```python
import functools
import math

import jax
import jax.numpy as jnp
import numpy as np
from jax import lax
from jax.experimental import pallas as pl
from jax.experimental.pallas import tpu as pltpu

F32 = jnp.float32
BF16 = jnp.bfloat16

D_MODEL = 1024
DEPTH = 2
GRID_W = 64
HY_WIDTH = D_MODEL // 4
NA_HEAD_DIM = 64
NA_WIDTH = D_MODEL // 2
NA_HEADS = NA_WIDTH // NA_HEAD_DIM
LRU_WIDTH = D_MODEL // 4
HY_ORDER = 2
HY_BANDS = 16
HY_FAST_DECAY = 0.3
HY_SLOW_DECAY = 1.5
HY_DECAY_TARGET = 1e-2
NA_WIN_ROWS = 8
NA_WIN_COLS = 16
LRU_C = 8.0
D_FF = -(-8 * D_MODEL // (3 * 256)) * 256
RMS_EPS = 1e-6
NEG_INF = -1e30
LOG2E = math.log2(math.e)

_HY_END = 3 * HY_WIDTH
_NA_END = _HY_END + 3 * NA_WIDTH
_LRU_MID = _NA_END + LRU_WIDTH

HALO = 8
SUBLANES = 8
VMEM_LIMIT = 56 * 1024 * 1024


def _rms(x, g):
    return x * lax.rsqrt(jnp.mean(x * x, axis=-1, keepdims=True) + RMS_EPS) * g


def _mod_kernel(ct_ref, w_ref, b_ref, o_ref, *, n_cond):
    ct = ct_ref[...]
    st = ct * jax.nn.sigmoid(ct)
    w = w_ref[0]
    rows = [jnp.sum(w * st[:, r:r + 1], axis=0, keepdims=True) for r in range(n_cond)]
    rows.append(jnp.zeros((SUBLANES - n_cond, w.shape[1]), F32))
    o_ref[0] = jnp.concatenate(rows, axis=0) + b_ref[0]


def _modulation(cond_t, n_cond, ada_w, ada_b):
    tn = 768
    n = ada_w.shape[-1]
    return pl.pallas_call(
        functools.partial(_mod_kernel, n_cond=n_cond),
        out_shape=jax.ShapeDtypeStruct((DEPTH, SUBLANES, n), F32),
        grid=(DEPTH, n // tn),
        in_specs=[pl.BlockSpec((D_MODEL, SUBLANES), lambda l, j: (0, 0)),
                  pl.BlockSpec((1, D_MODEL, tn), lambda l, j: (l, 0, j)),
                  pl.BlockSpec((1, 1, tn), lambda l, j: (l, 0, j))],
        out_specs=pl.BlockSpec((1, SUBLANES, tn), lambda l, j: (l, 0, j)),
        compiler_params=pltpu.CompilerParams(vmem_limit_bytes=VMEM_LIMIT),
        name="adaln_modulation",
    )(cond_t, ada_w, ada_b.reshape(DEPTH, 1, n))


CM_LANES = 256
LANES = 128
CM_HALVES = CM_LANES // LANES


def _chan_major_shape(B, L, C):
    return (B, C // SUBLANES, CM_HALVES, L // CM_LANES * SUBLANES, LANES)


def _chan_major_spec(tm, C):
    return pl.BlockSpec((1, C // SUBLANES, CM_HALVES, tm // CM_LANES * SUBLANES, LANES),
                        lambda b, i: (b, 0, 0, i, 0))


def _store_chan_major(ref, u):
    ut = u.T
    for g in range(u.shape[1] // SUBLANES):
        for j in range(u.shape[0] // CM_LANES):
            for h in range(CM_HALVES):
                lane0 = j * CM_LANES + h * LANES
                ref[0, g, h, j * SUBLANES:(j + 1) * SUBLANES, :] = ut[g * SUBLANES:(g + 1) * SUBLANES,
                                                                      lane0:lane0 + LANES]


def _load_chan_major(ref):
    _, ng, _, nr, _ = ref.shape
    rows = [jnp.concatenate([ref[0, g, h, j * SUBLANES:(j + 1) * SUBLANES, :]
                             for j in range(nr // SUBLANES) for h in range(CM_HALVES)], axis=1) for g in range(ng)]
    return jnp.concatenate(rows, axis=0).T


def _chan_rows(ref, lead, c, n):
    return jnp.concatenate([ref.at[lead + (h,)][pl.ds(c, n, stride=SUBLANES), :] for h in range(CM_HALVES)], axis=1)


def _inproj_kernel(xp_ref, xc_ref, xn_ref, sc_ref, sh_ref, g_ref, w_ref, hcw_ref, hcb_ref, lcw_ref, lcb_ref,
                   hv_ref, hx1_ref, hx2_ref, q_ref, k_ref, v_ref, lu_ref, lg_ref, pe_ref, *, tm, chan_major):
    i = pl.program_id(1)
    last = pl.num_programs(1) - 1
    g = g_ref[...]
    sc1 = 1.0 + sc_ref[0]
    sh = sh_ref[0]

    def norm_mod(xv):
        return _rms(xv, g) * sc1 + sh

    hp = norm_mod(xp_ref[0]) * (i > 0).astype(F32)
    hn = norm_mod(xn_ref[0]) * (i < last).astype(F32)
    he = jnp.concatenate([hp, norm_mod(xc_ref[0]), hn], axis=0).astype(BF16)

    pe_ref[:, 0:_HY_END] = jnp.dot(he, w_ref[:, 0:_HY_END], preferred_element_type=F32)
    pe_ref[:, _HY_END:] = jnp.dot(he, w_ref[:, _NA_END:_LRU_MID], preferred_element_type=F32)
    hc = he[HALO:HALO + tm]
    qkv = jnp.dot(hc, w_ref[:, _HY_END:_NA_END], preferred_element_type=F32)
    q_ref[0] = (qkv[:, 0:NA_WIDTH] * (NA_HEAD_DIM ** -0.5 * LOG2E)).astype(BF16)
    kk = qkv[:, NA_WIDTH:2 * NA_WIDTH]
    k_ref[0] = (kk.T if chan_major else kk).astype(BF16)
    v_ref[0] = qkv[:, 2 * NA_WIDTH:].astype(BF16)
    lg_ref[0] = jnp.dot(hc, w_ref[:, _LRU_MID:], preferred_element_type=F32)

    u = hcb_ref[...]
    for kk in range(3):
        u = u + hcw_ref[kk:kk + 1, :] * pe_ref[pl.ds(HALO - 1 + kk, tm), 0:_HY_END]
    for n, ref in enumerate((hv_ref, hx1_ref, hx2_ref)):
        un = u[:, n * HY_WIDTH:(n + 1) * HY_WIDTH]
        if chan_major:
            _store_chan_major(ref, un)
        else:
            ref[0] = un
    ul = lcb_ref[...]
    for kk in range(4):
        ul = ul + lcw_ref[kk:kk + 1, :] * pe_ref[pl.ds(HALO - 2 + kk, tm), _HY_END:]
    lu_ref[0] = ul


def _inproj(x, sc, sh, g, w_bf, hcw, hcb, lcw, lcb, *, tm, chan_major):
    B, L, D = x.shape
    nb = tm // HALO
    nh = L // HALO
    tok = lambda w, dt: jax.ShapeDtypeStruct((B, L, w), dt)
    tspec = lambda w: pl.BlockSpec((1, tm, w), lambda b, i: (b, i, 0))
    full = lambda a: pl.BlockSpec(a.shape, lambda b, i: (0,) * a.ndim)
    vec = pl.BlockSpec((1, 1, D), lambda b, i: (b, 0, 0))
    if chan_major:
        hy_shape = jax.ShapeDtypeStruct(_chan_major_shape(B, L, HY_WIDTH), F32)
        hy_spec = _chan_major_spec(tm, HY_WIDTH)
        k_shape = jax.ShapeDtypeStruct((B, NA_WIDTH, L), BF16)
        k_spec = pl.BlockSpec((1, NA_WIDTH, tm), lambda b, i: (b, 0, i))
    else:
        hy_shape, hy_spec = tok(HY_WIDTH, F32), tspec(HY_WIDTH)
        k_shape, k_spec = tok(NA_WIDTH, BF16), tspec(NA_WIDTH)
    return pl.pallas_call(
        functools.partial(_inproj_kernel, tm=tm, chan_major=chan_major),
        out_shape=[hy_shape] * 3 + [tok(NA_WIDTH, BF16), k_shape, tok(NA_WIDTH, BF16)] + [tok(LRU_WIDTH, F32)] * 2,
        grid=(B, L // tm),
        in_specs=[pl.BlockSpec((1, HALO, D), lambda b, i: (b, jnp.maximum(i * nb - 1, 0), 0)),
                  pl.BlockSpec((1, tm, D), lambda b, i: (b, i, 0)),
                  pl.BlockSpec((1, HALO, D), lambda b, i: (b, jnp.minimum((i + 1) * nb, nh - 1), 0)),
                  vec, vec, full(g), pl.BlockSpec(w_bf.shape, lambda b, i: (0, 0), pipeline_mode=pl.Buffered(1)),
                  full(hcw), full(hcb), full(lcw), full(lcb)],
        out_specs=[hy_spec] * 3 + [tspec(NA_WIDTH), k_spec, tspec(NA_WIDTH)] + [tspec(LRU_WIDTH)] * 2,
        scratch_shapes=[pltpu.VMEM((tm + 2 * HALO, _HY_END + LRU_WIDTH), F32)],
        compiler_params=pltpu.CompilerParams(vmem_limit_bytes=VMEM_LIMIT),
        name="inproj",
    )(x, x, x, sc, sh, g, w_bf, hcw, hcb, lcw, lcb)


FF_CHUNK = 256


def _out_ffn_kernel(yh_ref, yn_ref, yl_ref, wo_ref, x_ref, gm_ref, gtm_ref, sc_ref, sh_ref, gt_ref, gpre_ref,
                    gpost_ref, wgu_ref, wd_ref, o_ref, x1_ref, h_ref, *, chan_major, halves):
    yh = (_load_chan_major(yh_ref) if chan_major else yh_ref[0]).astype(BF16)
    tm = x_ref.shape[1]
    for r0 in range(0, tm, tm // halves):
        rs = slice(r0, r0 + tm // halves)
        y = jnp.dot(yh[rs], wo_ref[0:HY_WIDTH], preferred_element_type=F32)
        y = y + jnp.dot(yn_ref[0, rs, :], wo_ref[HY_WIDTH:HY_WIDTH + NA_WIDTH], preferred_element_type=F32)
        y = y + jnp.dot(yl_ref[0, rs, :], wo_ref[HY_WIDTH + NA_WIDTH:], preferred_element_type=F32)
        x = x_ref[0, rs, :] + gtm_ref[0] * _rms(y, gm_ref[...])
        x1_ref[rs, :] = x
        h_ref[rs, :] = (_rms(x, gpre_ref[...]) * (1.0 + sc_ref[0]) + sh_ref[0]).astype(BF16)
    for r0 in range(0, tm, tm // halves):
        rs = slice(r0, r0 + tm // halves)
        h = h_ref[rs, :]
        acc = jnp.zeros((tm // halves, x_ref.shape[2]), F32)
        for c0 in range(0, D_FF, FF_CHUNK):
            g = jnp.dot(h, wgu_ref[:, c0:c0 + FF_CHUNK], preferred_element_type=F32)
            u = jnp.dot(h, wgu_ref[:, D_FF + c0:D_FF + c0 + FF_CHUNK], preferred_element_type=F32)
            a = (g * jax.nn.sigmoid(g) * u).astype(BF16)
            acc = acc + jnp.dot(a, wd_ref[c0:c0 + FF_CHUNK, :], preferred_element_type=F32)
        o_ref[0, rs, :] = x1_ref[rs, :] + gt_ref[0] * _rms(acc, gpost_ref[...])


def _out_ffn(yh, yn, yl, wo, x, gm, gtm, sc, sh, gt, gpre, gpost, wgu, wd, *, tm, chan_major):
    B, L, D = x.shape
    tspec = lambda w: pl.BlockSpec((1, tm, w), lambda b, i: (b, i, 0))
    vec = pl.BlockSpec((1, 1, D), lambda b, i: (b, 0, 0))
    full = lambda a: pl.BlockSpec(a.shape, lambda b, i: (0,) * a.ndim)
    res = lambda a: pl.BlockSpec(a.shape, lambda b, i: (0,) * a.ndim, pipeline_mode=pl.Buffered(1))
    hy_spec = _chan_major_spec(tm, HY_WIDTH) if chan_major else tspec(HY_WIDTH)
    return pl.pallas_call(
        functools.partial(_out_ffn_kernel, chan_major=chan_major, halves=2 if tm >= 1024 else 1),
        out_shape=jax.ShapeDtypeStruct((B, L, D), F32),
        grid=(B, L // tm),
        in_specs=[hy_spec, tspec(NA_WIDTH), tspec(LRU_WIDTH), res(wo), tspec(D), full(gm), vec,
                  vec, vec, vec, full(gpre), full(gpost), res(wgu), res(wd)],
        out_specs=tspec(D),
        scratch_shapes=[pltpu.VMEM((tm, D), F32), pltpu.VMEM((tm, D), BF16)],
        compiler_params=pltpu.CompilerParams(vmem_limit_bytes=VMEM_LIMIT),
        name="out_ffn",
    )(yh, yn, yl, wo, x, gm, gtm, sc, sh, gt, gpre, gpost, wgu, wd)


HY_N1 = CM_LANES
HY_N2 = 128
HY_FILT_COLS = 2048


def _block_cplx(re, im):
    return np.block([[re, -im], [im, re]])


@functools.lru_cache(maxsize=None)
def _dft_consts():
    n = HY_N1 * HY_N2
    h = HY_N2 // 2
    i1 = np.arange(HY_N1, dtype=np.float64)
    i2 = np.arange(HY_N2, dtype=np.float64)
    a1 = 2.0 * np.pi * np.outer(i1, i1) / HY_N1
    a2 = 2.0 * np.pi * np.outer(i2, i2) / HY_N2
    at = 2.0 * np.pi * np.outer(i2, i1) / n
    r2, m2 = np.cos(a2), -np.sin(a2)
    r1, m1 = np.cos(a1), -np.sin(a1)
    f32 = lambda m: np.asarray(m, np.float32)
    return dict(
        rows_data=f32(_block_cplx(r2[:, :h], m2[:, :h])),
        rows_filt=f32(np.concatenate([r2, m2], axis=0)),
        rows_out=f32(_block_cplx(r2[:h, :], m2[:h, :])),
        lanes=f32(np.block([[r1, m1], [-m1, r1]])),
        twr=f32(np.cos(at)), twi=f32(-np.sin(at)))


def _dft_const(name):
    return jnp.asarray(_dft_consts()[name], F32)


def _cmul(ar, ai, br, bi):
    return ar * br - ai * bi, ar * bi + ai * br


def _hy_conv_kernel(v_ref, x1_ref, x2_ref, k0_ref, k1_ref, ks0_ref, ks1_ref, bias_ref, rd_ref, rf_ref, ro_ref,
                    ln_ref, twr_ref, twi_ref, o_ref, st_ref, ks_ref):
    g = pl.program_id(0)
    n_ch = SUBLANES
    h = HY_N2 // 2
    rows_data = rd_ref[...].astype(BF16)
    rows_filt = rf_ref[...].astype(BF16)
    rows_out = ro_ref[...].astype(BF16)
    lanes = ln_ref[...].astype(BF16)
    twr, twi = twr_ref[...], twi_ref[...]

    def rows_then_twiddle(mat, x, c):
        a = jnp.dot(mat, x.astype(BF16), preferred_element_type=F32)
        ar, ai = _cmul(a[:HY_N2], a[HY_N2:], twr, twi)
        st_ref[c * HY_N2:(c + 1) * HY_N2, 0:HY_N1] = ar.astype(BF16)
        st_ref[c * HY_N2:(c + 1) * HY_N2, HY_N1:] = ai.astype(BF16)

    for o, (k_ref, s_ref) in enumerate(((k0_ref, ks0_ref), (k1_ref, ks1_ref))):
        for c in range(n_ch):
            rows_then_twiddle(rows_filt, _chan_rows(k_ref, (0,), c, HY_N2), c)
        spec = jnp.dot(st_ref[...], lanes, preferred_element_type=F32)
        inv = 1.0 / (jnp.sum(s_ref[...], axis=1, keepdims=True) * float(HY_N1 * HY_N2))
        for c in range(n_ch):
            ks_ref[o, c * HY_N2:(c + 1) * HY_N2, :] = spec[c * HY_N2:(c + 1) * HY_N2] * inv[c:c + 1, :]

    z = [[_chan_rows(v_ref, (b, 0), c, h) for c in range(n_ch)] for b in range(2)]
    for o, gate_ref in enumerate((x1_ref, x2_ref)):
        for c in range(n_ch):
            rows_then_twiddle(rows_data, jnp.concatenate([z[0][c], z[1][c]], axis=0), c)
        x = jnp.dot(st_ref[...], lanes, preferred_element_type=F32)
        pr, pi = _cmul(x[:, :HY_N1], x[:, HY_N1:], ks_ref[o, :, 0:HY_N1], ks_ref[o, :, HY_N1:])
        st_ref[:, 0:HY_N1] = pr.astype(BF16)
        st_ref[:, HY_N1:] = (-pi).astype(BF16)
        y = jnp.dot(st_ref[...], lanes, preferred_element_type=F32)
        for c in range(n_ch):
            yc = y[c * HY_N2:(c + 1) * HY_N2]
            yr, yi = _cmul(yc[:, :HY_N1], yc[:, HY_N1:], twr, twi)
            out = jnp.dot(rows_out, jnp.concatenate([yr, yi], axis=0).astype(BF16),
                          preferred_element_type=F32)
            bias = bias_ref[o, g * n_ch + c]
            for b, conv in ((0, out[:h]), (1, -out[h:])):
                z[b][c] = _chan_rows(gate_ref, (b, 0), c, h) * (conv + bias * z[b][c])
    for b in range(2):
        for c in range(n_ch):
            for hh in range(CM_HALVES):
                o_ref.at[b, 0, hh][pl.ds(c, h, stride=SUBLANES), :] = z[b][c][:, hh * LANES:(hh + 1) * LANES]


def _hy_conv(v, x1, x2, kc, ksum, hy_bias):
    B, ng = v.shape[0], v.shape[1]
    assert B == 2 and v.shape[3] * 2 == HY_N2 * SUBLANES and kc.shape[2] == HY_N2 * SUBLANES
    consts = [_dft_const(n) for n in ("rows_data", "rows_filt", "rows_out", "lanes", "twr", "twi")]
    tok = pl.BlockSpec((B, 1) + v.shape[2:], lambda g: (0, g, 0, 0, 0))
    kspec = lambda o: pl.BlockSpec((1,) + kc.shape[1:], lambda g: (o * ng + g, 0, 0, 0))
    sspec = lambda o: pl.BlockSpec((SUBLANES, ksum.shape[1]), lambda g: (o * ng + g, 0))
    full = lambda a: pl.BlockSpec(a.shape, lambda g: (0,) * a.ndim)
    return pl.pallas_call(
        _hy_conv_kernel,
        out_shape=jax.ShapeDtypeStruct(v.shape, F32),
        grid=(ng,),
        in_specs=[tok, tok, tok, kspec(0), kspec(1), sspec(0), sspec(1),
                  pl.BlockSpec(memory_space=pltpu.SMEM)] + [full(a) for a in consts],
        out_specs=tok,
        scratch_shapes=[pltpu.VMEM((SUBLANES * HY_N2, 2 * HY_N1), BF16),
                        pltpu.VMEM((HY_ORDER, SUBLANES * HY_N2, 2 * HY_N1), F32)],
        compiler_params=pltpu.CompilerParams(vmem_limit_bytes=VMEM_LIMIT),
        name="hy_conv",
    )(v, x1, x2, kc, kc, ksum, ksum, hy_bias, *consts)


def _hy_filter_kernel(w1t_ref, w1c_ref, w1s_ref, b1_ref, f0_ref, w2_ref, b2_ref, f1_ref, w3_ref, dl_ref,
                      k_ref, s_ref, *, L, P, chan_major):
    i = pl.program_id(0)

    def pos_t(rows):
        n = i * P + lax.broadcasted_iota(jnp.int32, (rows, P), 1)
        return n, jnp.where(n < L, n, 2 * L - n).astype(F32) / L

    _, tb = pos_t(HY_BANDS)
    band = (lax.broadcasted_iota(jnp.int32, tb.shape, 0) + 1).astype(F32)
    ang = 2.0 * math.pi * tb * band
    _, t1 = pos_t(1)
    h = w1t_ref[...] * t1 + b1_ref[...]
    h = h + jnp.dot(w1c_ref[...], jnp.cos(ang).astype(BF16), preferred_element_type=F32)
    h = h + jnp.dot(w1s_ref[...], jnp.sin(ang).astype(BF16), preferred_element_type=F32)
    h = jnp.sin(f0_ref[...] * h)
    h = jnp.sin(f1_ref[...] * (jnp.dot(w2_ref[...], h.astype(BF16), preferred_element_type=F32) + b2_ref[...]))
    k = jnp.dot(w3_ref[0], h.astype(BF16), preferred_element_type=F32)
    n, t = pos_t(1)
    k = jnp.where(n == L, 0.0, k * jnp.exp(-t * dl_ref[...]))
    if chan_major:
        for j in range(P // CM_LANES):
            for hh in range(CM_HALVES):
                lane0 = j * CM_LANES + hh * LANES
                k_ref[:, hh, j * SUBLANES:(j + 1) * SUBLANES, :] = k[:, lane0:lane0 + LANES].reshape(
                    k_ref.shape[0], SUBLANES, LANES)
    else:
        k_ref[...] = k
    ka = jnp.abs(k)
    tot = ka[:, 0:LANES]
    for j in range(1, P // LANES):
        tot = tot + ka[:, j * LANES:(j + 1) * LANES]

    @pl.when(i == 0)
    def _():
        s_ref[...] = jnp.zeros_like(s_ref)

    s_ref[...] += tot


def _hy_filter(L, P, chan_major, f_w1, f_b1, f_w2, f_b2, f_w3, f_freq):
    nb = 2 * L // P
    assert nb % 2 == 0 and P % CM_LANES == 0
    hid = f_w2.shape[0]
    oc = HY_ORDER * HY_WIDTH
    col = lambda v: v.reshape(-1, 1)
    w3 = f_w3.reshape(hid, HY_ORDER, 2, HY_WIDTH)
    w3 = jnp.stack([w3[:, :, d].reshape(hid, oc).T for d in range(2)]).astype(BF16)
    deltas = np.abs(np.linspace(math.log(HY_DECAY_TARGET) / HY_FAST_DECAY,
                                math.log(HY_DECAY_TARGET) / HY_SLOW_DECAY, HY_WIDTH, dtype=np.float32))
    ins = [col(f_w1[0]), f_w1[1:1 + HY_BANDS].T.astype(BF16), f_w1[1 + HY_BANDS:].T.astype(BF16), col(f_b1),
           col(f_freq[0]), f_w2.T.astype(BF16), col(f_b2), col(f_freq[1]), w3,
           jnp.asarray(np.tile(deltas, HY_ORDER)[:, None])]
    full = lambda a: pl.BlockSpec(a.shape, lambda i: (0,) * a.ndim)
    specs = [full(a) for a in ins]
    specs[8] = pl.BlockSpec((1,) + w3.shape[1:], lambda i: (i // (nb // 2), 0, 0))
    if chan_major:
        kshape = (oc // SUBLANES, CM_HALVES, 2 * L // CM_LANES * SUBLANES, LANES)
        kspec = pl.BlockSpec((oc // SUBLANES, CM_HALVES, P // CM_LANES * SUBLANES, LANES), lambda i: (0, 0, i, 0))
    else:
        kshape = (oc, 2 * L)
        kspec = pl.BlockSpec((oc, P), lambda i: (0, i))
    return pl.pallas_call(
        functools.partial(_hy_filter_kernel, L=L, P=P, chan_major=chan_major),
        out_shape=[jax.ShapeDtypeStruct(kshape, F32), jax.ShapeDtypeStruct((oc, LANES), F32)],
        grid=(nb,),
        in_specs=specs,
        out_specs=[kspec, pl.BlockSpec((oc, LANES), lambda i: (0, 0))],
        compiler_params=pltpu.CompilerParams(dimension_semantics=("arbitrary",), vmem_limit_bytes=VMEM_LIMIT),
        name="hy_filter",
    )(*ins)


def _hyena(v, x1, x2, filt, hy_bias):
    L = v.shape[3] // SUBLANES * CM_LANES
    assert 2 * L == HY_N1 * HY_N2
    kc, ksum = _hy_filter(L, HY_FILT_COLS, True, *filt)
    return _hy_conv(v, x1, x2, kc, ksum, hy_bias)


@functools.lru_cache(maxsize=None)
def _ctx_dft_consts(n):
    i = np.arange(n, dtype=np.float64)
    a = 2.0 * np.pi * np.outer(i, i) / n
    f = np.concatenate([np.cos(a), -np.sin(a)], axis=0)
    return np.asarray(f, np.float32), np.asarray(f.T[:n // 2], np.float32)


def _hyena_ctx_kernel(v_ref, x1_ref, x2_ref, kc_ref, ks_ref, bias_ref, ff_ref, fi_ref, o_ref):
    n = ff_ref.shape[1]
    lc = n // 2
    C = v_ref.shape[2]
    ff = ff_ref[...].astype(BF16)
    inv = 1.0 / (jnp.sum(ks_ref[...], axis=1, keepdims=True) * float(n))
    kn = (kc_ref[...] * inv).T
    kspec = jnp.dot(ff, kn.astype(BF16), preferred_element_type=F32)
    z = jnp.concatenate([v_ref[0], v_ref[1]], axis=1)
    for o, g_ref in enumerate((x1_ref, x2_ref)):
        x = jnp.dot(ff[:, :lc], z.astype(BF16), preferred_element_type=F32)
        k = kspec[:, o * C:(o + 1) * C]
        kr = jnp.concatenate([k[:n], k[:n]], axis=1)
        ki = jnp.concatenate([k[n:], k[n:]], axis=1)
        yr, yi = _cmul(x[:n], x[n:], kr, ki)
        y = jnp.concatenate([yr, yi], axis=0)
        conv = jnp.dot(fi_ref[...].astype(BF16), y.astype(BF16), preferred_element_type=F32)
        bias = jnp.concatenate([bias_ref[o:o + 1, :], bias_ref[o:o + 1, :]], axis=1)
        z = jnp.concatenate([g_ref[0], g_ref[1]], axis=1) * (conv + bias * z)
    o_ref[0] = z[:, :C]
    o_ref[1] = z[:, C:]


def _hyena_ctx(v, x1, x2, filt, hy_bias):
    B, Lc, C = v.shape
    assert B == 2
    kc, ksum = _hy_filter(Lc, Lc, False, *filt)
    ff, fi = (jnp.asarray(m, F32) for m in _ctx_dft_consts(2 * Lc))
    return pl.pallas_call(
        _hyena_ctx_kernel,
        out_shape=jax.ShapeDtypeStruct((B, Lc, C), F32),
        compiler_params=pltpu.CompilerParams(vmem_limit_bytes=VMEM_LIMIT),
        name="hyena_ctx",
    )(v, x1, x2, kc, ksum, hy_bias, ff, fi)


NA_ROWS_PER_STEP = 8
NA_SOFTMAX_ROWS = 32
_NT = (((1,), (1,)), ((), ()))


def _na_bias_table(rpb):
    qc = np.arange(GRID_W)[:, None]
    kc = np.arange(GRID_W)[None, :]
    start = np.clip(qc - NA_WIN_COLS // 2, 0, GRID_W - NA_WIN_COLS)
    valid = (kc >= start) & (kc < start + NA_WIN_COLS)
    pad = jnp.pad(rpb, ((0, 0), (0, 0), (GRID_W, GRID_W)))
    shift = GRID_W + NA_WIN_COLS - 1
    toep = jnp.stack([pad[:, :, shift - c:shift - c + GRID_W] for c in range(GRID_W)], axis=2)
    full = jnp.where(jnp.asarray(valid)[None, None], toep * LOG2E, NEG_INF)
    t = jnp.stack([full[:, d:d + NA_WIN_ROWS] for d in range(NA_WIN_ROWS)], axis=1)
    t = t.reshape(NA_HEADS // 2, 2, NA_WIN_ROWS, NA_WIN_ROWS, GRID_W, GRID_W).transpose(0, 2, 1, 4, 3, 5)
    return t.reshape(NA_HEADS // 2, NA_WIN_ROWS, 2 * GRID_W, NA_WIN_ROWS * GRID_W).astype(F32)


def _na_kernel(q_ref, kp_ref, kc_ref, kn_ref, vp_ref, vc_ref, vn_ref, ck_ref, cv_ref, bias_ref, o_ref,
               wk_ref, ws_ref, wv_ref, sc_ref, pc_ref, ol_ref, li_ref, sl_ref, pl_ref):
    i = pl.program_id(2)
    last = pl.num_programs(2) - 1
    blk = NA_ROWS_PER_STEP * GRID_W
    for n, (kr, vr) in enumerate(((kp_ref, vp_ref), (kc_ref, vc_ref), (kn_ref, vn_ref))):
        wk_ref[:, n * blk:(n + 1) * blk] = kr[0]
        wv_ref[n * blk:(n + 1) * blk] = vr[0]
    ws_ref[:, 0:3 * blk - GRID_W] = wk_ref[:, GRID_W:3 * blk]

    def key_window(off):
        if off % 2 == 0:
            return wk_ref[:, off * GRID_W:(off + NA_WIN_ROWS) * GRID_W]
        return ws_ref[:, (off - 1) * GRID_W:(off - 1 + NA_WIN_ROWS) * GRID_W]
    first_head = lax.broadcasted_iota(jnp.int32, (GRID_W, 2 * NA_HEAD_DIM), 1) < NA_HEAD_DIM
    q = q_ref[0]
    zero = jnp.zeros((GRID_W, 2 * NA_HEAD_DIM), q.dtype)
    pieces = []
    for j in range(NA_ROWS_PER_STEP):
        qj = q[j * GRID_W:(j + 1) * GRID_W]
        pieces += [jnp.where(first_head, qj, zero), jnp.where(first_head, zero, qj)]
    qs = jnp.concatenate(pieces, axis=0)
    sc_ref[...] = lax.dot_general(qs, ck_ref[0], _NT, preferred_element_type=F32)
    half = NA_WIN_ROWS // 2
    rows = 2 * GRID_W

    def window_row(j):
        return jnp.where(i == 0, max(j + half, NA_ROWS_PER_STEP),
                         jnp.where(i == last, min(j + half, NA_ROWS_PER_STEP), j + half))

    for j in range(NA_ROWS_PER_STEP):
        kw = key_window(j + half)
        if j < half:
            kw = jnp.where(i == 0, key_window(NA_ROWS_PER_STEP), kw)
        elif j > half:
            kw = jnp.where(i == last, key_window(NA_ROWS_PER_STEP), kw)
        sl_ref[j] = jnp.dot(qs[j * rows:(j + 1) * rows], kw, preferred_element_type=F32)
    for j in range(NA_ROWS_PER_STEP):
        d = window_row(j) - j - 1
        for c0 in range(0, rows, NA_SOFTMAX_ROWS):
            rs = slice(c0, c0 + NA_SOFTMAX_ROWS)
            ra = slice(j * rows + c0, j * rows + c0 + NA_SOFTMAX_ROWS)
            s_loc = sl_ref[j, rs, :] + bias_ref[0, d, rs, :]
            s_ctx = sc_ref[ra, :]
            m = jnp.maximum(jnp.max(s_loc, axis=1, keepdims=True), jnp.max(s_ctx, axis=1, keepdims=True))
            p_loc = jnp.exp2(s_loc - m)
            p_ctx = jnp.exp2(s_ctx - m)
            l = jnp.sum(p_loc, axis=1, keepdims=True) + jnp.sum(p_ctx, axis=1, keepdims=True)
            pl_ref[j, rs, :] = p_loc.astype(BF16)
            pc_ref[ra, :] = p_ctx.astype(pc_ref.dtype)
            li_ref[ra, :] = jnp.broadcast_to(1.0 / l, (NA_SOFTMAX_ROWS, 2 * NA_HEAD_DIM))
    for j in range(NA_ROWS_PER_STEP):
        start = pl.multiple_of(window_row(j) * GRID_W, GRID_W)
        vw = wv_ref[pl.ds(start, NA_WIN_ROWS * GRID_W), :]
        ol_ref[j * rows:(j + 1) * rows, :] = jnp.dot(pl_ref[j], vw, preferred_element_type=F32)
    o = (ol_ref[...] + jnp.dot(pc_ref[...], cv_ref[0], preferred_element_type=F32)) * li_ref[...]
    for j in range(NA_ROWS_PER_STEP):
        r0 = j * rows
        oj = jnp.where(first_head, o[r0:r0 + GRID_W], o[r0 + GRID_W:r0 + rows])
        o_ref[0, j * GRID_W:(j + 1) * GRID_W, :] = oj.astype(o_ref.dtype)


def _na(q, k, v, ck, cv, bias):
    B, L, W = q.shape
    Lc = ck.shape[1]
    blk = NA_ROWS_PER_STEP * GRID_W
    nblk = L // blk
    assert NA_ROWS_PER_STEP == NA_WIN_ROWS and nblk >= 2 and GRID_W * 2 == LANES
    pw = 2 * NA_HEAD_DIM
    cur = pl.BlockSpec((1, blk, pw), lambda b, h, i: (b, i, h))
    prev = pl.BlockSpec((1, blk, pw), lambda b, h, i: (b, jnp.maximum(i - 1, 0), h))
    nxt = pl.BlockSpec((1, blk, pw), lambda b, h, i: (b, jnp.minimum(i + 1, nblk - 1), h))
    kcur = pl.BlockSpec((1, pw, blk), lambda b, h, i: (b, h, i))
    kprev = pl.BlockSpec((1, pw, blk), lambda b, h, i: (b, h, jnp.maximum(i - 1, 0)))
    knxt = pl.BlockSpec((1, pw, blk), lambda b, h, i: (b, h, jnp.minimum(i + 1, nblk - 1)))
    cspec = pl.BlockSpec((1, Lc, pw), lambda b, h, i: (b, 0, h))
    bspec = pl.BlockSpec((1,) + bias.shape[1:], lambda b, h, i: (h, 0, 0, 0))
    stacked = 2 * blk
    return pl.pallas_call(
        _na_kernel,
        out_shape=jax.ShapeDtypeStruct((B, L, W), BF16),
        grid=(B, W // pw, nblk),
        in_specs=[cur, kprev, kcur, knxt, prev, cur, nxt, cspec, cspec, bspec],
        out_specs=cur,
        scratch_shapes=[pltpu.VMEM((pw, 3 * blk), BF16), pltpu.VMEM((pw, 3 * blk), BF16),
                        pltpu.VMEM((3 * blk, pw), BF16),
                        pltpu.VMEM((stacked, Lc), F32), pltpu.VMEM((stacked, Lc), BF16),
                        pltpu.VMEM((stacked, pw), F32), pltpu.VMEM((stacked, pw), F32),
                        pltpu.VMEM((NA_ROWS_PER_STEP, 2 * GRID_W, NA_WIN_ROWS * GRID_W), F32),
                        pltpu.VMEM((NA_ROWS_PER_STEP, 2 * GRID_W, NA_WIN_ROWS * GRID_W), BF16)],
        compiler_params=pltpu.CompilerParams(vmem_limit_bytes=VMEM_LIMIT),
        name="nattn",
    )(q, k, k, k, v, v, v, ck, cv, bias)


def _ctx_attn_kernel(q_ref, k_ref, v_ref, o_ref):
    q = q_ref[0]
    k = k_ref[0]
    v = v_ref[0]
    first_head = lax.broadcasted_iota(jnp.int32, q.shape, 1) < NA_HEAD_DIM
    outs = []
    for h in range(2):
        qm = jnp.where(first_head if h == 0 else jnp.logical_not(first_head), q, jnp.zeros_like(q))
        s = lax.dot_general(qm, k, _NT, preferred_element_type=F32)
        p = jnp.exp2(s - jnp.max(s, axis=1, keepdims=True))
        o = jnp.dot(p.astype(BF16), v, preferred_element_type=F32)
        outs.append(o / jnp.sum(p, axis=1, keepdims=True))
    o_ref[0] = jnp.where(first_head, outs[0], outs[1]).astype(o_ref.dtype)


def _ctx_attn(q, k, v):
    B, Lc, W = q.shape
    pw = 2 * NA_HEAD_DIM
    spec = pl.BlockSpec((1, Lc, pw), lambda b, h: (b, 0, h))
    return pl.pallas_call(
        _ctx_attn_kernel,
        out_shape=jax.ShapeDtypeStruct((B, Lc, W), BF16),
        grid=(B, W // pw),
        in_specs=[spec, spec, spec],
        out_specs=spec,
        name="ctx_attn",
    )(q, k, v)


LRU_CHUNK = 1024


def _lru_gate_weights(wa, ba, wi, bi, lam):
    def bd(w):
        nb, c, _ = w.shape
        blk_id = np.arange(nb * c) // c
        tiled = jnp.tile(w.reshape(nb * c, c), (1, nb))
        return jnp.where(jnp.asarray(blk_id[:, None] == blk_id[None, :]), tiled, 0.0)
    wg = jnp.stack([jnp.concatenate([bd(wa[d]), bd(wi[d])], axis=1) for d in range(2)]).astype(BF16)
    bg = jnp.stack([jnp.concatenate([ba[d], bi[d]])[None, :] for d in range(2)])
    return wg, bg, lam[:, None, :]


def _lru_coeffs(u, wg, bg, lam):
    C = u.shape[1]
    g = jnp.dot(u.astype(BF16), wg, preferred_element_type=F32) + bg
    sig = 0.5 + 0.5 * jnp.tanh(0.5 * g)
    r, ig = sig[:, :C], sig[:, C:]
    nl = -lam
    softplus = jnp.maximum(nl, 0.0) + jnp.log(1.0 + jnp.exp(-jnp.abs(nl)))
    log_a = (-LRU_C * softplus) * r
    a = jnp.exp(log_a)
    t = jnp.tanh(log_a)
    b = jnp.sqrt(-2.0 * t / (1.0 - t)) * (ig * u)
    return a, b


def _lru_scan(a, b, h0, reverse, ac_ref, bc_ref, h_ref):
    T, C = a.shape
    row = lax.broadcasted_iota(jnp.int32, a.shape, 0) % SUBLANES
    for s in (1, 2, 4):
        shift = T - s if reverse else s
        keep = (row < SUBLANES - s) if reverse else (row >= s)
        b = jnp.where(keep, a * pltpu.roll(b, shift, 0) + b, b)
        a = jnp.where(keep, a * pltpu.roll(a, shift, 0), a)
    ac_ref[...] = a
    bc_ref[...] = b
    ng = T // SUBLANES

    def group(g, h):
        r0 = pl.multiple_of((ng - 1 - g if reverse else g) * SUBLANES, SUBLANES)
        hr = ac_ref[pl.ds(r0, SUBLANES), :] * h + bc_ref[pl.ds(r0, SUBLANES), :]
        h_ref[pl.ds(r0, SUBLANES), :] = hr
        edge = hr[0:1] if reverse else hr[SUBLANES - 1:SUBLANES]
        return jnp.broadcast_to(edge, (SUBLANES, C))

    return lax.fori_loop(0, ng, group, h0, unroll=4)


def _gelu_tanh(x):
    return 0.5 * x * (1.0 + jnp.tanh(math.sqrt(2.0 / math.pi) * (x + 0.044715 * (x * x * x))))


def _lru_ctx_kernel(u_ref, xg_ref, wg_ref, bg_ref, lam_ref, hend_ref, yc_ref, ac_ref, bc_ref, h_ref):
    u = u_ref[0]
    C = u.shape[1]
    total = jnp.zeros_like(u)
    for d, rev in enumerate((False, True)):
        a, b = _lru_coeffs(u, wg_ref[d], bg_ref[d], lam_ref[d])
        hl = _lru_scan(a, b, jnp.zeros((SUBLANES, C), F32), rev, ac_ref, bc_ref, h_ref)
        hend_ref[0, d:d + 1, :] = hl[0:1]
        total = total + h_ref[...]
    yc_ref[0] = (total * _gelu_tanh(xg_ref[0])).astype(yc_ref.dtype)


def _lru_ctx(u, xg, wg, bg, lam):
    B, Lc, C = u.shape
    tok = pl.BlockSpec((1, Lc, C), lambda b: (b, 0, 0))
    full = lambda a: pl.BlockSpec(a.shape, lambda b: (0,) * a.ndim)
    return pl.pallas_call(
        _lru_ctx_kernel,
        out_shape=[jax.ShapeDtypeStruct((B, 2, C), F32), jax.ShapeDtypeStruct((B, Lc, C), BF16)],
        grid=(B,),
        in_specs=[tok, tok, full(wg), full(bg), full(lam)],
        out_specs=[pl.BlockSpec((1, 2, C), lambda b: (b, 0, 0)), tok],
        scratch_shapes=[pltpu.VMEM((Lc, C), F32)] * 3,
        name="lru_ctx",
    )(u, xg, wg, bg, lam)


def _lru_dir_kernel(*refs, d, reverse):
    if reverse:
        u_ref, hend_ref, wg_ref, bg_ref, lam_ref, hf_ref, xg_ref, o_ref, ac_ref, bc_ref, h_ref, carry_ref = refs
    else:
        u_ref, hend_ref, wg_ref, bg_ref, lam_ref, o_ref, ac_ref, bc_ref, carry_ref = refs
        h_ref = o_ref.at[0]
    C = u_ref.shape[2]

    @pl.when(pl.program_id(1) == 0)
    def _():
        carry_ref[...] = jnp.broadcast_to(hend_ref[0, d:d + 1, :], (SUBLANES, C))

    a, b = _lru_coeffs(u_ref[0], wg_ref[d], bg_ref[d], lam_ref[d])
    carry_ref[...] = _lru_scan(a, b, carry_ref[...], reverse, ac_ref, bc_ref, h_ref)
    if reverse:
        o_ref[0] = ((hf_ref[0] + h_ref[...]) * _gelu_tanh(xg_ref[0])).astype(o_ref.dtype)


def _lru_dir(u, hend, wg, bg, lam, hf=None, xg=None):
    B, L, C = u.shape
    reverse = hf is not None
    T = LRU_CHUNK
    nb = L // T
    tok = pl.BlockSpec((1, T, C), (lambda b, i: (b, nb - 1 - i, 0)) if reverse else (lambda b, i: (b, i, 0)))
    full = lambda a: pl.BlockSpec(a.shape, lambda b, i: (0,) * a.ndim)
    ins = [u, hend, wg, bg, lam] + ([hf, xg] if reverse else [])
    specs = [tok, pl.BlockSpec((1, 2, C), lambda b, i: (b, 0, 0)), full(wg), full(bg), full(lam)]
    specs += [tok, tok] if reverse else []
    scratch = [pltpu.VMEM((T, C), F32)] * (3 if reverse else 2) + [pltpu.VMEM((SUBLANES, C), F32)]
    return pl.pallas_call(
        functools.partial(_lru_dir_kernel, d=int(reverse), reverse=reverse),
        out_shape=jax.ShapeDtypeStruct((B, L, C), BF16 if reverse else F32),
        grid=(B, nb),
        in_specs=specs,
        out_specs=tok,
        scratch_shapes=scratch,
        compiler_params=pltpu.CompilerParams(dimension_semantics=("arbitrary", "arbitrary")),
        name="lru_bwd" if reverse else "lru_fwd",
    )(*ins)


def _lru(u, xg, u_c, xg_c, wa, ba, wi, bi, lam):
    wg, bg, lam3 = _lru_gate_weights(wa, ba, wi, bi, lam)
    hend, yc = _lru_ctx(u_c, xg_c, wg, bg, lam3)
    hf = _lru_dir(u, hend, wg, bg, lam3)
    return _lru_dir(u, hend, wg, bg, lam3, hf, xg), yc


def kernel(x, c, ctx, c_ctx, ada_w, ada_b, g_mix_pre, g_mix_post, g_ffn_pre, g_ffn_post, w_in, w_out, hy_conv_w,
           hy_conv_b, hy_f_w1, hy_f_b1, hy_f_w2, hy_f_b2, hy_f_w3, hy_f_freq, hy_bias, na_rpb, lru_conv_w,
           lru_conv_b, lru_wa, lru_ba, lru_wi, lru_bi, lru_lam, ffn_w_gu, ffn_w_down):
    B, L, D = x.shape
    Lc = ctx.shape[1]
    tm = 1024

    assert B + 1 <= SUBLANES
    cond_t = jnp.zeros((D, SUBLANES), F32).at[:, 0:B].set(c.T).at[:, B].set(c_ctx)
    mods = _modulation(cond_t, B + 1, ada_w, ada_b)

    xc = ctx
    for l in range(DEPTH):
        with_ctx_out = l < DEPTH - 1
        m = mods[l].reshape(8, 6, D)
        lat = [m[0:B, j][:, None, :] for j in range(6)]
        cx = [jnp.broadcast_to(m[B, j][None, None, :], (B, 1, D)) for j in range(6)]
        row = lambda a: a.reshape(1, -1)

        w_in_bf = w_in[l].astype(BF16)
        w_out_bf = w_out[l].astype(BF16)
        wgu = ffn_w_gu[l].astype(BF16)
        wd = ffn_w_down[l].astype(BF16)

        conv_args = (hy_conv_w[l], row(hy_conv_b[l]), lru_conv_w[l], row(lru_conv_b[l]))
        hv, hx1, hx2, q, k, v, lu, lg = _inproj(x, lat[1], lat[0], row(g_mix_pre[l]), w_in_bf, *conv_args, tm=tm,
                                                chan_major=True)
        cv, cx1, cx2, cq, ck, cvv, clu, clg = _inproj(xc, cx[1], cx[0], row(g_mix_pre[l]), w_in_bf, *conv_args,
                                                      tm=Lc, chan_major=False)

        filt = (hy_f_w1[l], hy_f_b1[l], hy_f_w2[l], hy_f_b2[l], hy_f_w3[l], hy_f_freq[l])
        y_hy = _hyena(hv, hx1, hx2, filt, hy_bias[l])
        y_na = _na(q, k, v, ck, cvv, _na_bias_table(na_rpb[l]))
        y_lru, yc_lru = _lru(lu, lg, clu, clg, lru_wa[l], lru_ba[l], lru_wi[l], lru_bi[l], lru_lam[l])

        ffn_args = (row(g_ffn_pre[l]), row(g_ffn_post[l]), wgu, wd)
        x = _out_ffn(y_hy, y_na, y_lru, w_out_bf, x, row(g_mix_post[l]), lat[2], lat[4], lat[3], lat[5], *ffn_args,
                     tm=tm, chan_major=True)

        if with_ctx_out:
            yc_hy = _hyena_ctx(cv, cx1, cx2, filt, hy_bias[l])
            yc_na = _ctx_attn(cq, ck, cvv)
            xc = _out_ffn(yc_hy, yc_na, yc_lru, w_out_bf, xc, row(g_mix_post[l]), cx[2], cx[4], cx[3], cx[5],
                          *ffn_args, tm=Lc, chan_major=False)
    return x
```

```python
import functools
import math

import jax
import jax.numpy as jnp
import numpy as np
from jax import lax
from jax.experimental import pallas as pl
from jax.experimental.pallas import tpu as pltpu

F32 = jnp.float32
BF16 = jnp.bfloat16

D_MODEL = 1024
DEPTH = 2
GRID_W = 64
HY_WIDTH = D_MODEL // 4
NA_HEAD_DIM = 64
NA_WIDTH = D_MODEL // 2
NA_HEADS = NA_WIDTH // NA_HEAD_DIM
LRU_WIDTH = D_MODEL // 4
HY_ORDER = 2
HY_BANDS = 16
HY_FAST_DECAY = 0.3
HY_SLOW_DECAY = 1.5
HY_DECAY_TARGET = 1e-2
NA_WIN_ROWS = 8
NA_WIN_COLS = 16
LRU_C = 8.0
D_FF = -(-8 * D_MODEL // (3 * 256)) * 256
RMS_EPS = 1e-6
NEG_INF = -1e30
LOG2E = math.log2(math.e)

_HY_END = 3 * HY_WIDTH
_NA_END = _HY_END + 3 * NA_WIDTH
_LRU_MID = _NA_END + LRU_WIDTH

HALO = 8
SUBLANES = 8
VMEM_LIMIT = 56 * 1024 * 1024


def _rms(x, g):
    return x * lax.rsqrt(jnp.mean(x * x, axis=-1, keepdims=True) + RMS_EPS) * g


def _mod_kernel(ct_ref, w_ref, b_ref, o_ref, *, n_cond):
    ct = ct_ref[...]
    st = ct * jax.nn.sigmoid(ct)
    w = w_ref[0]
    rows = [jnp.sum(w * st[:, r:r + 1], axis=0, keepdims=True) for r in range(n_cond)]
    rows.append(jnp.zeros((SUBLANES - n_cond, w.shape[1]), F32))
    o_ref[0] = jnp.concatenate(rows, axis=0) + b_ref[0]


def _modulation(cond_t, n_cond, ada_w, ada_b):
    tn = 768
    n = ada_w.shape[-1]
    return pl.pallas_call(
        functools.partial(_mod_kernel, n_cond=n_cond),
        out_shape=jax.ShapeDtypeStruct((DEPTH, SUBLANES, n), F32),
        grid=(DEPTH, n // tn),
        in_specs=[pl.BlockSpec((D_MODEL, SUBLANES), lambda l, j: (0, 0)),
                  pl.BlockSpec((1, D_MODEL, tn), lambda l, j: (l, 0, j)),
                  pl.BlockSpec((1, 1, tn), lambda l, j: (l, 0, j))],
        out_specs=pl.BlockSpec((1, SUBLANES, tn), lambda l, j: (l, 0, j)),
        compiler_params=pltpu.CompilerParams(vmem_limit_bytes=VMEM_LIMIT),
        name="adaln_modulation",
    )(cond_t, ada_w, ada_b.reshape(DEPTH, 1, n))


CM_LANES = 256
LANES = 128
CM_HALVES = CM_LANES // LANES


def _chan_major_shape(B, L, C):
    return (B, C // SUBLANES, CM_HALVES, L // CM_LANES * SUBLANES, LANES)


def _chan_major_spec(tm, C):
    return pl.BlockSpec((1, C // SUBLANES, CM_HALVES, tm // CM_LANES * SUBLANES, LANES),
                        lambda b, i: (b, 0, 0, i, 0))


def _store_chan_major(ref, u):
    ut = u.T
    for g in range(u.shape[1] // SUBLANES):
        for j in range(u.shape[0] // CM_LANES):
            for h in range(CM_HALVES):
                lane0 = j * CM_LANES + h * LANES
                ref[0, g, h, j * SUBLANES:(j + 1) * SUBLANES, :] = ut[g * SUBLANES:(g + 1) * SUBLANES,
                                                                      lane0:lane0 + LANES]


def _load_chan_major(ref):
    _, ng, _, nr, _ = ref.shape
    rows = [jnp.concatenate([ref[0, g, h, j * SUBLANES:(j + 1) * SUBLANES, :]
                             for j in range(nr // SUBLANES) for h in range(CM_HALVES)], axis=1) for g in range(ng)]
    return jnp.concatenate(rows, axis=0).T


def _chan_rows(ref, lead, c, n):
    return jnp.concatenate([ref.at[lead + (h,)][pl.ds(c, n, stride=SUBLANES), :] for h in range(CM_HALVES)], axis=1)


def _inproj_kernel(xp_ref, xc_ref, xn_ref, sc_ref, sh_ref, g_ref, w_ref, hcw_ref, hcb_ref, lcw_ref, lcb_ref,
                   hv_ref, hx1_ref, hx2_ref, q_ref, k_ref, v_ref, lu_ref, lg_ref, pe_ref, *, tm, chan_major):
    i = pl.program_id(1)
    last = pl.num_programs(1) - 1
    g = g_ref[...]
    sc1 = 1.0 + sc_ref[0]
    sh = sh_ref[0]

    def norm_mod(xv):
        return _rms(xv, g) * sc1 + sh

    hp = norm_mod(xp_ref[0]) * (i > 0).astype(F32)
    hn = norm_mod(xn_ref[0]) * (i < last).astype(F32)
    he = jnp.concatenate([hp, norm_mod(xc_ref[0]), hn], axis=0).astype(BF16)

    pe_ref[:, 0:_HY_END] = jnp.dot(he, w_ref[:, 0:_HY_END], preferred_element_type=F32)
    pe_ref[:, _HY_END:] = jnp.dot(he, w_ref[:, _NA_END:_LRU_MID], preferred_element_type=F32)
    hc = he[HALO:HALO + tm]
    qkv = jnp.dot(hc, w_ref[:, _HY_END:_NA_END], preferred_element_type=F32)
    q_ref[0] = (qkv[:, 0:NA_WIDTH] * (NA_HEAD_DIM ** -0.5 * LOG2E)).astype(BF16)
    kk = qkv[:, NA_WIDTH:2 * NA_WIDTH]
    k_ref[0] = (kk.T if chan_major else kk).astype(BF16)
    v_ref[0] = qkv[:, 2 * NA_WIDTH:].astype(BF16)
    lg_ref[0] = jnp.dot(hc, w_ref[:, _LRU_MID:], preferred_element_type=F32)

    u = hcb_ref[...]
    for kk in range(3):
        u = u + hcw_ref[kk:kk + 1, :] * pe_ref[pl.ds(HALO - 1 + kk, tm), 0:_HY_END]
    for n, ref in enumerate((hv_ref, hx1_ref, hx2_ref)):
        un = u[:, n * HY_WIDTH:(n + 1) * HY_WIDTH]
        if chan_major:
            _store_chan_major(ref, un)
        else:
            ref[0] = un
    ul = lcb_ref[...]
    for kk in range(4):
        ul = ul + lcw_ref[kk:kk + 1, :] * pe_ref[pl.ds(HALO - 2 + kk, tm), _HY_END:]
    lu_ref[0] = ul


def _inproj(x, sc, sh, g, w_bf, hcw, hcb, lcw, lcb, *, tm, chan_major):
    B, L, D = x.shape
    nb = tm // HALO
    nh = L // HALO
    tok = lambda w, dt: jax.ShapeDtypeStruct((B, L, w), dt)
    tspec = lambda w: pl.BlockSpec((1, tm, w), lambda b, i: (b, i, 0))
    full = lambda a: pl.BlockSpec(a.shape, lambda b, i: (0,) * a.ndim)
    vec = pl.BlockSpec((1, 1, D), lambda b, i: (b, 0, 0))
    if chan_major:
        hy_shape = jax.ShapeDtypeStruct(_chan_major_shape(B, L, HY_WIDTH), F32)
        hy_spec = _chan_major_spec(tm, HY_WIDTH)
        k_shape = jax.ShapeDtypeStruct((B, NA_WIDTH, L), BF16)
        k_spec = pl.BlockSpec((1, NA_WIDTH, tm), lambda b, i: (b, 0, i))
    else:
        hy_shape, hy_spec = tok(HY_WIDTH, F32), tspec(HY_WIDTH)
        k_shape, k_spec = tok(NA_WIDTH, BF16), tspec(NA_WIDTH)
    return pl.pallas_call(
        functools.partial(_inproj_kernel, tm=tm, chan_major=chan_major),
        out_shape=[hy_shape] * 3 + [tok(NA_WIDTH, BF16), k_shape, tok(NA_WIDTH, BF16)] + [tok(LRU_WIDTH, F32)] * 2,
        grid=(B, L // tm),
        in_specs=[pl.BlockSpec((1, HALO, D), lambda b, i: (b, jnp.maximum(i * nb - 1, 0), 0)),
                  pl.BlockSpec((1, tm, D), lambda b, i: (b, i, 0)),
                  pl.BlockSpec((1, HALO, D), lambda b, i: (b, jnp.minimum((i + 1) * nb, nh - 1), 0)),
                  vec, vec, full(g), pl.BlockSpec(w_bf.shape, lambda b, i: (0, 0), pipeline_mode=pl.Buffered(1)),
                  full(hcw), full(hcb), full(lcw), full(lcb)],
        out_specs=[hy_spec] * 3 + [tspec(NA_WIDTH), k_spec, tspec(NA_WIDTH)] + [tspec(LRU_WIDTH)] * 2,
        scratch_shapes=[pltpu.VMEM((tm + 2 * HALO, _HY_END + LRU_WIDTH), F32)],
        compiler_params=pltpu.CompilerParams(vmem_limit_bytes=VMEM_LIMIT),
        name="inproj",
    )(x, x, x, sc, sh, g, w_bf, hcw, hcb, lcw, lcb)


FF_CHUNK = 256


def _out_ffn_kernel(yh_ref, yn_ref, yl_ref, wo_ref, x_ref, gm_ref, gtm_ref, sc_ref, sh_ref, gt_ref, gpre_ref,
                    gpost_ref, wgu_ref, wd_ref, o_ref, x1_ref, h_ref, *, chan_major, halves):
    yh = (_load_chan_major(yh_ref) if chan_major else yh_ref[0]).astype(BF16)
    tm = x_ref.shape[1]
    for r0 in range(0, tm, tm // halves):
        rs = slice(r0, r0 + tm // halves)
        y = jnp.dot(yh[rs], wo_ref[0:HY_WIDTH], preferred_element_type=F32)
        y = y + jnp.dot(yn_ref[0, rs, :], wo_ref[HY_WIDTH:HY_WIDTH + NA_WIDTH], preferred_element_type=F32)
        y = y + jnp.dot(yl_ref[0, rs, :], wo_ref[HY_WIDTH + NA_WIDTH:], preferred_element_type=F32)
        x = x_ref[0, rs, :] + gtm_ref[0] * _rms(y, gm_ref[...])
        x1_ref[rs, :] = x
        h_ref[rs, :] = (_rms(x, gpre_ref[...]) * (1.0 + sc_ref[0]) + sh_ref[0]).astype(BF16)
    for r0 in range(0, tm, tm // halves):
        rs = slice(r0, r0 + tm // halves)
        h = h_ref[rs, :]
        acc = jnp.zeros((tm // halves, x_ref.shape[2]), F32)
        for c0 in range(0, D_FF, FF_CHUNK):
            g = jnp.dot(h, wgu_ref[:, c0:c0 + FF_CHUNK], preferred_element_type=F32)
            u = jnp.dot(h, wgu_ref[:, D_FF + c0:D_FF + c0 + FF_CHUNK], preferred_element_type=F32)
            a = (g * jax.nn.sigmoid(g) * u).astype(BF16)
            acc = acc + jnp.dot(a, wd_ref[c0:c0 + FF_CHUNK, :], preferred_element_type=F32)
        o_ref[0, rs, :] = x1_ref[rs, :] + gt_ref[0] * _rms(acc, gpost_ref[...])


def _out_ffn(yh, yn, yl, wo, x, gm, gtm, sc, sh, gt, gpre, gpost, wgu, wd, *, tm, chan_major):
    B, L, D = x.shape
    tspec = lambda w: pl.BlockSpec((1, tm, w), lambda b, i: (b, i, 0))
    vec = pl.BlockSpec((1, 1, D), lambda b, i: (b, 0, 0))
    full = lambda a: pl.BlockSpec(a.shape, lambda b, i: (0,) * a.ndim)
    res = lambda a: pl.BlockSpec(a.shape, lambda b, i: (0,) * a.ndim, pipeline_mode=pl.Buffered(1))
    hy_spec = _chan_major_spec(tm, HY_WIDTH) if chan_major else tspec(HY_WIDTH)
    return pl.pallas_call(
        functools.partial(_out_ffn_kernel, chan_major=chan_major, halves=2 if tm >= 1024 else 1),
        out_shape=jax.ShapeDtypeStruct((B, L, D), F32),
        grid=(B, L // tm),
        in_specs=[hy_spec, tspec(NA_WIDTH), tspec(LRU_WIDTH), res(wo), tspec(D), full(gm), vec,
                  vec, vec, vec, full(gpre), full(gpost), res(wgu), res(wd)],
        out_specs=tspec(D),
        scratch_shapes=[pltpu.VMEM((tm, D), F32), pltpu.VMEM((tm, D), BF16)],
        compiler_params=pltpu.CompilerParams(vmem_limit_bytes=VMEM_LIMIT),
        name="out_ffn",
    )(yh, yn, yl, wo, x, gm, gtm, sc, sh, gt, gpre, gpost, wgu, wd)


HY_N1 = CM_LANES
HY_N2 = 128
HY_FILT_COLS = 2048


def _block_cplx(re, im):
    return np.block([[re, -im], [im, re]])


@functools.lru_cache(maxsize=None)
def _dft_consts():
    n = HY_N1 * HY_N2
    h = HY_N2 // 2
    i1 = np.arange(HY_N1, dtype=np.float64)
    i2 = np.arange(HY_N2, dtype=np.float64)
    a1 = 2.0 * np.pi * np.outer(i1, i1) / HY_N1
    a2 = 2.0 * np.pi * np.outer(i2, i2) / HY_N2
    at = 2.0 * np.pi * np.outer(i2, i1) / n
    r2, m2 = np.cos(a2), -np.sin(a2)
    r1, m1 = np.cos(a1), -np.sin(a1)
    f32 = lambda m: np.asarray(m, np.float32)
    return dict(
        rows_data=f32(_block_cplx(r2[:, :h], m2[:, :h])),
        rows_filt=f32(np.concatenate([r2, m2], axis=0)),
        rows_out=f32(_block_cplx(r2[:h, :], m2[:h, :])),
        lanes=f32(np.block([[r1, m1], [-m1, r1]])),
        twr=f32(np.cos(at)), twi=f32(-np.sin(at)))


def _dft_const(name):
    return jnp.asarray(_dft_consts()[name], F32)


def _cmul(ar, ai, br, bi):
    return ar * br - ai * bi, ar * bi + ai * br


def _hy_conv_kernel(v_ref, x1_ref, x2_ref, k0_ref, k1_ref, ks0_ref, ks1_ref, bias_ref, rd_ref, rf_ref, ro_ref,
                    ln_ref, twr_ref, twi_ref, o_ref, st_ref, ks_ref):
    g = pl.program_id(0)
    n_ch = SUBLANES
    h = HY_N2 // 2
    rows_data = rd_ref[...].astype(BF16)
    rows_filt = rf_ref[...].astype(BF16)
    rows_out = ro_ref[...].astype(BF16)
    lanes = ln_ref[...].astype(BF16)
    twr, twi = twr_ref[...], twi_ref[...]

    def rows_then_twiddle(mat, x, c):
        a = jnp.dot(mat, x.astype(BF16), preferred_element_type=F32)
        ar, ai = _cmul(a[:HY_N2], a[HY_N2:], twr, twi)
        st_ref[c * HY_N2:(c + 1) * HY_N2, 0:HY_N1] = ar.astype(BF16)
        st_ref[c * HY_N2:(c + 1) * HY_N2, HY_N1:] = ai.astype(BF16)

    for o, (k_ref, s_ref) in enumerate(((k0_ref, ks0_ref), (k1_ref, ks1_ref))):
        for c in range(n_ch):
            rows_then_twiddle(rows_filt, _chan_rows(k_ref, (0,), c, HY_N2), c)
        spec = jnp.dot(st_ref[...], lanes, preferred_element_type=F32)
        inv = 1.0 / (jnp.sum(s_ref[...], axis=1, keepdims=True) * float(HY_N1 * HY_N2))
        for c in range(n_ch):
            ks_ref[o, c * HY_N2:(c + 1) * HY_N2, :] = spec[c * HY_N2:(c + 1) * HY_N2] * inv[c:c + 1, :]

    z = [[_chan_rows(v_ref, (b, 0), c, h) for c in range(n_ch)] for b in range(2)]
    for o, gate_ref in enumerate((x1_ref, x2_ref)):
        for c in range(n_ch):
            rows_then_twiddle(rows_data, jnp.concatenate([z[0][c], z[1][c]], axis=0), c)
        x = jnp.dot(st_ref[...], lanes, preferred_element_type=F32)
        pr, pi = _cmul(x[:, :HY_N1], x[:, HY_N1:], ks_ref[o, :, 0:HY_N1], ks_ref[o, :, HY_N1:])
        st_ref[:, 0:HY_N1] = pr.astype(BF16)
        st_ref[:, HY_N1:] = (-pi).astype(BF16)
        y = jnp.dot(st_ref[...], lanes, preferred_element_type=F32)
        for c in range(n_ch):
            yc = y[c * HY_N2:(c + 1) * HY_N2]
            yr, yi = _cmul(yc[:, :HY_N1], yc[:, HY_N1:], twr, twi)
            out = jnp.dot(rows_out, jnp.concatenate([yr, yi], axis=0).astype(BF16),
                          preferred_element_type=F32)
            bias = bias_ref[o, g * n_ch + c]
            for b, conv in ((0, out[:h]), (1, -out[h:])):
                z[b][c] = _chan_rows(gate_ref, (b, 0), c, h) * (conv + bias * z[b][c])
    for b in range(2):
        for c in range(n_ch):
            for hh in range(CM_HALVES):
                o_ref.at[b, 0, hh][pl.ds(c, h, stride=SUBLANES), :] = z[b][c][:, hh * LANES:(hh + 1) * LANES]


def _hy_conv(v, x1, x2, kc, ksum, hy_bias):
    B, ng = v.shape[0], v.shape[1]
    assert B == 2 and v.shape[3] * 2 == HY_N2 * SUBLANES and kc.shape[2] == HY_N2 * SUBLANES
    consts = [_dft_const(n) for n in ("rows_data", "rows_filt", "rows_out", "lanes", "twr", "twi")]
    tok = pl.BlockSpec((B, 1) + v.shape[2:], lambda g: (0, g, 0, 0, 0))
    kspec = lambda o: pl.BlockSpec((1,) + kc.shape[1:], lambda g: (o * ng + g, 0, 0, 0))
    sspec = lambda o: pl.BlockSpec((SUBLANES, ksum.shape[1]), lambda g: (o * ng + g, 0))
    full = lambda a: pl.BlockSpec(a.shape, lambda g: (0,) * a.ndim)
    return pl.pallas_call(
        _hy_conv_kernel,
        out_shape=jax.ShapeDtypeStruct(v.shape, F32),
        grid=(ng,),
        in_specs=[tok, tok, tok, kspec(0), kspec(1), sspec(0), sspec(1),
                  pl.BlockSpec(memory_space=pltpu.SMEM)] + [full(a) for a in consts],
        out_specs=tok,
        scratch_shapes=[pltpu.VMEM((SUBLANES * HY_N2, 2 * HY_N1), BF16),
                        pltpu.VMEM((HY_ORDER, SUBLANES * HY_N2, 2 * HY_N1), F32)],
        compiler_params=pltpu.CompilerParams(vmem_limit_bytes=VMEM_LIMIT),
        name="hy_conv",
    )(v, x1, x2, kc, kc, ksum, ksum, hy_bias, *consts)


def _hy_filter_kernel(w1t_ref, w1cs_ref, trig_ref, b1_ref, f0_ref, w2_ref, b2_ref, f1_ref, w3_ref, dl_ref,
                      k_ref, s_ref, *, L, P, chan_major):
    i = pl.program_id(0)

    def pos_t(rows):
        n = i * P + lax.broadcasted_iota(jnp.int32, (rows, P), 1)
        return n, jnp.where(n < L, n, 2 * L - n).astype(F32) / L

    _, t1 = pos_t(1)
    h = w1t_ref[...] * t1 + b1_ref[...]
    h = h + jnp.dot(w1cs_ref[...], trig_ref[...].astype(BF16), preferred_element_type=F32)
    h = jnp.sin(f0_ref[...] * h)
    h = jnp.sin(f1_ref[...] * (jnp.dot(w2_ref[...], h.astype(BF16), preferred_element_type=F32) + b2_ref[...]))
    k = jnp.dot(w3_ref[0], h.astype(BF16), preferred_element_type=F32)
    n, t = pos_t(1)
    k = jnp.where(n == L, 0.0, k * jnp.exp(-t * dl_ref[...]))
    if chan_major:
        for j in range(P // CM_LANES):
            for hh in range(CM_HALVES):
                lane0 = j * CM_LANES + hh * LANES
                k_ref[:, hh, j * SUBLANES:(j + 1) * SUBLANES, :] = k[:, lane0:lane0 + LANES].reshape(
                    k_ref.shape[0], SUBLANES, LANES)
    else:
        k_ref[...] = k
    ka = jnp.abs(k)
    tot = ka[:, 0:LANES]
    for j in range(1, P // LANES):
        tot = tot + ka[:, j * LANES:(j + 1) * LANES]

    @pl.when(i == 0)
    def _():
        s_ref[...] = jnp.zeros_like(s_ref)

    s_ref[...] += tot


@functools.lru_cache(maxsize=None)
def _hy_trig_features(L):
    n = np.arange(2 * L)
    t = np.where(n < L, n, 2 * L - n).astype(np.float64) / L
    ang = 2.0 * np.pi * t[None, :] * np.arange(1, HY_BANDS + 1, dtype=np.float64)[:, None]
    return np.asarray(np.concatenate([np.cos(ang), np.sin(ang)], axis=0), np.float32)


def _hy_filter(L, P, chan_major, f_w1, f_b1, f_w2, f_b2, f_w3, f_freq):
    nb = 2 * L // P
    assert nb % 2 == 0 and P % CM_LANES == 0
    hid = f_w2.shape[0]
    oc = HY_ORDER * HY_WIDTH
    col = lambda v: v.reshape(-1, 1)
    w3 = f_w3.reshape(hid, HY_ORDER, 2, HY_WIDTH)
    w3 = jnp.stack([w3[:, :, d].reshape(hid, oc).T for d in range(2)]).astype(BF16)
    deltas = np.abs(np.linspace(math.log(HY_DECAY_TARGET) / HY_FAST_DECAY,
                                math.log(HY_DECAY_TARGET) / HY_SLOW_DECAY, HY_WIDTH, dtype=np.float32))
    ins = [col(f_w1[0]), f_w1[1:].T.astype(BF16), jnp.asarray(_hy_trig_features(L)), col(f_b1),
           col(f_freq[0]), f_w2.T.astype(BF16), col(f_b2), col(f_freq[1]), w3,
           jnp.asarray(np.tile(deltas, HY_ORDER)[:, None])]
    full = lambda a: pl.BlockSpec(a.shape, lambda i: (0,) * a.ndim)
    specs = [full(a) for a in ins]
    specs[2] = pl.BlockSpec((2 * HY_BANDS, P), lambda i: (0, i))
    specs[8] = pl.BlockSpec((1,) + w3.shape[1:], lambda i: (i // (nb // 2), 0, 0))
    if chan_major:
        kshape = (oc // SUBLANES, CM_HALVES, 2 * L // CM_LANES * SUBLANES, LANES)
        kspec = pl.BlockSpec((oc // SUBLANES, CM_HALVES, P // CM_LANES * SUBLANES, LANES), lambda i: (0, 0, i, 0))
    else:
        kshape = (oc, 2 * L)
        kspec = pl.BlockSpec((oc, P), lambda i: (0, i))
    return pl.pallas_call(
        functools.partial(_hy_filter_kernel, L=L, P=P, chan_major=chan_major),
        out_shape=[jax.ShapeDtypeStruct(kshape, F32), jax.ShapeDtypeStruct((oc, LANES), F32)],
        grid=(nb,),
        in_specs=specs,
        out_specs=[kspec, pl.BlockSpec((oc, LANES), lambda i: (0, 0))],
        compiler_params=pltpu.CompilerParams(dimension_semantics=("arbitrary",), vmem_limit_bytes=VMEM_LIMIT),
        name="hy_filter",
    )(*ins)


def _hyena(v, x1, x2, filt, hy_bias):
    L = v.shape[3] // SUBLANES * CM_LANES
    assert 2 * L == HY_N1 * HY_N2
    kc, ksum = _hy_filter(L, HY_FILT_COLS, True, *filt)
    return _hy_conv(v, x1, x2, kc, ksum, hy_bias)


@functools.lru_cache(maxsize=None)
def _ctx_dft_consts(n):
    i = np.arange(n, dtype=np.float64)
    a = 2.0 * np.pi * np.outer(i, i) / n
    f = np.concatenate([np.cos(a), -np.sin(a)], axis=0)
    return np.asarray(f, np.float32), np.asarray(f.T[:n // 2], np.float32)


def _hyena_ctx_kernel(v_ref, x1_ref, x2_ref, kc_ref, ks_ref, bias_ref, ff_ref, fi_ref, o_ref):
    n = ff_ref.shape[1]
    lc = n // 2
    C = v_ref.shape[2]
    ff = ff_ref[...].astype(BF16)
    inv = 1.0 / (jnp.sum(ks_ref[...], axis=1, keepdims=True) * float(n))
    kn = (kc_ref[...] * inv).T
    kspec = jnp.dot(ff, kn.astype(BF16), preferred_element_type=F32)
    z = jnp.concatenate([v_ref[0], v_ref[1]], axis=1)
    for o, g_ref in enumerate((x1_ref, x2_ref)):
        x = jnp.dot(ff[:, :lc], z.astype(BF16), preferred_element_type=F32)
        k = kspec[:, o * C:(o + 1) * C]
        kr = jnp.concatenate([k[:n], k[:n]], axis=1)
        ki = jnp.concatenate([k[n:], k[n:]], axis=1)
        yr, yi = _cmul(x[:n], x[n:], kr, ki)
        y = jnp.concatenate([yr, yi], axis=0)
        conv = jnp.dot(fi_ref[...].astype(BF16), y.astype(BF16), preferred_element_type=F32)
        bias = jnp.concatenate([bias_ref[o:o + 1, :], bias_ref[o:o + 1, :]], axis=1)
        z = jnp.concatenate([g_ref[0], g_ref[1]], axis=1) * (conv + bias * z)
    o_ref[0] = z[:, :C]
    o_ref[1] = z[:, C:]


def _hyena_ctx(v, x1, x2, filt, hy_bias):
    B, Lc, C = v.shape
    assert B == 2
    kc, ksum = _hy_filter(Lc, Lc, False, *filt)
    ff, fi = (jnp.asarray(m, F32) for m in _ctx_dft_consts(2 * Lc))
    return pl.pallas_call(
        _hyena_ctx_kernel,
        out_shape=jax.ShapeDtypeStruct((B, Lc, C), F32),
        compiler_params=pltpu.CompilerParams(vmem_limit_bytes=VMEM_LIMIT),
        name="hyena_ctx",
    )(v, x1, x2, kc, ksum, hy_bias, ff, fi)


NA_ROWS_PER_STEP = 8
NA_SOFTMAX_ROWS = 32
_NT = (((1,), (1,)), ((), ()))


def _na_bias_table(rpb):
    qc = np.arange(GRID_W)[:, None]
    kc = np.arange(GRID_W)[None, :]
    start = np.clip(qc - NA_WIN_COLS // 2, 0, GRID_W - NA_WIN_COLS)
    valid = (kc >= start) & (kc < start + NA_WIN_COLS)
    pad = jnp.pad(rpb, ((0, 0), (0, 0), (GRID_W, GRID_W)))
    shift = GRID_W + NA_WIN_COLS - 1
    toep = jnp.stack([pad[:, :, shift - c:shift - c + GRID_W] for c in range(GRID_W)], axis=2)
    full = jnp.where(jnp.asarray(valid)[None, None], toep * LOG2E, NEG_INF)
    t = jnp.stack([full[:, d:d + NA_WIN_ROWS] for d in range(NA_WIN_ROWS)], axis=1)
    t = t.reshape(NA_HEADS // 2, 2, NA_WIN_ROWS, NA_WIN_ROWS, GRID_W, GRID_W).transpose(0, 2, 1, 4, 3, 5)
    return t.reshape(NA_HEADS // 2, NA_WIN_ROWS, 2 * GRID_W, NA_WIN_ROWS * GRID_W).astype(F32)


def _na_kernel(q_ref, kp_ref, kc_ref, kn_ref, vp_ref, vc_ref, vn_ref, ck_ref, cv_ref, bias_ref, o_ref,
               wk_ref, ws_ref, wv_ref, sc_ref, pc_ref, ol_ref, li_ref, sl_ref, pl_ref):
    i = pl.program_id(2)
    last = pl.num_programs(2) - 1
    blk = NA_ROWS_PER_STEP * GRID_W
    for n, (kr, vr) in enumerate(((kp_ref, vp_ref), (kc_ref, vc_ref), (kn_ref, vn_ref))):
        wk_ref[:, n * blk:(n + 1) * blk] = kr[0]
        wv_ref[n * blk:(n + 1) * blk] = vr[0]
    ws_ref[:, 0:3 * blk - GRID_W] = wk_ref[:, GRID_W:3 * blk]

    def key_window(off):
        if off % 2 == 0:
            return wk_ref[:, off * GRID_W:(off + NA_WIN_ROWS) * GRID_W]
        return ws_ref[:, (off - 1) * GRID_W:(off - 1 + NA_WIN_ROWS) * GRID_W]
    first_head = lax.broadcasted_iota(jnp.int32, (GRID_W, 2 * NA_HEAD_DIM), 1) < NA_HEAD_DIM
    q = q_ref[0]
    zero = jnp.zeros((GRID_W, 2 * NA_HEAD_DIM), q.dtype)
    pieces = []
    for j in range(NA_ROWS_PER_STEP):
        qj = q[j * GRID_W:(j + 1) * GRID_W]
        pieces += [jnp.where(first_head, qj, zero), jnp.where(first_head, zero, qj)]
    qs = jnp.concatenate(pieces, axis=0)
    sc_ref[...] = lax.dot_general(qs, ck_ref[0], _NT, preferred_element_type=F32)
    half = NA_WIN_ROWS // 2
    rows = 2 * GRID_W

    def window_row(j):
        return jnp.where(i == 0, max(j + half, NA_ROWS_PER_STEP),
                         jnp.where(i == last, min(j + half, NA_ROWS_PER_STEP), j + half))

    for j in range(NA_ROWS_PER_STEP):
        kw = key_window(j + half)
        if j < half:
            kw = jnp.where(i == 0, key_window(NA_ROWS_PER_STEP), kw)
        elif j > half:
            kw = jnp.where(i == last, key_window(NA_ROWS_PER_STEP), kw)
        sl_ref[j] = jnp.dot(qs[j * rows:(j + 1) * rows], kw, preferred_element_type=F32)
    for j in range(NA_ROWS_PER_STEP):
        d = window_row(j) - j - 1
        for c0 in range(0, rows, NA_SOFTMAX_ROWS):
            rs = slice(c0, c0 + NA_SOFTMAX_ROWS)
            ra = slice(j * rows + c0, j * rows + c0 + NA_SOFTMAX_ROWS)
            s_loc = sl_ref[j, rs, :] + bias_ref[0, d, rs, :]
            s_ctx = sc_ref[ra, :]
            m = jnp.maximum(jnp.max(s_loc, axis=1, keepdims=True), jnp.max(s_ctx, axis=1, keepdims=True))
            p_loc = jnp.exp2(s_loc - m)
            p_ctx = jnp.exp2(s_ctx - m)
            l = jnp.sum(p_loc, axis=1, keepdims=True) + jnp.sum(p_ctx, axis=1, keepdims=True)
            pl_ref[j, rs, :] = p_loc.astype(BF16)
            pc_ref[ra, :] = p_ctx.astype(pc_ref.dtype)
            li_ref[ra, :] = jnp.broadcast_to(1.0 / l, (NA_SOFTMAX_ROWS, 2 * NA_HEAD_DIM))
    for j in range(NA_ROWS_PER_STEP):
        start = pl.multiple_of(window_row(j) * GRID_W, GRID_W)
        vw = wv_ref[pl.ds(start, NA_WIN_ROWS * GRID_W), :]
        ol_ref[j * rows:(j + 1) * rows, :] = jnp.dot(pl_ref[j], vw, preferred_element_type=F32)
    o = (ol_ref[...] + jnp.dot(pc_ref[...], cv_ref[0], preferred_element_type=F32)) * li_ref[...]
    for j in range(NA_ROWS_PER_STEP):
        r0 = j * rows
        oj = jnp.where(first_head, o[r0:r0 + GRID_W], o[r0 + GRID_W:r0 + rows])
        o_ref[0, j * GRID_W:(j + 1) * GRID_W, :] = oj.astype(o_ref.dtype)


def _na(q, k, v, ck, cv, bias):
    B, L, W = q.shape
    Lc = ck.shape[1]
    blk = NA_ROWS_PER_STEP * GRID_W
    nblk = L // blk
    assert NA_ROWS_PER_STEP == NA_WIN_ROWS and nblk >= 2 and GRID_W * 2 == LANES
    pw = 2 * NA_HEAD_DIM
    cur = pl.BlockSpec((1, blk, pw), lambda b, h, i: (b, i, h))
    prev = pl.BlockSpec((1, blk, pw), lambda b, h, i: (b, jnp.maximum(i - 1, 0), h))
    nxt = pl.BlockSpec((1, blk, pw), lambda b, h, i: (b, jnp.minimum(i + 1, nblk - 1), h))
    kcur = pl.BlockSpec((1, pw, blk), lambda b, h, i: (b, h, i))
    kprev = pl.BlockSpec((1, pw, blk), lambda b, h, i: (b, h, jnp.maximum(i - 1, 0)))
    knxt = pl.BlockSpec((1, pw, blk), lambda b, h, i: (b, h, jnp.minimum(i + 1, nblk - 1)))
    cspec = pl.BlockSpec((1, Lc, pw), lambda b, h, i: (b, 0, h))
    bspec = pl.BlockSpec((1,) + bias.shape[1:], lambda b, h, i: (h, 0, 0, 0))
    stacked = 2 * blk
    return pl.pallas_call(
        _na_kernel,
        out_shape=jax.ShapeDtypeStruct((B, L, W), BF16),
        grid=(B, W // pw, nblk),
        in_specs=[cur, kprev, kcur, knxt, prev, cur, nxt, cspec, cspec, bspec],
        out_specs=cur,
        scratch_shapes=[pltpu.VMEM((pw, 3 * blk), BF16), pltpu.VMEM((pw, 3 * blk), BF16),
                        pltpu.VMEM((3 * blk, pw), BF16),
                        pltpu.VMEM((stacked, Lc), F32), pltpu.VMEM((stacked, Lc), BF16),
                        pltpu.VMEM((stacked, pw), F32), pltpu.VMEM((stacked, pw), F32),
                        pltpu.VMEM((NA_ROWS_PER_STEP, 2 * GRID_W, NA_WIN_ROWS * GRID_W), F32),
                        pltpu.VMEM((NA_ROWS_PER_STEP, 2 * GRID_W, NA_WIN_ROWS * GRID_W), BF16)],
        compiler_params=pltpu.CompilerParams(vmem_limit_bytes=VMEM_LIMIT),
        name="nattn",
    )(q, k, k, k, v, v, v, ck, cv, bias)


def _ctx_attn_kernel(q_ref, k_ref, v_ref, o_ref):
    q = q_ref[0]
    k = k_ref[0]
    v = v_ref[0]
    first_head = lax.broadcasted_iota(jnp.int32, q.shape, 1) < NA_HEAD_DIM
    outs = []
    for h in range(2):
        qm = jnp.where(first_head if h == 0 else jnp.logical_not(first_head), q, jnp.zeros_like(q))
        s = lax.dot_general(qm, k, _NT, preferred_element_type=F32)
        p = jnp.exp2(s - jnp.max(s, axis=1, keepdims=True))
        o = jnp.dot(p.astype(BF16), v, preferred_element_type=F32)
        outs.append(o / jnp.sum(p, axis=1, keepdims=True))
    o_ref[0] = jnp.where(first_head, outs[0], outs[1]).astype(o_ref.dtype)


def _ctx_attn(q, k, v):
    B, Lc, W = q.shape
    pw = 2 * NA_HEAD_DIM
    spec = pl.BlockSpec((1, Lc, pw), lambda b, h: (b, 0, h))
    return pl.pallas_call(
        _ctx_attn_kernel,
        out_shape=jax.ShapeDtypeStruct((B, Lc, W), BF16),
        grid=(B, W // pw),
        in_specs=[spec, spec, spec],
        out_specs=spec,
        name="ctx_attn",
    )(q, k, v)


LRU_CHUNK = 2048


def _lru_gate_weights(wa, ba, wi, bi, lam):
    def bd(w):
        nb, c, _ = w.shape
        blk_id = np.arange(nb * c) // c
        tiled = jnp.tile(w.reshape(nb * c, c), (1, nb))
        return jnp.where(jnp.asarray(blk_id[:, None] == blk_id[None, :]), tiled, 0.0)
    wg = jnp.stack([jnp.concatenate([bd(wa[d]), bd(wi[d])], axis=1) for d in range(2)]).astype(BF16)
    bg = jnp.stack([jnp.concatenate([ba[d], bi[d]])[None, :] for d in range(2)])
    return wg, bg, lam[:, None, :]


def _lru_coeffs(u, wg, bg, lam):
    C = u.shape[1]
    g = jnp.dot(u.astype(BF16), wg, preferred_element_type=F32) + bg
    sig = 0.5 + 0.5 * jnp.tanh(0.5 * g)
    r, ig = sig[:, :C], sig[:, C:]
    nl = -lam
    softplus = jnp.maximum(nl, 0.0) + jnp.log(1.0 + jnp.exp(-jnp.abs(nl)))
    log_a = (-LRU_C * softplus) * r
    a = jnp.exp(log_a)
    t = jnp.tanh(log_a)
    b = jnp.sqrt(-2.0 * t / (1.0 - t)) * (ig * u)
    return a, b


def _lru_scan(a, b, h0, reverse, ac_ref, bc_ref, h_ref):
    T, C = a.shape
    row = lax.broadcasted_iota(jnp.int32, a.shape, 0) % SUBLANES
    for s in (1, 2, 4):
        shift = T - s if reverse else s
        keep = (row < SUBLANES - s) if reverse else (row >= s)
        b = jnp.where(keep, a * pltpu.roll(b, shift, 0) + b, b)
        a = jnp.where(keep, a * pltpu.roll(a, shift, 0), a)
    ac_ref[...] = a
    bc_ref[...] = b
    ng = T // SUBLANES

    def group(g, h):
        r0 = pl.multiple_of((ng - 1 - g if reverse else g) * SUBLANES, SUBLANES)
        hr = ac_ref[pl.ds(r0, SUBLANES), :] * h + bc_ref[pl.ds(r0, SUBLANES), :]
        h_ref[pl.ds(r0, SUBLANES), :] = hr
        edge = hr[0:1] if reverse else hr[SUBLANES - 1:SUBLANES]
        return jnp.broadcast_to(edge, (SUBLANES, C))

    return lax.fori_loop(0, ng, group, h0, unroll=4)


def _gelu_tanh(x):
    return 0.5 * x * (1.0 + jnp.tanh(math.sqrt(2.0 / math.pi) * (x + 0.044715 * (x * x * x))))


def _lru_ctx_kernel(u_ref, xg_ref, wg_ref, bg_ref, lam_ref, hend_ref, yc_ref, ac_ref, bc_ref, h_ref):
    u = u_ref[0]
    C = u.shape[1]
    total = jnp.zeros_like(u)
    for d, rev in enumerate((False, True)):
        a, b = _lru_coeffs(u, wg_ref[d], bg_ref[d], lam_ref[d])
        hl = _lru_scan(a, b, jnp.zeros((SUBLANES, C), F32), rev, ac_ref, bc_ref, h_ref)
        hend_ref[0, d:d + 1, :] = hl[0:1]
        total = total + h_ref[...]
    yc_ref[0] = (total * _gelu_tanh(xg_ref[0])).astype(yc_ref.dtype)


def _lru_ctx(u, xg, wg, bg, lam):
    B, Lc, C = u.shape
    tok = pl.BlockSpec((1, Lc, C), lambda b: (b, 0, 0))
    full = lambda a: pl.BlockSpec(a.shape, lambda b: (0,) * a.ndim)
    return pl.pallas_call(
        _lru_ctx_kernel,
        out_shape=[jax.ShapeDtypeStruct((B, 2, C), F32), jax.ShapeDtypeStruct((B, Lc, C), BF16)],
        grid=(B,),
        in_specs=[tok, tok, full(wg), full(bg), full(lam)],
        out_specs=[pl.BlockSpec((1, 2, C), lambda b: (b, 0, 0)), tok],
        scratch_shapes=[pltpu.VMEM((Lc, C), F32)] * 3,
        name="lru_ctx",
    )(u, xg, wg, bg, lam)


def _lru_dir_kernel(*refs, d, reverse):
    if reverse:
        u_ref, hend_ref, wg_ref, bg_ref, lam_ref, hf_ref, xg_ref, o_ref, ac_ref, bc_ref, h_ref, carry_ref = refs
    else:
        u_ref, hend_ref, wg_ref, bg_ref, lam_ref, o_ref, ac_ref, bc_ref, carry_ref = refs
        h_ref = o_ref.at[0]
    C = u_ref.shape[2]

    @pl.when(pl.program_id(1) == 0)
    def _():
        carry_ref[...] = jnp.broadcast_to(hend_ref[0, d:d + 1, :], (SUBLANES, C))

    a, b = _lru_coeffs(u_ref[0], wg_ref[d], bg_ref[d], lam_ref[d])
    carry_ref[...] = _lru_scan(a, b, carry_ref[...], reverse, ac_ref, bc_ref, h_ref)
    if reverse:
        o_ref[0] = ((hf_ref[0] + h_ref[...]) * _gelu_tanh(xg_ref[0])).astype(o_ref.dtype)


def _lru_dir(u, hend, wg, bg, lam, hf=None, xg=None):
    B, L, C = u.shape
    reverse = hf is not None
    T = LRU_CHUNK
    nb = L // T
    tok = pl.BlockSpec((1, T, C), (lambda b, i: (b, nb - 1 - i, 0)) if reverse else (lambda b, i: (b, i, 0)))
    full = lambda a: pl.BlockSpec(a.shape, lambda b, i: (0,) * a.ndim)
    ins = [u, hend, wg, bg, lam] + ([hf, xg] if reverse else [])
    specs = [tok, pl.BlockSpec((1, 2, C), lambda b, i: (b, 0, 0)), full(wg), full(bg), full(lam)]
    specs += [tok, tok] if reverse else []
    scratch = [pltpu.VMEM((T, C), F32)] * (3 if reverse else 2) + [pltpu.VMEM((SUBLANES, C), F32)]
    return pl.pallas_call(
        functools.partial(_lru_dir_kernel, d=int(reverse), reverse=reverse),
        out_shape=jax.ShapeDtypeStruct((B, L, C), BF16 if reverse else F32),
        grid=(B, nb),
        in_specs=specs,
        out_specs=tok,
        scratch_shapes=scratch,
        compiler_params=pltpu.CompilerParams(dimension_semantics=("arbitrary", "arbitrary")),
        name="lru_bwd" if reverse else "lru_fwd",
    )(*ins)


def _lru(u, xg, u_c, xg_c, wa, ba, wi, bi, lam):
    wg, bg, lam3 = _lru_gate_weights(wa, ba, wi, bi, lam)
    hend, yc = _lru_ctx(u_c, xg_c, wg, bg, lam3)
    hf = _lru_dir(u, hend, wg, bg, lam3)
    return _lru_dir(u, hend, wg, bg, lam3, hf, xg), yc


def kernel(x, c, ctx, c_ctx, ada_w, ada_b, g_mix_pre, g_mix_post, g_ffn_pre, g_ffn_post, w_in, w_out, hy_conv_w,
           hy_conv_b, hy_f_w1, hy_f_b1, hy_f_w2, hy_f_b2, hy_f_w3, hy_f_freq, hy_bias, na_rpb, lru_conv_w,
           lru_conv_b, lru_wa, lru_ba, lru_wi, lru_bi, lru_lam, ffn_w_gu, ffn_w_down):
    B, L, D = x.shape
    Lc = ctx.shape[1]
    tm = 1024

    assert B + 1 <= SUBLANES
    cond_t = jnp.zeros((D, SUBLANES), F32).at[:, 0:B].set(c.T).at[:, B].set(c_ctx)
    mods = _modulation(cond_t, B + 1, ada_w, ada_b)

    xc = ctx
    for l in range(DEPTH):
        with_ctx_out = l < DEPTH - 1
        m = mods[l].reshape(8, 6, D)
        lat = [m[0:B, j][:, None, :] for j in range(6)]
        cx = [jnp.broadcast_to(m[B, j][None, None, :], (B, 1, D)) for j in range(6)]
        row = lambda a: a.reshape(1, -1)

        w_in_bf = w_in[l].astype(BF16)
        w_out_bf = w_out[l].astype(BF16)
        wgu = ffn_w_gu[l].astype(BF16)
        wd = ffn_w_down[l].astype(BF16)

        conv_args = (hy_conv_w[l], row(hy_conv_b[l]), lru_conv_w[l], row(lru_conv_b[l]))
        hv, hx1, hx2, q, k, v, lu, lg = _inproj(x, lat[1], lat[0], row(g_mix_pre[l]), w_in_bf, *conv_args, tm=tm,
                                                chan_major=True)
        cv, cx1, cx2, cq, ck, cvv, clu, clg = _inproj(xc, cx[1], cx[0], row(g_mix_pre[l]), w_in_bf, *conv_args,
                                                      tm=Lc, chan_major=False)

        filt = (hy_f_w1[l], hy_f_b1[l], hy_f_w2[l], hy_f_b2[l], hy_f_w3[l], hy_f_freq[l])
        y_hy = _hyena(hv, hx1, hx2, filt, hy_bias[l])
        y_na = _na(q, k, v, ck, cvv, _na_bias_table(na_rpb[l]))
        y_lru, yc_lru = _lru(lu, lg, clu, clg, lru_wa[l], lru_ba[l], lru_wi[l], lru_bi[l], lru_lam[l])

        ffn_args = (row(g_ffn_pre[l]), row(g_ffn_post[l]), wgu, wd)
        x = _out_ffn(y_hy, y_na, y_lru, w_out_bf, x, row(g_mix_post[l]), lat[2], lat[4], lat[3], lat[5], *ffn_args,
                     tm=tm, chan_major=True)

        if with_ctx_out:
            yc_hy = _hyena_ctx(cv, cx1, cx2, filt, hy_bias[l])
            yc_na = _ctx_attn(cq, ck, cvv)
            xc = _out_ffn(yc_hy, yc_na, yc_lru, w_out_bf, xc, row(g_mix_post[l]), cx[2], cx[4], cx[3], cx[5],
                          *ffn_args, tm=Lc, chan_major=False)
    return x
```

```python
import functools
import math

import jax
import jax.numpy as jnp
import numpy as np
from jax import lax
from jax.experimental import pallas as pl
from jax.experimental.pallas import tpu as pltpu

F32 = jnp.float32
BF16 = jnp.bfloat16

D_MODEL = 1024
DEPTH = 2
GRID_W = 64
HY_WIDTH = D_MODEL // 4
NA_HEAD_DIM = 64
NA_WIDTH = D_MODEL // 2
NA_HEADS = NA_WIDTH // NA_HEAD_DIM
LRU_WIDTH = D_MODEL // 4
HY_ORDER = 2
HY_BANDS = 16
HY_FAST_DECAY = 0.3
HY_SLOW_DECAY = 1.5
HY_DECAY_TARGET = 1e-2
NA_WIN_ROWS = 8
NA_WIN_COLS = 16
LRU_C = 8.0
D_FF = -(-8 * D_MODEL // (3 * 256)) * 256
RMS_EPS = 1e-6
NEG_INF = -1e30
LOG2E = math.log2(math.e)

_HY_END = 3 * HY_WIDTH
_NA_END = _HY_END + 3 * NA_WIDTH
_LRU_MID = _NA_END + LRU_WIDTH

HALO = 8
SUBLANES = 8
VMEM_LIMIT = 56 * 1024 * 1024


def _rms(x, g):
    return x * lax.rsqrt(jnp.mean(x * x, axis=-1, keepdims=True) + RMS_EPS) * g


def _mod_kernel(ct_ref, w_ref, b_ref, o_ref, *, n_cond):
    ct = ct_ref[...]
    st = ct * jax.nn.sigmoid(ct)
    w = w_ref[0]
    rows = [jnp.sum(w * st[:, r:r + 1], axis=0, keepdims=True) for r in range(n_cond)]
    rows.append(jnp.zeros((SUBLANES - n_cond, w.shape[1]), F32))
    o_ref[0] = jnp.concatenate(rows, axis=0) + b_ref[0]


def _modulation(cond_t, n_cond, ada_w, ada_b):
    tn = 768
    n = ada_w.shape[-1]
    return pl.pallas_call(
        functools.partial(_mod_kernel, n_cond=n_cond),
        out_shape=jax.ShapeDtypeStruct((DEPTH, SUBLANES, n), F32),
        grid=(DEPTH, n // tn),
        in_specs=[pl.BlockSpec((D_MODEL, SUBLANES), lambda l, j: (0, 0)),
                  pl.BlockSpec((1, D_MODEL, tn), lambda l, j: (l, 0, j)),
                  pl.BlockSpec((1, 1, tn), lambda l, j: (l, 0, j))],
        out_specs=pl.BlockSpec((1, SUBLANES, tn), lambda l, j: (l, 0, j)),
        compiler_params=pltpu.CompilerParams(vmem_limit_bytes=VMEM_LIMIT),
        name="adaln_modulation",
    )(cond_t, ada_w, ada_b.reshape(DEPTH, 1, n))


CM_LANES = 256
LANES = 128
CM_HALVES = CM_LANES // LANES


def _chan_major_shape(B, L, C):
    return (B, C // SUBLANES, CM_HALVES, L // CM_LANES * SUBLANES, LANES)


def _chan_major_spec(tm, C):
    return pl.BlockSpec((1, C // SUBLANES, CM_HALVES, tm // CM_LANES * SUBLANES, LANES),
                        lambda b, i: (b, 0, 0, i, 0))


def _store_chan_major(ref, u):
    ut = u.T
    for g in range(u.shape[1] // SUBLANES):
        for j in range(u.shape[0] // CM_LANES):
            for h in range(CM_HALVES):
                lane0 = j * CM_LANES + h * LANES
                ref[0, g, h, j * SUBLANES:(j + 1) * SUBLANES, :] = ut[g * SUBLANES:(g + 1) * SUBLANES,
                                                                      lane0:lane0 + LANES]


def _load_chan_major(ref):
    _, ng, _, nr, _ = ref.shape
    rows = [jnp.concatenate([ref[0, g, h, j * SUBLANES:(j + 1) * SUBLANES, :]
                             for j in range(nr // SUBLANES) for h in range(CM_HALVES)], axis=1) for g in range(ng)]
    return jnp.concatenate(rows, axis=0).T


def _chan_rows(ref, lead, c, n):
    return jnp.concatenate([ref.at[lead + (h,)][pl.ds(c, n, stride=SUBLANES), :] for h in range(CM_HALVES)], axis=1)


def _inproj_kernel(xp_ref, xc_ref, xn_ref, sc_ref, sh_ref, g_ref, w_ref, hcw_ref, hcb_ref, lcw_ref, lcb_ref,
                   hv_ref, hx1_ref, hx2_ref, q_ref, k_ref, v_ref, lu_ref, lg_ref, pe_ref, *, tm, chan_major):
    i = pl.program_id(1)
    last = pl.num_programs(1) - 1
    g = g_ref[...]
    sc1 = 1.0 + sc_ref[0]
    sh = sh_ref[0]

    def norm_mod(xv):
        return _rms(xv, g) * sc1 + sh

    hp = norm_mod(xp_ref[0]) * (i > 0).astype(F32)
    hn = norm_mod(xn_ref[0]) * (i < last).astype(F32)
    he = jnp.concatenate([hp, norm_mod(xc_ref[0]), hn], axis=0).astype(BF16)

    pe_ref[:, 0:_HY_END] = jnp.dot(he, w_ref[:, 0:_HY_END], preferred_element_type=F32)
    pe_ref[:, _HY_END:] = jnp.dot(he, w_ref[:, _NA_END:_LRU_MID], preferred_element_type=F32)
    hc = he[HALO:HALO + tm]
    qkv = jnp.dot(hc, w_ref[:, _HY_END:_NA_END], preferred_element_type=F32)
    q_ref[0] = (qkv[:, 0:NA_WIDTH] * (NA_HEAD_DIM ** -0.5 * LOG2E)).astype(BF16)
    kk = qkv[:, NA_WIDTH:2 * NA_WIDTH]
    k_ref[0] = (kk.T if chan_major else kk).astype(BF16)
    v_ref[0] = qkv[:, 2 * NA_WIDTH:].astype(BF16)
    lg_ref[0] = jnp.dot(hc, w_ref[:, _LRU_MID:], preferred_element_type=F32)

    u = hcb_ref[...]
    for kk in range(3):
        u = u + hcw_ref[kk:kk + 1, :] * pe_ref[pl.ds(HALO - 1 + kk, tm), 0:_HY_END]
    for n, ref in enumerate((hv_ref, hx1_ref, hx2_ref)):
        un = u[:, n * HY_WIDTH:(n + 1) * HY_WIDTH]
        if chan_major:
            _store_chan_major(ref, un)
        else:
            ref[0] = un
    ul = lcb_ref[...]
    for kk in range(4):
        ul = ul + lcw_ref[kk:kk + 1, :] * pe_ref[pl.ds(HALO - 2 + kk, tm), _HY_END:]
    lu_ref[0] = ul


def _inproj(x, sc, sh, g, w_bf, hcw, hcb, lcw, lcb, *, tm, chan_major):
    B, L, D = x.shape
    nb = tm // HALO
    nh = L // HALO
    tok = lambda w, dt: jax.ShapeDtypeStruct((B, L, w), dt)
    tspec = lambda w: pl.BlockSpec((1, tm, w), lambda b, i: (b, i, 0))
    full = lambda a: pl.BlockSpec(a.shape, lambda b, i: (0,) * a.ndim)
    vec = pl.BlockSpec((1, 1, D), lambda b, i: (b, 0, 0))
    if chan_major:
        hy_shape = jax.ShapeDtypeStruct(_chan_major_shape(B, L, HY_WIDTH), F32)
        hy_spec = _chan_major_spec(tm, HY_WIDTH)
        k_shape = jax.ShapeDtypeStruct((B, NA_WIDTH, L), BF16)
        k_spec = pl.BlockSpec((1, NA_WIDTH, tm), lambda b, i: (b, 0, i))
    else:
        hy_shape, hy_spec = tok(HY_WIDTH, F32), tspec(HY_WIDTH)
        k_shape, k_spec = tok(NA_WIDTH, BF16), tspec(NA_WIDTH)
    return pl.pallas_call(
        functools.partial(_inproj_kernel, tm=tm, chan_major=chan_major),
        out_shape=[hy_shape] * 3 + [tok(NA_WIDTH, BF16), k_shape, tok(NA_WIDTH, BF16)] + [tok(LRU_WIDTH, F32)] * 2,
        grid=(B, L // tm),
        in_specs=[pl.BlockSpec((1, HALO, D), lambda b, i: (b, jnp.maximum(i * nb - 1, 0), 0)),
                  pl.BlockSpec((1, tm, D), lambda b, i: (b, i, 0)),
                  pl.BlockSpec((1, HALO, D), lambda b, i: (b, jnp.minimum((i + 1) * nb, nh - 1), 0)),
                  vec, vec, full(g), pl.BlockSpec(w_bf.shape, lambda b, i: (0, 0), pipeline_mode=pl.Buffered(1)),
                  full(hcw), full(hcb), full(lcw), full(lcb)],
        out_specs=[hy_spec] * 3 + [tspec(NA_WIDTH), k_spec, tspec(NA_WIDTH)] + [tspec(LRU_WIDTH)] * 2,
        scratch_shapes=[pltpu.VMEM((tm + 2 * HALO, _HY_END + LRU_WIDTH), F32)],
        compiler_params=pltpu.CompilerParams(vmem_limit_bytes=VMEM_LIMIT),
        name="inproj",
    )(x, x, x, sc, sh, g, w_bf, hcw, hcb, lcw, lcb)


FF_CHUNK = 256


def _out_ffn_kernel(yh_ref, yn_ref, yl_ref, wo_ref, x_ref, gm_ref, gtm_ref, sc_ref, sh_ref, gt_ref, gpre_ref,
                    gpost_ref, wgu_ref, wd_ref, o_ref, x1_ref, h_ref, *, chan_major, halves):
    yh = (_load_chan_major(yh_ref) if chan_major else yh_ref[0]).astype(BF16)
    tm = x_ref.shape[1]
    for r0 in range(0, tm, tm // halves):
        rs = slice(r0, r0 + tm // halves)
        y = jnp.dot(yh[rs], wo_ref[0:HY_WIDTH], preferred_element_type=F32)
        y = y + jnp.dot(yn_ref[0, rs, :], wo_ref[HY_WIDTH:HY_WIDTH + NA_WIDTH], preferred_element_type=F32)
        y = y + jnp.dot(yl_ref[0, rs, :], wo_ref[HY_WIDTH + NA_WIDTH:], preferred_element_type=F32)
        x = x_ref[0, rs, :] + gtm_ref[0] * _rms(y, gm_ref[...])
        x1_ref[rs, :] = x
        h_ref[rs, :] = (_rms(x, gpre_ref[...]) * (1.0 + sc_ref[0]) + sh_ref[0]).astype(BF16)
    for r0 in range(0, tm, tm // halves):
        rs = slice(r0, r0 + tm // halves)
        h = h_ref[rs, :]
        acc = jnp.zeros((tm // halves, x_ref.shape[2]), F32)
        for c0 in range(0, D_FF, FF_CHUNK):
            g = jnp.dot(h, wgu_ref[:, c0:c0 + FF_CHUNK], preferred_element_type=F32)
            u = jnp.dot(h, wgu_ref[:, D_FF + c0:D_FF + c0 + FF_CHUNK], preferred_element_type=F32)
            a = (g * jax.nn.sigmoid(g) * u).astype(BF16)
            acc = acc + jnp.dot(a, wd_ref[c0:c0 + FF_CHUNK, :], preferred_element_type=F32)
        o_ref[0, rs, :] = x1_ref[rs, :] + gt_ref[0] * _rms(acc, gpost_ref[...])


def _out_ffn(yh, yn, yl, wo, x, gm, gtm, sc, sh, gt, gpre, gpost, wgu, wd, *, tm, chan_major):
    B, L, D = x.shape
    tspec = lambda w: pl.BlockSpec((1, tm, w), lambda b, i: (b, i, 0))
    vec = pl.BlockSpec((1, 1, D), lambda b, i: (b, 0, 0))
    full = lambda a: pl.BlockSpec(a.shape, lambda b, i: (0,) * a.ndim)
    res = lambda a: pl.BlockSpec(a.shape, lambda b, i: (0,) * a.ndim, pipeline_mode=pl.Buffered(1))
    hy_spec = _chan_major_spec(tm, HY_WIDTH) if chan_major else tspec(HY_WIDTH)
    return pl.pallas_call(
        functools.partial(_out_ffn_kernel, chan_major=chan_major, halves=2 if tm >= 1024 else 1),
        out_shape=jax.ShapeDtypeStruct((B, L, D), F32),
        grid=(B, L // tm),
        in_specs=[hy_spec, tspec(NA_WIDTH), tspec(LRU_WIDTH), res(wo), tspec(D), full(gm), vec,
                  vec, vec, vec, full(gpre), full(gpost), res(wgu), res(wd)],
        out_specs=tspec(D),
        scratch_shapes=[pltpu.VMEM((tm, D), F32), pltpu.VMEM((tm, D), BF16)],
        compiler_params=pltpu.CompilerParams(vmem_limit_bytes=VMEM_LIMIT),
        name="out_ffn",
    )(yh, yn, yl, wo, x, gm, gtm, sc, sh, gt, gpre, gpost, wgu, wd)


HY_N1 = CM_LANES
HY_N2 = 128
HY_FILT_COLS = 2048


def _block_cplx(re, im):
    return np.block([[re, -im], [im, re]])


@functools.lru_cache(maxsize=None)
def _dft_consts():
    n = HY_N1 * HY_N2
    h = HY_N2 // 2
    i1 = np.arange(HY_N1, dtype=np.float64)
    i2 = np.arange(HY_N2, dtype=np.float64)
    a1 = 2.0 * np.pi * np.outer(i1, i1) / HY_N1
    a2 = 2.0 * np.pi * np.outer(i2, i2) / HY_N2
    at = 2.0 * np.pi * np.outer(i2, i1) / n
    r2, m2 = np.cos(a2), -np.sin(a2)
    r1, m1 = np.cos(a1), -np.sin(a1)
    f32 = lambda m: np.asarray(m, np.float32)
    return dict(
        rows_data=f32(_block_cplx(r2[:, :h], m2[:, :h])),
        rows_filt=f32(np.concatenate([r2, m2], axis=0)),
        rows_out=f32(_block_cplx(r2[:h, :], m2[:h, :])),
        lanes=f32(np.block([[r1, m1], [-m1, r1]])),
        twr=f32(np.cos(at)), twi=f32(-np.sin(at)))


def _dft_const(name):
    return jnp.asarray(_dft_consts()[name], F32)


def _cmul(ar, ai, br, bi):
    return ar * br - ai * bi, ar * bi + ai * br


def _hy_conv_kernel(v_ref, x1_ref, x2_ref, k0_ref, k1_ref, ks0_ref, ks1_ref, bias_ref, rd_ref, rf_ref, ro_ref,
                    ln_ref, twr_ref, twi_ref, o_ref, st_ref, ks_ref):
    g = pl.program_id(0)
    n_ch = SUBLANES
    h = HY_N2 // 2
    rows_data = rd_ref[...].astype(BF16)
    rows_filt = rf_ref[...].astype(BF16)
    rows_out = ro_ref[...].astype(BF16)
    lanes = ln_ref[...].astype(BF16)
    twr, twi = twr_ref[...], twi_ref[...]

    def rows_then_twiddle(mat, x, c):
        a = jnp.dot(mat, x.astype(BF16), preferred_element_type=F32)
        ar, ai = _cmul(a[:HY_N2], a[HY_N2:], twr, twi)
        st_ref[c * HY_N2:(c + 1) * HY_N2, 0:HY_N1] = ar.astype(BF16)
        st_ref[c * HY_N2:(c + 1) * HY_N2, HY_N1:] = ai.astype(BF16)

    for o, (k_ref, s_ref) in enumerate(((k0_ref, ks0_ref), (k1_ref, ks1_ref))):
        for c in range(n_ch):
            rows_then_twiddle(rows_filt, _chan_rows(k_ref, (0,), c, HY_N2), c)
        spec = jnp.dot(st_ref[...], lanes, preferred_element_type=F32)
        inv = 1.0 / (jnp.sum(s_ref[...], axis=1, keepdims=True) * float(HY_N1 * HY_N2))
        for c in range(n_ch):
            ks_ref[o, c * HY_N2:(c + 1) * HY_N2, :] = spec[c * HY_N2:(c + 1) * HY_N2] * inv[c:c + 1, :]

    z = [[_chan_rows(v_ref, (b, 0), c, h) for c in range(n_ch)] for b in range(2)]
    for o, gate_ref in enumerate((x1_ref, x2_ref)):
        for c in range(n_ch):
            rows_then_twiddle(rows_data, jnp.concatenate([z[0][c], z[1][c]], axis=0), c)
        x = jnp.dot(st_ref[...], lanes, preferred_element_type=F32)
        pr, pi = _cmul(x[:, :HY_N1], x[:, HY_N1:], ks_ref[o, :, 0:HY_N1], ks_ref[o, :, HY_N1:])
        st_ref[:, 0:HY_N1] = pr.astype(BF16)
        st_ref[:, HY_N1:] = (-pi).astype(BF16)
        y = jnp.dot(st_ref[...], lanes, preferred_element_type=F32)
        for c in range(n_ch):
            yc = y[c * HY_N2:(c + 1) * HY_N2]
            yr, yi = _cmul(yc[:, :HY_N1], yc[:, HY_N1:], twr, twi)
            out = jnp.dot(rows_out, jnp.concatenate([yr, yi], axis=0).astype(BF16),
                          preferred_element_type=F32)
            bias = bias_ref[o, g * n_ch + c]
            for b, conv in ((0, out[:h]), (1, -out[h:])):
                z[b][c] = _chan_rows(gate_ref, (b, 0), c, h) * (conv + bias * z[b][c])
    for b in range(2):
        for c in range(n_ch):
            for hh in range(CM_HALVES):
                o_ref.at[b, 0, hh][pl.ds(c, h, stride=SUBLANES), :] = z[b][c][:, hh * LANES:(hh + 1) * LANES]


def _hy_conv(v, x1, x2, kc, ksum, hy_bias):
    B, ng = v.shape[0], v.shape[1]
    assert B == 2 and v.shape[3] * 2 == HY_N2 * SUBLANES and kc.shape[2] == HY_N2 * SUBLANES
    consts = [_dft_const(n) for n in ("rows_data", "rows_filt", "rows_out", "lanes", "twr", "twi")]
    tok = pl.BlockSpec((B, 1) + v.shape[2:], lambda g: (0, g, 0, 0, 0))
    kspec = lambda o: pl.BlockSpec((1,) + kc.shape[1:], lambda g: (o * ng + g, 0, 0, 0))
    sspec = lambda o: pl.BlockSpec((SUBLANES, ksum.shape[1]), lambda g: (o * ng + g, 0))
    full = lambda a: pl.BlockSpec(a.shape, lambda g: (0,) * a.ndim)
    return pl.pallas_call(
        _hy_conv_kernel,
        out_shape=jax.ShapeDtypeStruct(v.shape, F32),
        grid=(ng,),
        in_specs=[tok, tok, tok, kspec(0), kspec(1), sspec(0), sspec(1),
                  pl.BlockSpec(memory_space=pltpu.SMEM)] + [full(a) for a in consts],
        out_specs=tok,
        scratch_shapes=[pltpu.VMEM((SUBLANES * HY_N2, 2 * HY_N1), BF16),
                        pltpu.VMEM((HY_ORDER, SUBLANES * HY_N2, 2 * HY_N1), F32)],
        compiler_params=pltpu.CompilerParams(vmem_limit_bytes=VMEM_LIMIT),
        name="hy_conv",
    )(v, x1, x2, kc, kc, ksum, ksum, hy_bias, *consts)


def _hy_filter_kernel(w1t_ref, w1cs_ref, trig_ref, b1_ref, f0_ref, w2_ref, b2_ref, f1_ref, w3_ref, dl_ref,
                      k_ref, s_ref, *, L, P, chan_major):
    i = pl.program_id(0)

    def pos_t(rows):
        n = i * P + lax.broadcasted_iota(jnp.int32, (rows, P), 1)
        return n, jnp.where(n < L, n, 2 * L - n).astype(F32) / L

    _, t1 = pos_t(1)
    h = w1t_ref[...] * t1 + b1_ref[...]
    h = h + jnp.dot(w1cs_ref[...], trig_ref[...].astype(BF16), preferred_element_type=F32)
    h = jnp.sin(f0_ref[...] * h)
    h = jnp.sin(f1_ref[...] * (jnp.dot(w2_ref[...], h.astype(BF16), preferred_element_type=F32) + b2_ref[...]))
    k = jnp.dot(w3_ref[0], h.astype(BF16), preferred_element_type=F32)
    n, t = pos_t(1)
    k = jnp.where(n == L, 0.0, k * jnp.exp(-t * dl_ref[...]))
    if chan_major:
        for j in range(P // CM_LANES):
            for hh in range(CM_HALVES):
                lane0 = j * CM_LANES + hh * LANES
                k_ref[:, hh, j * SUBLANES:(j + 1) * SUBLANES, :] = k[:, lane0:lane0 + LANES].reshape(
                    k_ref.shape[0], SUBLANES, LANES)
    else:
        k_ref[...] = k
    ka = jnp.abs(k)
    tot = ka[:, 0:LANES]
    for j in range(1, P // LANES):
        tot = tot + ka[:, j * LANES:(j + 1) * LANES]

    @pl.when(i == 0)
    def _():
        s_ref[...] = jnp.zeros_like(s_ref)

    s_ref[...] += tot


@functools.lru_cache(maxsize=None)
def _hy_trig_features(L):
    n = np.arange(2 * L)
    t = np.where(n < L, n, 2 * L - n).astype(np.float64) / L
    ang = 2.0 * np.pi * t[None, :] * np.arange(1, HY_BANDS + 1, dtype=np.float64)[:, None]
    return np.asarray(np.concatenate([np.cos(ang), np.sin(ang)], axis=0), np.float32)


def _hy_filter(L, P, chan_major, f_w1, f_b1, f_w2, f_b2, f_w3, f_freq):
    nb = 2 * L // P
    assert nb % 2 == 0 and P % CM_LANES == 0
    hid = f_w2.shape[0]
    oc = HY_ORDER * HY_WIDTH
    col = lambda v: v.reshape(-1, 1)
    w3 = f_w3.reshape(hid, HY_ORDER, 2, HY_WIDTH)
    w3 = jnp.stack([w3[:, :, d].reshape(hid, oc).T for d in range(2)]).astype(BF16)
    deltas = np.abs(np.linspace(math.log(HY_DECAY_TARGET) / HY_FAST_DECAY,
                                math.log(HY_DECAY_TARGET) / HY_SLOW_DECAY, HY_WIDTH, dtype=np.float32))
    ins = [col(f_w1[0]), f_w1[1:].T.astype(BF16), jnp.asarray(_hy_trig_features(L)), col(f_b1),
           col(f_freq[0]), f_w2.T.astype(BF16), col(f_b2), col(f_freq[1]), w3,
           jnp.asarray(np.tile(deltas, HY_ORDER)[:, None])]
    full = lambda a: pl.BlockSpec(a.shape, lambda i: (0,) * a.ndim)
    specs = [full(a) for a in ins]
    specs[2] = pl.BlockSpec((2 * HY_BANDS, P), lambda i: (0, i))
    specs[8] = pl.BlockSpec((1,) + w3.shape[1:], lambda i: (i // (nb // 2), 0, 0))
    if chan_major:
        kshape = (oc // SUBLANES, CM_HALVES, 2 * L // CM_LANES * SUBLANES, LANES)
        kspec = pl.BlockSpec((oc // SUBLANES, CM_HALVES, P // CM_LANES * SUBLANES, LANES), lambda i: (0, 0, i, 0))
    else:
        kshape = (oc, 2 * L)
        kspec = pl.BlockSpec((oc, P), lambda i: (0, i))
    return pl.pallas_call(
        functools.partial(_hy_filter_kernel, L=L, P=P, chan_major=chan_major),
        out_shape=[jax.ShapeDtypeStruct(kshape, F32), jax.ShapeDtypeStruct((oc, LANES), F32)],
        grid=(nb,),
        in_specs=specs,
        out_specs=[kspec, pl.BlockSpec((oc, LANES), lambda i: (0, 0))],
        compiler_params=pltpu.CompilerParams(dimension_semantics=("arbitrary",), vmem_limit_bytes=VMEM_LIMIT),
        name="hy_filter",
    )(*ins)


def _hyena(v, x1, x2, filt, hy_bias):
    L = v.shape[3] // SUBLANES * CM_LANES
    assert 2 * L == HY_N1 * HY_N2
    kc, ksum = _hy_filter(L, HY_FILT_COLS, True, *filt)
    return _hy_conv(v, x1, x2, kc, ksum, hy_bias)


@functools.lru_cache(maxsize=None)
def _ctx_dft_consts(n):
    i = np.arange(n, dtype=np.float64)
    a = 2.0 * np.pi * np.outer(i, i) / n
    f = np.concatenate([np.cos(a), -np.sin(a)], axis=0)
    return np.asarray(f, np.float32), np.asarray(f.T[:n // 2], np.float32)


def _hyena_ctx_kernel(v_ref, x1_ref, x2_ref, kc_ref, ks_ref, bias_ref, ff_ref, fi_ref, o_ref):
    n = ff_ref.shape[1]
    lc = n // 2
    C = v_ref.shape[2]
    ff = ff_ref[...].astype(BF16)
    inv = 1.0 / (jnp.sum(ks_ref[...], axis=1, keepdims=True) * float(n))
    kn = (kc_ref[...] * inv).T
    kspec = jnp.dot(ff, kn.astype(BF16), preferred_element_type=F32)
    z = jnp.concatenate([v_ref[0], v_ref[1]], axis=1)
    for o, g_ref in enumerate((x1_ref, x2_ref)):
        x = jnp.dot(ff[:, :lc], z.astype(BF16), preferred_element_type=F32)
        k = kspec[:, o * C:(o + 1) * C]
        kr = jnp.concatenate([k[:n], k[:n]], axis=1)
        ki = jnp.concatenate([k[n:], k[n:]], axis=1)
        yr, yi = _cmul(x[:n], x[n:], kr, ki)
        y = jnp.concatenate([yr, yi], axis=0)
        conv = jnp.dot(fi_ref[...].astype(BF16), y.astype(BF16), preferred_element_type=F32)
        bias = jnp.concatenate([bias_ref[o:o + 1, :], bias_ref[o:o + 1, :]], axis=1)
        z = jnp.concatenate([g_ref[0], g_ref[1]], axis=1) * (conv + bias * z)
    o_ref[0] = z[:, :C]
    o_ref[1] = z[:, C:]


def _hyena_ctx(v, x1, x2, filt, hy_bias):
    B, Lc, C = v.shape
    assert B == 2
    kc, ksum = _hy_filter(Lc, Lc, False, *filt)
    ff, fi = (jnp.asarray(m, F32) for m in _ctx_dft_consts(2 * Lc))
    return pl.pallas_call(
        _hyena_ctx_kernel,
        out_shape=jax.ShapeDtypeStruct((B, Lc, C), F32),
        compiler_params=pltpu.CompilerParams(vmem_limit_bytes=VMEM_LIMIT),
        name="hyena_ctx",
    )(v, x1, x2, kc, ksum, hy_bias, ff, fi)


NA_ROWS_PER_STEP = 8
NA_SOFTMAX_ROWS = 32
_NT = (((1,), (1,)), ((), ()))


def _na_bias_table(rpb):
    qc = np.arange(GRID_W)[:, None]
    kc = np.arange(GRID_W)[None, :]
    start = np.clip(qc - NA_WIN_COLS // 2, 0, GRID_W - NA_WIN_COLS)
    valid = (kc >= start) & (kc < start + NA_WIN_COLS)
    pad = jnp.pad(rpb, ((0, 0), (0, 0), (GRID_W, GRID_W)))
    shift = GRID_W + NA_WIN_COLS - 1
    toep = jnp.stack([pad[:, :, shift - c:shift - c + GRID_W] for c in range(GRID_W)], axis=1)
    full = jnp.where(jnp.asarray(valid)[None, :, None, :], toep * LOG2E, NEG_INF)
    full = full.reshape(NA_HEADS // 2, 2, GRID_W, 2 * NA_WIN_ROWS - 1, GRID_W)
    t = jnp.stack([full[:, :, :, d:d + NA_WIN_ROWS] for d in range(NA_WIN_ROWS)], axis=1)
    return t.reshape(NA_HEADS // 2, NA_WIN_ROWS, 2 * GRID_W, NA_WIN_ROWS * GRID_W).astype(F32)


def _na_kernel(q_ref, kp_ref, kc_ref, kn_ref, vp_ref, vc_ref, vn_ref, ck_ref, cv_ref, bias_ref, o_ref,
               wk_ref, ws_ref, wv_ref, sc_ref, pc_ref, ol_ref, li_ref, sl_ref, pl_ref):
    i = pl.program_id(2)
    last = pl.num_programs(2) - 1
    blk = NA_ROWS_PER_STEP * GRID_W
    for n, (kr, vr) in enumerate(((kp_ref, vp_ref), (kc_ref, vc_ref), (kn_ref, vn_ref))):
        wk_ref[:, n * blk:(n + 1) * blk] = kr[0]
        wv_ref[n * blk:(n + 1) * blk] = vr[0]
    ws_ref[:, 0:3 * blk - GRID_W] = wk_ref[:, GRID_W:3 * blk]

    def key_window(off):
        if off % 2 == 0:
            return wk_ref[:, off * GRID_W:(off + NA_WIN_ROWS) * GRID_W]
        return ws_ref[:, (off - 1) * GRID_W:(off - 1 + NA_WIN_ROWS) * GRID_W]
    first_head = lax.broadcasted_iota(jnp.int32, (GRID_W, 2 * NA_HEAD_DIM), 1) < NA_HEAD_DIM
    q = q_ref[0]
    zero = jnp.zeros((GRID_W, 2 * NA_HEAD_DIM), q.dtype)
    pieces = []
    for j in range(NA_ROWS_PER_STEP):
        qj = q[j * GRID_W:(j + 1) * GRID_W]
        pieces += [jnp.where(first_head, qj, zero), jnp.where(first_head, zero, qj)]
    qs = jnp.concatenate(pieces, axis=0)
    sc_ref[...] = lax.dot_general(qs, ck_ref[0], _NT, preferred_element_type=F32)
    half = NA_WIN_ROWS // 2
    rows = 2 * GRID_W

    def window_row(j):
        return jnp.where(i == 0, max(j + half, NA_ROWS_PER_STEP),
                         jnp.where(i == last, min(j + half, NA_ROWS_PER_STEP), j + half))

    for j in range(NA_ROWS_PER_STEP):
        kw = key_window(j + half)
        if j < half:
            kw = jnp.where(i == 0, key_window(NA_ROWS_PER_STEP), kw)
        elif j > half:
            kw = jnp.where(i == last, key_window(NA_ROWS_PER_STEP), kw)
        sl_ref[j] = jnp.dot(qs[j * rows:(j + 1) * rows], kw, preferred_element_type=F32)
    for j in range(NA_ROWS_PER_STEP):
        d = window_row(j) - j - 1
        for c0 in range(0, rows, NA_SOFTMAX_ROWS):
            rs = slice(c0, c0 + NA_SOFTMAX_ROWS)
            ra = slice(j * rows + c0, j * rows + c0 + NA_SOFTMAX_ROWS)
            s_loc = sl_ref[j, rs, :] + bias_ref[0, d, rs, :]
            s_ctx = sc_ref[ra, :]
            m = jnp.maximum(jnp.max(s_loc, axis=1, keepdims=True), jnp.max(s_ctx, axis=1, keepdims=True))
            p_loc = jnp.exp2(s_loc - m)
            p_ctx = jnp.exp2(s_ctx - m)
            l = jnp.sum(p_loc, axis=1, keepdims=True) + jnp.sum(p_ctx, axis=1, keepdims=True)
            pl_ref[j, rs, :] = p_loc.astype(BF16)
            pc_ref[ra, :] = p_ctx.astype(pc_ref.dtype)
            li_ref[ra, :] = jnp.broadcast_to(1.0 / l, (NA_SOFTMAX_ROWS, 2 * NA_HEAD_DIM))
    for j in range(NA_ROWS_PER_STEP):
        start = pl.multiple_of(window_row(j) * GRID_W, GRID_W)
        vw = wv_ref[pl.ds(start, NA_WIN_ROWS * GRID_W), :]
        ol_ref[j * rows:(j + 1) * rows, :] = jnp.dot(pl_ref[j], vw, preferred_element_type=F32)
    o = (ol_ref[...] + jnp.dot(pc_ref[...], cv_ref[0], preferred_element_type=F32)) * li_ref[...]
    for j in range(NA_ROWS_PER_STEP):
        r0 = j * rows
        oj = jnp.where(first_head, o[r0:r0 + GRID_W], o[r0 + GRID_W:r0 + rows])
        o_ref[0, j * GRID_W:(j + 1) * GRID_W, :] = oj.astype(o_ref.dtype)


def _na(q, k, v, ck, cv, bias):
    B, L, W = q.shape
    Lc = ck.shape[1]
    blk = NA_ROWS_PER_STEP * GRID_W
    nblk = L // blk
    assert NA_ROWS_PER_STEP == NA_WIN_ROWS and nblk >= 2 and GRID_W * 2 == LANES
    pw = 2 * NA_HEAD_DIM
    cur = pl.BlockSpec((1, blk, pw), lambda b, h, i: (b, i, h))
    prev = pl.BlockSpec((1, blk, pw), lambda b, h, i: (b, jnp.maximum(i - 1, 0), h))
    nxt = pl.BlockSpec((1, blk, pw), lambda b, h, i: (b, jnp.minimum(i + 1, nblk - 1), h))
    kcur = pl.BlockSpec((1, pw, blk), lambda b, h, i: (b, h, i))
    kprev = pl.BlockSpec((1, pw, blk), lambda b, h, i: (b, h, jnp.maximum(i - 1, 0)))
    knxt = pl.BlockSpec((1, pw, blk), lambda b, h, i: (b, h, jnp.minimum(i + 1, nblk - 1)))
    cspec = pl.BlockSpec((1, Lc, pw), lambda b, h, i: (b, 0, h))
    bspec = pl.BlockSpec((1,) + bias.shape[1:], lambda b, h, i: (h, 0, 0, 0))
    stacked = 2 * blk
    return pl.pallas_call(
        _na_kernel,
        out_shape=jax.ShapeDtypeStruct((B, L, W), BF16),
        grid=(B, W // pw, nblk),
        in_specs=[cur, kprev, kcur, knxt, prev, cur, nxt, cspec, cspec, bspec],
        out_specs=cur,
        scratch_shapes=[pltpu.VMEM((pw, 3 * blk), BF16), pltpu.VMEM((pw, 3 * blk), BF16),
                        pltpu.VMEM((3 * blk, pw), BF16),
                        pltpu.VMEM((stacked, Lc), F32), pltpu.VMEM((stacked, Lc), BF16),
                        pltpu.VMEM((stacked, pw), F32), pltpu.VMEM((stacked, pw), F32),
                        pltpu.VMEM((NA_ROWS_PER_STEP, 2 * GRID_W, NA_WIN_ROWS * GRID_W), F32),
                        pltpu.VMEM((NA_ROWS_PER_STEP, 2 * GRID_W, NA_WIN_ROWS * GRID_W), BF16)],
        compiler_params=pltpu.CompilerParams(vmem_limit_bytes=VMEM_LIMIT),
        name="nattn",
    )(q, k, k, k, v, v, v, ck, cv, bias)


def _ctx_attn_kernel(q_ref, k_ref, v_ref, o_ref):
    q = q_ref[0]
    k = k_ref[0]
    v = v_ref[0]
    first_head = lax.broadcasted_iota(jnp.int32, q.shape, 1) < NA_HEAD_DIM
    outs = []
    for h in range(2):
        qm = jnp.where(first_head if h == 0 else jnp.logical_not(first_head), q, jnp.zeros_like(q))
        s = lax.dot_general(qm, k, _NT, preferred_element_type=F32)
        p = jnp.exp2(s - jnp.max(s, axis=1, keepdims=True))
        o = jnp.dot(p.astype(BF16), v, preferred_element_type=F32)
        outs.append(o / jnp.sum(p, axis=1, keepdims=True))
    o_ref[0] = jnp.where(first_head, outs[0], outs[1]).astype(o_ref.dtype)


def _ctx_attn(q, k, v):
    B, Lc, W = q.shape
    pw = 2 * NA_HEAD_DIM
    spec = pl.BlockSpec((1, Lc, pw), lambda b, h: (b, 0, h))
    return pl.pallas_call(
        _ctx_attn_kernel,
        out_shape=jax.ShapeDtypeStruct((B, Lc, W), BF16),
        grid=(B, W // pw),
        in_specs=[spec, spec, spec],
        out_specs=spec,
        name="ctx_attn",
    )(q, k, v)


LRU_CHUNK = 2048


def _lru_gate_weights(wa, ba, wi, bi, lam):
    def bd(w):
        nb, c, _ = w.shape
        blk_id = np.arange(nb * c) // c
        tiled = jnp.tile(w.reshape(nb * c, c), (1, nb))
        return jnp.where(jnp.asarray(blk_id[:, None] == blk_id[None, :]), tiled, 0.0)
    wg = jnp.stack([jnp.concatenate([bd(wa[d]), bd(wi[d])], axis=1) for d in range(2)]).astype(BF16)
    bg = jnp.stack([jnp.concatenate([ba[d], bi[d]])[None, :] for d in range(2)])
    return wg, bg, lam[:, None, :]


def _lru_coeffs(u, wg, bg, lam):
    C = u.shape[1]
    g = jnp.dot(u.astype(BF16), wg, preferred_element_type=F32) + bg
    sig = 0.5 + 0.5 * jnp.tanh(0.5 * g)
    r, ig = sig[:, :C], sig[:, C:]
    nl = -lam
    softplus = jnp.maximum(nl, 0.0) + jnp.log(1.0 + jnp.exp(-jnp.abs(nl)))
    log_a = (-LRU_C * softplus) * r
    a = jnp.exp(log_a)
    t = jnp.tanh(log_a)
    b = jnp.sqrt(-2.0 * t / (1.0 - t)) * (ig * u)
    return a, b


def _lru_scan(a, b, h0, reverse, ac_ref, bc_ref, h_ref):
    T, C = a.shape
    row = lax.broadcasted_iota(jnp.int32, a.shape, 0) % SUBLANES
    for s in (1, 2, 4):
        shift = T - s if reverse else s
        keep = (row < SUBLANES - s) if reverse else (row >= s)
        b = jnp.where(keep, a * pltpu.roll(b, shift, 0) + b, b)
        a = jnp.where(keep, a * pltpu.roll(a, shift, 0), a)
    ac_ref[...] = a
    bc_ref[...] = b
    ng = T // SUBLANES

    def group(g, h):
        r0 = pl.multiple_of((ng - 1 - g if reverse else g) * SUBLANES, SUBLANES)
        hr = ac_ref[pl.ds(r0, SUBLANES), :] * h + bc_ref[pl.ds(r0, SUBLANES), :]
        h_ref[pl.ds(r0, SUBLANES), :] = hr
        edge = hr[0:1] if reverse else hr[SUBLANES - 1:SUBLANES]
        return jnp.broadcast_to(edge, (SUBLANES, C))

    return lax.fori_loop(0, ng, group, h0, unroll=4)


def _gelu_tanh(x):
    return 0.5 * x * (1.0 + jnp.tanh(math.sqrt(2.0 / math.pi) * (x + 0.044715 * (x * x * x))))


def _lru_ctx_kernel(u_ref, xg_ref, wg_ref, bg_ref, lam_ref, hend_ref, yc_ref, ac_ref, bc_ref, h_ref):
    u = u_ref[0]
    C = u.shape[1]
    total = jnp.zeros_like(u)
    for d, rev in enumerate((False, True)):
        a, b = _lru_coeffs(u, wg_ref[d], bg_ref[d], lam_ref[d])
        hl = _lru_scan(a, b, jnp.zeros((SUBLANES, C), F32), rev, ac_ref, bc_ref, h_ref)
        hend_ref[0, d:d + 1, :] = hl[0:1]
        total = total + h_ref[...]
    yc_ref[0] = (total * _gelu_tanh(xg_ref[0])).astype(yc_ref.dtype)


def _lru_ctx(u, xg, wg, bg, lam):
    B, Lc, C = u.shape
    tok = pl.BlockSpec((1, Lc, C), lambda b: (b, 0, 0))
    full = lambda a: pl.BlockSpec(a.shape, lambda b: (0,) * a.ndim)
    return pl.pallas_call(
        _lru_ctx_kernel,
        out_shape=[jax.ShapeDtypeStruct((B, 2, C), F32), jax.ShapeDtypeStruct((B, Lc, C), BF16)],
        grid=(B,),
        in_specs=[tok, tok, full(wg), full(bg), full(lam)],
        out_specs=[pl.BlockSpec((1, 2, C), lambda b: (b, 0, 0)), tok],
        scratch_shapes=[pltpu.VMEM((Lc, C), F32)] * 3,
        name="lru_ctx",
    )(u, xg, wg, bg, lam)


def _lru_dir_kernel(*refs, d, reverse):
    if reverse:
        u_ref, hend_ref, wg_ref, bg_ref, lam_ref, hf_ref, xg_ref, o_ref, ac_ref, bc_ref, h_ref, carry_ref = refs
    else:
        u_ref, hend_ref, wg_ref, bg_ref, lam_ref, o_ref, ac_ref, bc_ref, carry_ref = refs
        h_ref = o_ref.at[0]
    C = u_ref.shape[2]

    @pl.when(pl.program_id(1) == 0)
    def _():
        carry_ref[...] = jnp.broadcast_to(hend_ref[0, d:d + 1, :], (SUBLANES, C))

    a, b = _lru_coeffs(u_ref[0], wg_ref[d], bg_ref[d], lam_ref[d])
    carry_ref[...] = _lru_scan(a, b, carry_ref[...], reverse, ac_ref, bc_ref, h_ref)
    if reverse:
        o_ref[0] = ((hf_ref[0] + h_ref[...]) * _gelu_tanh(xg_ref[0])).astype(o_ref.dtype)


def _lru_dir(u, hend, wg, bg, lam, hf=None, xg=None):
    B, L, C = u.shape
    reverse = hf is not None
    T = LRU_CHUNK
    nb = L // T
    tok = pl.BlockSpec((1, T, C), (lambda b, i: (b, nb - 1 - i, 0)) if reverse else (lambda b, i: (b, i, 0)))
    full = lambda a: pl.BlockSpec(a.shape, lambda b, i: (0,) * a.ndim)
    ins = [u, hend, wg, bg, lam] + ([hf, xg] if reverse else [])
    specs = [tok, pl.BlockSpec((1, 2, C), lambda b, i: (b, 0, 0)), full(wg), full(bg), full(lam)]
    specs += [tok, tok] if reverse else []
    scratch = [pltpu.VMEM((T, C), F32)] * (3 if reverse else 2) + [pltpu.VMEM((SUBLANES, C), F32)]
    return pl.pallas_call(
        functools.partial(_lru_dir_kernel, d=int(reverse), reverse=reverse),
        out_shape=jax.ShapeDtypeStruct((B, L, C), BF16 if reverse else F32),
        grid=(B, nb),
        in_specs=specs,
        out_specs=tok,
        scratch_shapes=scratch,
        compiler_params=pltpu.CompilerParams(dimension_semantics=("arbitrary", "arbitrary")),
        name="lru_bwd" if reverse else "lru_fwd",
    )(*ins)


def _lru(u, xg, u_c, xg_c, wa, ba, wi, bi, lam):
    wg, bg, lam3 = _lru_gate_weights(wa, ba, wi, bi, lam)
    hend, yc = _lru_ctx(u_c, xg_c, wg, bg, lam3)
    hf = _lru_dir(u, hend, wg, bg, lam3)
    return _lru_dir(u, hend, wg, bg, lam3, hf, xg), yc


def kernel(x, c, ctx, c_ctx, ada_w, ada_b, g_mix_pre, g_mix_post, g_ffn_pre, g_ffn_post, w_in, w_out, hy_conv_w,
           hy_conv_b, hy_f_w1, hy_f_b1, hy_f_w2, hy_f_b2, hy_f_w3, hy_f_freq, hy_bias, na_rpb, lru_conv_w,
           lru_conv_b, lru_wa, lru_ba, lru_wi, lru_bi, lru_lam, ffn_w_gu, ffn_w_down):
    B, L, D = x.shape
    Lc = ctx.shape[1]
    tm = 1024

    assert B + 1 <= SUBLANES
    cond_t = jnp.zeros((D, SUBLANES), F32).at[:, 0:B].set(c.T).at[:, B].set(c_ctx)
    mods = _modulation(cond_t, B + 1, ada_w, ada_b)

    xc = ctx
    for l in range(DEPTH):
        with_ctx_out = l < DEPTH - 1
        m = mods[l].reshape(8, 6, D)
        lat = [m[0:B, j][:, None, :] for j in range(6)]
        cx = [jnp.broadcast_to(m[B, j][None, None, :], (B, 1, D)) for j in range(6)]
        row = lambda a: a.reshape(1, -1)

        w_in_bf = w_in[l].astype(BF16)
        w_out_bf = w_out[l].astype(BF16)
        wgu = ffn_w_gu[l].astype(BF16)
        wd = ffn_w_down[l].astype(BF16)

        conv_args = (hy_conv_w[l], row(hy_conv_b[l]), lru_conv_w[l], row(lru_conv_b[l]))
        hv, hx1, hx2, q, k, v, lu, lg = _inproj(x, lat[1], lat[0], row(g_mix_pre[l]), w_in_bf, *conv_args, tm=tm,
                                                chan_major=True)
        cv, cx1, cx2, cq, ck, cvv, clu, clg = _inproj(xc, cx[1], cx[0], row(g_mix_pre[l]), w_in_bf, *conv_args,
                                                      tm=Lc, chan_major=False)

        filt = (hy_f_w1[l], hy_f_b1[l], hy_f_w2[l], hy_f_b2[l], hy_f_w3[l], hy_f_freq[l])
        y_hy = _hyena(hv, hx1, hx2, filt, hy_bias[l])
        y_na = _na(q, k, v, ck, cvv, _na_bias_table(na_rpb[l]))
        y_lru, yc_lru = _lru(lu, lg, clu, clg, lru_wa[l], lru_ba[l], lru_wi[l], lru_bi[l], lru_lam[l])

        ffn_args = (row(g_ffn_pre[l]), row(g_ffn_post[l]), wgu, wd)
        x = _out_ffn(y_hy, y_na, y_lru, w_out_bf, x, row(g_mix_post[l]), lat[2], lat[4], lat[3], lat[5], *ffn_args,
                     tm=tm, chan_major=True)

        if with_ctx_out:
            yc_hy = _hyena_ctx(cv, cx1, cx2, filt, hy_bias[l])
            yc_na = _ctx_attn(cq, ck, cvv)
            xc = _out_ffn(yc_hy, yc_na, yc_lru, w_out_bf, xc, row(g_mix_post[l]), cx[2], cx[4], cx[3], cx[5],
                          *ffn_args, tm=Lc, chan_major=False)
    return x
```

```python
import functools
import math

import jax
import jax.numpy as jnp
import numpy as np
from jax import lax
from jax.experimental import pallas as pl
from jax.experimental.pallas import tpu as pltpu

F32 = jnp.float32
BF16 = jnp.bfloat16

D_MODEL = 1024
DEPTH = 2
GRID_W = 64
HY_WIDTH = D_MODEL // 4
NA_HEAD_DIM = 64
NA_WIDTH = D_MODEL // 2
NA_HEADS = NA_WIDTH // NA_HEAD_DIM
LRU_WIDTH = D_MODEL // 4
HY_ORDER = 2
HY_BANDS = 16
HY_FAST_DECAY = 0.3
HY_SLOW_DECAY = 1.5
HY_DECAY_TARGET = 1e-2
NA_WIN_ROWS = 8
NA_WIN_COLS = 16
LRU_C = 8.0
D_FF = -(-8 * D_MODEL // (3 * 256)) * 256
RMS_EPS = 1e-6
NEG_INF = -1e30
LOG2E = math.log2(math.e)

_HY_END = 3 * HY_WIDTH
_NA_END = _HY_END + 3 * NA_WIDTH
_LRU_MID = _NA_END + LRU_WIDTH

HALO = 8
SUBLANES = 8
VMEM_LIMIT = 56 * 1024 * 1024


def _rms(x, g):
    return x * lax.rsqrt(jnp.mean(x * x, axis=-1, keepdims=True) + RMS_EPS) * g


def _mod_kernel(ct_ref, w_ref, b_ref, o_ref, *, n_cond):
    ct = ct_ref[...]
    st = ct * jax.nn.sigmoid(ct)
    w = w_ref[0]
    rows = [jnp.sum(w * st[:, r:r + 1], axis=0, keepdims=True) for r in range(n_cond)]
    rows.append(jnp.zeros((SUBLANES - n_cond, w.shape[1]), F32))
    o_ref[0] = jnp.concatenate(rows, axis=0) + b_ref[0]


def _modulation(cond_t, n_cond, ada_w, ada_b):
    tn = 768
    n = ada_w.shape[-1]
    return pl.pallas_call(
        functools.partial(_mod_kernel, n_cond=n_cond),
        out_shape=jax.ShapeDtypeStruct((DEPTH, SUBLANES, n), F32),
        grid=(DEPTH, n // tn),
        in_specs=[pl.BlockSpec((D_MODEL, SUBLANES), lambda l, j: (0, 0)),
                  pl.BlockSpec((1, D_MODEL, tn), lambda l, j: (l, 0, j)),
                  pl.BlockSpec((1, 1, tn), lambda l, j: (l, 0, j))],
        out_specs=pl.BlockSpec((1, SUBLANES, tn), lambda l, j: (l, 0, j)),
        compiler_params=pltpu.CompilerParams(vmem_limit_bytes=VMEM_LIMIT),
        name="adaln_modulation",
    )(cond_t, ada_w, ada_b.reshape(DEPTH, 1, n))


CM_LANES = 256
LANES = 128
CM_HALVES = CM_LANES // LANES


def _chan_major_shape(B, L, C):
    return (B, C // SUBLANES, CM_HALVES, L // CM_LANES * SUBLANES, LANES)


def _chan_major_spec(tm, C):
    return pl.BlockSpec((1, C // SUBLANES, CM_HALVES, tm // CM_LANES * SUBLANES, LANES),
                        lambda b, i: (b, 0, 0, i, 0))


def _store_chan_major(ref, u):
    ut = u.T
    for g in range(u.shape[1] // SUBLANES):
        for j in range(u.shape[0] // CM_LANES):
            for h in range(CM_HALVES):
                lane0 = j * CM_LANES + h * LANES
                ref[0, g, h, j * SUBLANES:(j + 1) * SUBLANES, :] = ut[g * SUBLANES:(g + 1) * SUBLANES,
                                                                      lane0:lane0 + LANES]


def _load_chan_major(ref):
    _, ng, _, nr, _ = ref.shape
    rows = [jnp.concatenate([ref[0, g, h, j * SUBLANES:(j + 1) * SUBLANES, :]
                             for j in range(nr // SUBLANES) for h in range(CM_HALVES)], axis=1) for g in range(ng)]
    return jnp.concatenate(rows, axis=0).T


def _chan_rows(ref, lead, c, n):
    return jnp.concatenate([ref.at[lead + (h,)][pl.ds(c, n, stride=SUBLANES), :] for h in range(CM_HALVES)], axis=1)


def _inproj_kernel(xp_ref, xc_ref, xn_ref, sc_ref, sh_ref, g_ref, w_ref, hcw_ref, hcb_ref, lcw_ref, lcb_ref,
                   hv_ref, hx1_ref, hx2_ref, q_ref, k_ref, v_ref, lu_ref, lg_ref, pe_ref, *, tm, chan_major):
    i = pl.program_id(1)
    last = pl.num_programs(1) - 1
    g = g_ref[...]
    sc1 = 1.0 + sc_ref[0]
    sh = sh_ref[0]

    def norm_mod(xv):
        return _rms(xv, g) * sc1 + sh

    hp = norm_mod(xp_ref[0]) * (i > 0).astype(F32)
    hn = norm_mod(xn_ref[0]) * (i < last).astype(F32)
    he = jnp.concatenate([hp, norm_mod(xc_ref[0]), hn], axis=0).astype(BF16)

    pe_ref[:, 0:_HY_END] = jnp.dot(he, w_ref[:, 0:_HY_END], preferred_element_type=F32)
    pe_ref[:, _HY_END:] = jnp.dot(he, w_ref[:, _NA_END:_LRU_MID], preferred_element_type=F32)
    hc = he[HALO:HALO + tm]
    qkv = jnp.dot(hc, w_ref[:, _HY_END:_NA_END], preferred_element_type=F32)
    q_ref[0] = (qkv[:, 0:NA_WIDTH] * (NA_HEAD_DIM ** -0.5 * LOG2E)).astype(BF16)
    kk = qkv[:, NA_WIDTH:2 * NA_WIDTH]
    k_ref[0] = (kk.T if chan_major else kk).astype(BF16)
    v_ref[0] = qkv[:, 2 * NA_WIDTH:].astype(BF16)
    lg_ref[0] = jnp.dot(hc, w_ref[:, _LRU_MID:], preferred_element_type=F32)

    u = hcb_ref[...]
    for kk in range(3):
        u = u + hcw_ref[kk:kk + 1, :] * pe_ref[pl.ds(HALO - 1 + kk, tm), 0:_HY_END]
    for n, ref in enumerate((hv_ref, hx1_ref, hx2_ref)):
        un = u[:, n * HY_WIDTH:(n + 1) * HY_WIDTH]
        if chan_major:
            _store_chan_major(ref, un)
        else:
            ref[0] = un
    ul = lcb_ref[...]
    for kk in range(4):
        ul = ul + lcw_ref[kk:kk + 1, :] * pe_ref[pl.ds(HALO - 2 + kk, tm), _HY_END:]
    lu_ref[0] = ul


def _inproj(x, sc, sh, g, w_bf, hcw, hcb, lcw, lcb, *, tm, chan_major):
    B, L, D = x.shape
    nb = tm // HALO
    nh = L // HALO
    tok = lambda w, dt: jax.ShapeDtypeStruct((B, L, w), dt)
    tspec = lambda w: pl.BlockSpec((1, tm, w), lambda b, i: (b, i, 0))
    full = lambda a: pl.BlockSpec(a.shape, lambda b, i: (0,) * a.ndim)
    vec = pl.BlockSpec((1, 1, D), lambda b, i: (b, 0, 0))
    if chan_major:
        hy_shape = jax.ShapeDtypeStruct(_chan_major_shape(B, L, HY_WIDTH), F32)
        hy_spec = _chan_major_spec(tm, HY_WIDTH)
        k_shape = jax.ShapeDtypeStruct((B, NA_WIDTH, L), BF16)
        k_spec = pl.BlockSpec((1, NA_WIDTH, tm), lambda b, i: (b, 0, i))
    else:
        hy_shape, hy_spec = tok(HY_WIDTH, F32), tspec(HY_WIDTH)
        k_shape, k_spec = tok(NA_WIDTH, BF16), tspec(NA_WIDTH)
    return pl.pallas_call(
        functools.partial(_inproj_kernel, tm=tm, chan_major=chan_major),
        out_shape=[hy_shape] * 3 + [tok(NA_WIDTH, BF16), k_shape, tok(NA_WIDTH, BF16)] + [tok(LRU_WIDTH, F32)] * 2,
        grid=(B, L // tm),
        in_specs=[pl.BlockSpec((1, HALO, D), lambda b, i: (b, jnp.maximum(i * nb - 1, 0), 0)),
                  pl.BlockSpec((1, tm, D), lambda b, i: (b, i, 0)),
                  pl.BlockSpec((1, HALO, D), lambda b, i: (b, jnp.minimum((i + 1) * nb, nh - 1), 0)),
                  vec, vec, full(g), pl.BlockSpec(w_bf.shape, lambda b, i: (0, 0), pipeline_mode=pl.Buffered(1)),
                  full(hcw), full(hcb), full(lcw), full(lcb)],
        out_specs=[hy_spec] * 3 + [tspec(NA_WIDTH), k_spec, tspec(NA_WIDTH)] + [tspec(LRU_WIDTH)] * 2,
        scratch_shapes=[pltpu.VMEM((tm + 2 * HALO, _HY_END + LRU_WIDTH), F32)],
        compiler_params=pltpu.CompilerParams(vmem_limit_bytes=VMEM_LIMIT),
        name="inproj",
    )(x, x, x, sc, sh, g, w_bf, hcw, hcb, lcw, lcb)


FF_CHUNK = 256


def _out_ffn_kernel(yh_ref, yn_ref, yl_ref, wo_ref, x_ref, gm_ref, gtm_ref, sc_ref, sh_ref, gt_ref, gpre_ref,
                    gpost_ref, wgu_ref, wd_ref, o_ref, x1_ref, h_ref, *, chan_major, halves):
    yh = (_load_chan_major(yh_ref) if chan_major else yh_ref[0]).astype(BF16)
    tm = x_ref.shape[1]
    for r0 in range(0, tm, tm // halves):
        rs = slice(r0, r0 + tm // halves)
        y = jnp.dot(yh[rs], wo_ref[0:HY_WIDTH], preferred_element_type=F32)
        y = y + jnp.dot(yn_ref[0, rs, :], wo_ref[HY_WIDTH:HY_WIDTH + NA_WIDTH], preferred_element_type=F32)
        y = y + jnp.dot(yl_ref[0, rs, :], wo_ref[HY_WIDTH + NA_WIDTH:], preferred_element_type=F32)
        x = x_ref[0, rs, :] + gtm_ref[0] * _rms(y, gm_ref[...])
        x1_ref[rs, :] = x
        h_ref[rs, :] = (_rms(x, gpre_ref[...]) * (1.0 + sc_ref[0]) + sh_ref[0]).astype(BF16)
    for r0 in range(0, tm, tm // halves):
        rs = slice(r0, r0 + tm // halves)
        h = h_ref[rs, :]
        acc = jnp.zeros((tm // halves, x_ref.shape[2]), F32)
        for c0 in range(0, D_FF, FF_CHUNK):
            g = jnp.dot(h, wgu_ref[:, c0:c0 + FF_CHUNK], preferred_element_type=F32)
            u = jnp.dot(h, wgu_ref[:, D_FF + c0:D_FF + c0 + FF_CHUNK], preferred_element_type=F32)
            a = (g * jax.nn.sigmoid(g) * u).astype(BF16)
            acc = acc + jnp.dot(a, wd_ref[c0:c0 + FF_CHUNK, :], preferred_element_type=F32)
        o_ref[0, rs, :] = x1_ref[rs, :] + gt_ref[0] * _rms(acc, gpost_ref[...])


def _out_ffn(yh, yn, yl, wo, x, gm, gtm, sc, sh, gt, gpre, gpost, wgu, wd, *, tm, chan_major):
    B, L, D = x.shape
    tspec = lambda w: pl.BlockSpec((1, tm, w), lambda b, i: (b, i, 0))
    vec = pl.BlockSpec((1, 1, D), lambda b, i: (b, 0, 0))
    full = lambda a: pl.BlockSpec(a.shape, lambda b, i: (0,) * a.ndim)
    res = lambda a: pl.BlockSpec(a.shape, lambda b, i: (0,) * a.ndim, pipeline_mode=pl.Buffered(1))
    hy_spec = _chan_major_spec(tm, HY_WIDTH) if chan_major else tspec(HY_WIDTH)
    return pl.pallas_call(
        functools.partial(_out_ffn_kernel, chan_major=chan_major, halves=2 if tm >= 1024 else 1),
        out_shape=jax.ShapeDtypeStruct((B, L, D), F32),
        grid=(B, L // tm),
        in_specs=[hy_spec, tspec(NA_WIDTH), tspec(LRU_WIDTH), res(wo), tspec(D), full(gm), vec,
                  vec, vec, vec, full(gpre), full(gpost), res(wgu), res(wd)],
        out_specs=tspec(D),
        scratch_shapes=[pltpu.VMEM((tm, D), F32), pltpu.VMEM((tm, D), BF16)],
        compiler_params=pltpu.CompilerParams(vmem_limit_bytes=VMEM_LIMIT),
        name="out_ffn",
    )(yh, yn, yl, wo, x, gm, gtm, sc, sh, gt, gpre, gpost, wgu, wd)


HY_N1 = CM_LANES
HY_N2 = 128
HY_FILT_COLS = 2048


def _block_cplx(re, im):
    return np.block([[re, -im], [im, re]])


@functools.lru_cache(maxsize=None)
def _dft_consts():
    n = HY_N1 * HY_N2
    h = HY_N2 // 2
    i1 = np.arange(HY_N1, dtype=np.float64)
    i2 = np.arange(HY_N2, dtype=np.float64)
    a1 = 2.0 * np.pi * np.outer(i1, i1) / HY_N1
    a2 = 2.0 * np.pi * np.outer(i2, i2) / HY_N2
    at = 2.0 * np.pi * np.outer(i2, i1) / n
    r2, m2 = np.cos(a2), -np.sin(a2)
    r1, m1 = np.cos(a1), -np.sin(a1)
    f32 = lambda m: np.asarray(m, np.float32)
    return dict(
        rows_data=f32(_block_cplx(r2[:, :h], m2[:, :h])),
        rows_filt=f32(np.concatenate([r2, m2], axis=0)),
        rows_out=f32(_block_cplx(r2[:h, :], m2[:h, :])),
        lanes=f32(np.block([[r1, m1], [-m1, r1]])),
        twr=f32(np.cos(at)), twi=f32(-np.sin(at)))


def _dft_const(name):
    return jnp.asarray(_dft_consts()[name], F32)


def _cmul(ar, ai, br, bi):
    return ar * br - ai * bi, ar * bi + ai * br


def _hy_conv_kernel(v_ref, x1_ref, x2_ref, k0_ref, k1_ref, ks0_ref, ks1_ref, bias_ref, rd_ref, rf_ref, ro_ref,
                    ln_ref, twr_ref, twi_ref, o_ref, st_ref, ks_ref):
    g = pl.program_id(0)
    n_ch = SUBLANES
    h = HY_N2 // 2
    rows_data = rd_ref[...].astype(BF16)
    rows_filt = rf_ref[...].astype(BF16)
    rows_out = ro_ref[...].astype(BF16)
    lanes = ln_ref[...].astype(BF16)
    twr, twi = twr_ref[...], twi_ref[...]

    def rows_then_twiddle(mat, x, c):
        a = jnp.dot(mat, x.astype(BF16), preferred_element_type=F32)
        ar, ai = _cmul(a[:HY_N2], a[HY_N2:], twr, twi)
        st_ref[c * HY_N2:(c + 1) * HY_N2, 0:HY_N1] = ar.astype(BF16)
        st_ref[c * HY_N2:(c + 1) * HY_N2, HY_N1:] = ai.astype(BF16)

    for o, (k_ref, s_ref) in enumerate(((k0_ref, ks0_ref), (k1_ref, ks1_ref))):
        for c in range(n_ch):
            rows_then_twiddle(rows_filt, _chan_rows(k_ref, (0,), c, HY_N2), c)
        spec = jnp.dot(st_ref[...], lanes, preferred_element_type=F32)
        inv = 1.0 / (jnp.sum(s_ref[...], axis=1, keepdims=True) * float(HY_N1 * HY_N2))
        for c in range(n_ch):
            ks_ref[o, c * HY_N2:(c + 1) * HY_N2, :] = spec[c * HY_N2:(c + 1) * HY_N2] * inv[c:c + 1, :]

    z = [[_chan_rows(v_ref, (b, 0), c, h) for c in range(n_ch)] for b in range(2)]
    for o, gate_ref in enumerate((x1_ref, x2_ref)):
        for c in range(n_ch):
            rows_then_twiddle(rows_data, jnp.concatenate([z[0][c], z[1][c]], axis=0), c)
        x = jnp.dot(st_ref[...], lanes, preferred_element_type=F32)
        pr, pi = _cmul(x[:, :HY_N1], x[:, HY_N1:], ks_ref[o, :, 0:HY_N1], ks_ref[o, :, HY_N1:])
        st_ref[:, 0:HY_N1] = pr.astype(BF16)
        st_ref[:, HY_N1:] = (-pi).astype(BF16)
        y = jnp.dot(st_ref[...], lanes, preferred_element_type=F32)
        for c in range(n_ch):
            yc = y[c * HY_N2:(c + 1) * HY_N2]
            yr, yi = _cmul(yc[:, :HY_N1], yc[:, HY_N1:], twr, twi)
            out = jnp.dot(rows_out, jnp.concatenate([yr, yi], axis=0).astype(BF16),
                          preferred_element_type=F32)
            bias = bias_ref[o, g * n_ch + c]
            for b, conv in ((0, out[:h]), (1, -out[h:])):
                z[b][c] = _chan_rows(gate_ref, (b, 0), c, h) * (conv + bias * z[b][c])
    for b in range(2):
        for c in range(n_ch):
            for hh in range(CM_HALVES):
                o_ref.at[b, 0, hh][pl.ds(c, h, stride=SUBLANES), :] = z[b][c][:, hh * LANES:(hh + 1) * LANES]


def _hy_conv(v, x1, x2, kc, ksum, hy_bias):
    B, ng = v.shape[0], v.shape[1]
    assert B == 2 and v.shape[3] * 2 == HY_N2 * SUBLANES and kc.shape[2] == HY_N2 * SUBLANES
    consts = [_dft_const(n) for n in ("rows_data", "rows_filt", "rows_out", "lanes", "twr", "twi")]
    tok = pl.BlockSpec((B, 1) + v.shape[2:], lambda g: (0, g, 0, 0, 0))
    kspec = lambda o: pl.BlockSpec((1,) + kc.shape[1:], lambda g: (o * ng + g, 0, 0, 0))
    sspec = lambda o: pl.BlockSpec((SUBLANES, ksum.shape[1]), lambda g: (o * ng + g, 0))
    full = lambda a: pl.BlockSpec(a.shape, lambda g: (0,) * a.ndim)
    return pl.pallas_call(
        _hy_conv_kernel,
        out_shape=jax.ShapeDtypeStruct(v.shape, F32),
        grid=(ng,),
        in_specs=[tok, tok, tok, kspec(0), kspec(1), sspec(0), sspec(1),
                  pl.BlockSpec(memory_space=pltpu.SMEM)] + [full(a) for a in consts],
        out_specs=tok,
        scratch_shapes=[pltpu.VMEM((SUBLANES * HY_N2, 2 * HY_N1), BF16),
                        pltpu.VMEM((HY_ORDER, SUBLANES * HY_N2, 2 * HY_N1), F32)],
        compiler_params=pltpu.CompilerParams(vmem_limit_bytes=VMEM_LIMIT),
        name="hy_conv",
    )(v, x1, x2, kc, kc, ksum, ksum, hy_bias, *consts)


def _hy_filter_kernel(w1t_ref, w1cs_ref, trig_ref, b1_ref, f0_ref, w2_ref, b2_ref, f1_ref, w3_ref, dl_ref,
                      k_ref, s_ref, *, L, P, chan_major):
    i = pl.program_id(0)

    def pos_t(rows):
        n = i * P + lax.broadcasted_iota(jnp.int32, (rows, P), 1)
        return n, jnp.where(n < L, n, 2 * L - n).astype(F32) / L

    _, t1 = pos_t(1)
    h = w1t_ref[...] * t1 + b1_ref[...]
    h = h + jnp.dot(w1cs_ref[...], trig_ref[...].astype(BF16), preferred_element_type=F32)
    h = jnp.sin(f0_ref[...] * h)
    h = jnp.sin(f1_ref[...] * (jnp.dot(w2_ref[...], h.astype(BF16), preferred_element_type=F32) + b2_ref[...]))
    k = jnp.dot(w3_ref[0], h.astype(BF16), preferred_element_type=F32)
    n, t = pos_t(1)
    k = jnp.where(n == L, 0.0, k * jnp.exp(-t * dl_ref[...]))
    if chan_major:
        for j in range(P // CM_LANES):
            for hh in range(CM_HALVES):
                lane0 = j * CM_LANES + hh * LANES
                k_ref[:, hh, j * SUBLANES:(j + 1) * SUBLANES, :] = k[:, lane0:lane0 + LANES].reshape(
                    k_ref.shape[0], SUBLANES, LANES)
    else:
        k_ref[...] = k
    ka = jnp.abs(k)
    tot = ka[:, 0:LANES]
    for j in range(1, P // LANES):
        tot = tot + ka[:, j * LANES:(j + 1) * LANES]

    @pl.when(i == 0)
    def _():
        s_ref[...] = jnp.zeros_like(s_ref)

    s_ref[...] += tot


@functools.lru_cache(maxsize=None)
def _hy_trig_features(L):
    n = np.arange(2 * L)
    t = np.where(n < L, n, 2 * L - n).astype(np.float64) / L
    ang = 2.0 * np.pi * t[None, :] * np.arange(1, HY_BANDS + 1, dtype=np.float64)[:, None]
    return np.asarray(np.concatenate([np.cos(ang), np.sin(ang)], axis=0), np.float32)


def _hy_filter(L, P, chan_major, f_w1, f_b1, f_w2, f_b2, f_w3, f_freq):
    nb = 2 * L // P
    assert nb % 2 == 0 and P % CM_LANES == 0
    hid = f_w2.shape[0]
    oc = HY_ORDER * HY_WIDTH
    col = lambda v: v.reshape(-1, 1)
    w3 = f_w3.reshape(hid, HY_ORDER, 2, HY_WIDTH)
    w3 = jnp.stack([w3[:, :, d].reshape(hid, oc).T for d in range(2)]).astype(BF16)
    deltas = np.abs(np.linspace(math.log(HY_DECAY_TARGET) / HY_FAST_DECAY,
                                math.log(HY_DECAY_TARGET) / HY_SLOW_DECAY, HY_WIDTH, dtype=np.float32))
    ins = [col(f_w1[0]), f_w1[1:].T.astype(BF16), jnp.asarray(_hy_trig_features(L)), col(f_b1),
           col(f_freq[0]), f_w2.T.astype(BF16), col(f_b2), col(f_freq[1]), w3,
           jnp.asarray(np.tile(deltas, HY_ORDER)[:, None])]
    full = lambda a: pl.BlockSpec(a.shape, lambda i: (0,) * a.ndim)
    specs = [full(a) for a in ins]
    specs[2] = pl.BlockSpec((2 * HY_BANDS, P), lambda i: (0, i))
    specs[8] = pl.BlockSpec((1,) + w3.shape[1:], lambda i: (i // (nb // 2), 0, 0))
    if chan_major:
        kshape = (oc // SUBLANES, CM_HALVES, 2 * L // CM_LANES * SUBLANES, LANES)
        kspec = pl.BlockSpec((oc // SUBLANES, CM_HALVES, P // CM_LANES * SUBLANES, LANES), lambda i: (0, 0, i, 0))
    else:
        kshape = (oc, 2 * L)
        kspec = pl.BlockSpec((oc, P), lambda i: (0, i))
    return pl.pallas_call(
        functools.partial(_hy_filter_kernel, L=L, P=P, chan_major=chan_major),
        out_shape=[jax.ShapeDtypeStruct(kshape, F32), jax.ShapeDtypeStruct((oc, LANES), F32)],
        grid=(nb,),
        in_specs=specs,
        out_specs=[kspec, pl.BlockSpec((oc, LANES), lambda i: (0, 0))],
        compiler_params=pltpu.CompilerParams(dimension_semantics=("arbitrary",), vmem_limit_bytes=VMEM_LIMIT),
        name="hy_filter",
    )(*ins)


def _hyena(v, x1, x2, filt, hy_bias):
    L = v.shape[3] // SUBLANES * CM_LANES
    assert 2 * L == HY_N1 * HY_N2
    kc, ksum = _hy_filter(L, HY_FILT_COLS, True, *filt)
    return _hy_conv(v, x1, x2, kc, ksum, hy_bias)


@functools.lru_cache(maxsize=None)
def _ctx_dft_consts(n):
    i = np.arange(n, dtype=np.float64)
    a = 2.0 * np.pi * np.outer(i, i) / n
    f = np.concatenate([np.cos(a), -np.sin(a)], axis=0)
    return np.asarray(f, np.float32), np.asarray(f.T[:n // 2], np.float32)


def _hyena_ctx_kernel(v_ref, x1_ref, x2_ref, kc_ref, ks_ref, bias_ref, ff_ref, fi_ref, o_ref):
    n = ff_ref.shape[1]
    lc = n // 2
    C = v_ref.shape[2]
    ff = ff_ref[...].astype(BF16)
    inv = 1.0 / (jnp.sum(ks_ref[...], axis=1, keepdims=True) * float(n))
    kn = (kc_ref[...] * inv).T
    kspec = jnp.dot(ff, kn.astype(BF16), preferred_element_type=F32)
    z = jnp.concatenate([v_ref[0], v_ref[1]], axis=1)
    for o, g_ref in enumerate((x1_ref, x2_ref)):
        x = jnp.dot(ff[:, :lc], z.astype(BF16), preferred_element_type=F32)
        k = kspec[:, o * C:(o + 1) * C]
        kr = jnp.concatenate([k[:n], k[:n]], axis=1)
        ki = jnp.concatenate([k[n:], k[n:]], axis=1)
        yr, yi = _cmul(x[:n], x[n:], kr, ki)
        y = jnp.concatenate([yr, yi], axis=0)
        conv = jnp.dot(fi_ref[...].astype(BF16), y.astype(BF16), preferred_element_type=F32)
        bias = jnp.concatenate([bias_ref[o:o + 1, :], bias_ref[o:o + 1, :]], axis=1)
        z = jnp.concatenate([g_ref[0], g_ref[1]], axis=1) * (conv + bias * z)
    o_ref[0] = z[:, :C]
    o_ref[1] = z[:, C:]


def _hyena_ctx(v, x1, x2, filt, hy_bias):
    B, Lc, C = v.shape
    assert B == 2
    kc, ksum = _hy_filter(Lc, Lc, False, *filt)
    ff, fi = (jnp.asarray(m, F32) for m in _ctx_dft_consts(2 * Lc))
    return pl.pallas_call(
        _hyena_ctx_kernel,
        out_shape=jax.ShapeDtypeStruct((B, Lc, C), F32),
        compiler_params=pltpu.CompilerParams(vmem_limit_bytes=VMEM_LIMIT),
        name="hyena_ctx",
    )(v, x1, x2, kc, ksum, hy_bias, ff, fi)


NA_ROWS_PER_STEP = 8
NA_SOFTMAX_ROWS = 32
_NT = (((1,), (1,)), ((), ()))


def _na_bias_table(rpb):
    qc = np.arange(GRID_W)[:, None]
    kc = np.arange(GRID_W)[None, :]
    start = np.clip(qc - NA_WIN_COLS // 2, 0, GRID_W - NA_WIN_COLS)
    valid = (kc >= start) & (kc < start + NA_WIN_COLS)
    pad = jnp.pad(rpb, ((0, 0), (0, 0), (GRID_W, GRID_W)))
    shift = GRID_W + NA_WIN_COLS - 1
    toep = jnp.stack([pad[:, :, shift - c:shift - c + GRID_W] for c in range(GRID_W)], axis=1)
    full = jnp.where(jnp.asarray(valid)[None, :, None, :], toep * LOG2E, NEG_INF)
    full = full.reshape(NA_HEADS // 2, 2, GRID_W, 2 * NA_WIN_ROWS - 1, GRID_W)
    t = jnp.stack([full[:, :, :, d:d + NA_WIN_ROWS] for d in range(NA_WIN_ROWS)], axis=1)
    return t.reshape(NA_HEADS // 2, NA_WIN_ROWS, 2 * GRID_W, NA_WIN_ROWS * GRID_W).astype(F32)


def _na_kernel(q_ref, kp_ref, kc_ref, kn_ref, vp_ref, vc_ref, vn_ref, ck_ref, cv_ref, bias_ref, o_ref,
               wk_ref, ws_ref, wv_ref, sc_ref, pc_ref, ol_ref, li_ref, sl_ref, pl_ref):
    i = pl.program_id(2)
    last = pl.num_programs(2) - 1
    blk = NA_ROWS_PER_STEP * GRID_W
    for n, (kr, vr) in enumerate(((kp_ref, vp_ref), (kc_ref, vc_ref), (kn_ref, vn_ref))):
        wk_ref[:, n * blk:(n + 1) * blk] = kr[0]
        wv_ref[n * blk:(n + 1) * blk] = vr[0]
    ws_ref[:, 0:3 * blk - GRID_W] = wk_ref[:, GRID_W:3 * blk]

    def key_window(off):
        if off % 2 == 0:
            return wk_ref[:, off * GRID_W:(off + NA_WIN_ROWS) * GRID_W]
        return ws_ref[:, (off - 1) * GRID_W:(off - 1 + NA_WIN_ROWS) * GRID_W]
    first_head = lax.broadcasted_iota(jnp.int32, (GRID_W, 2 * NA_HEAD_DIM), 1) < NA_HEAD_DIM
    q = q_ref[0]
    zero = jnp.zeros((GRID_W, 2 * NA_HEAD_DIM), q.dtype)
    pieces = []
    for j in range(NA_ROWS_PER_STEP):
        qj = q[j * GRID_W:(j + 1) * GRID_W]
        pieces += [jnp.where(first_head, qj, zero), jnp.where(first_head, zero, qj)]
    qs = jnp.concatenate(pieces, axis=0)
    sc_ref[...] = lax.dot_general(qs, ck_ref[0], _NT, preferred_element_type=F32)
    half = NA_WIN_ROWS // 2
    rows = 2 * GRID_W

    def window_row(j):
        return jnp.where(i == 0, max(j + half, NA_ROWS_PER_STEP),
                         jnp.where(i == last, min(j + half, NA_ROWS_PER_STEP), j + half))

    for j in range(NA_ROWS_PER_STEP):
        kw = key_window(j + half)
        if j < half:
            kw = jnp.where(i == 0, key_window(NA_ROWS_PER_STEP), kw)
        elif j > half:
            kw = jnp.where(i == last, key_window(NA_ROWS_PER_STEP), kw)
        sl_ref[j] = jnp.dot(qs[j * rows:(j + 1) * rows], kw, preferred_element_type=F32)
    for j in range(NA_ROWS_PER_STEP):
        d = window_row(j) - j - 1
        for c0 in range(0, rows, NA_SOFTMAX_ROWS):
            rs = slice(c0, c0 + NA_SOFTMAX_ROWS)
            ra = slice(j * rows + c0, j * rows + c0 + NA_SOFTMAX_ROWS)
            s_loc = sl_ref[j, rs, :] + bias_ref[0, d, rs, :]
            s_ctx = sc_ref[ra, :]
            m = jnp.maximum(jnp.max(s_loc, axis=1, keepdims=True), jnp.max(s_ctx, axis=1, keepdims=True))
            p_loc = jnp.exp2(s_loc - m)
            p_ctx = jnp.exp2(s_ctx - m)
            l = jnp.sum(p_loc, axis=1, keepdims=True) + jnp.sum(p_ctx, axis=1, keepdims=True)
            pl_ref[j, rs, :] = p_loc.astype(BF16)
            pc_ref[ra, :] = p_ctx.astype(pc_ref.dtype)
            li_ref[ra, :] = jnp.broadcast_to(1.0 / l, (NA_SOFTMAX_ROWS, 2 * NA_HEAD_DIM))
    for j in range(NA_ROWS_PER_STEP):
        start = pl.multiple_of(window_row(j) * GRID_W, GRID_W)
        vw = wv_ref[pl.ds(start, NA_WIN_ROWS * GRID_W), :]
        ol_ref[j * rows:(j + 1) * rows, :] = jnp.dot(pl_ref[j], vw, preferred_element_type=F32)
    o = (ol_ref[...] + jnp.dot(pc_ref[...], cv_ref[0], preferred_element_type=F32)) * li_ref[...]
    for j in range(NA_ROWS_PER_STEP):
        r0 = j * rows
        oj = jnp.where(first_head, o[r0:r0 + GRID_W], o[r0 + GRID_W:r0 + rows])
        o_ref[0, j * GRID_W:(j + 1) * GRID_W, :] = oj.astype(o_ref.dtype)


def _na(q, k, v, ck, cv, bias):
    B, L, W = q.shape
    Lc = ck.shape[1]
    blk = NA_ROWS_PER_STEP * GRID_W
    nblk = L // blk
    assert NA_ROWS_PER_STEP == NA_WIN_ROWS and nblk >= 2 and GRID_W * 2 == LANES
    pw = 2 * NA_HEAD_DIM
    cur = pl.BlockSpec((1, blk, pw), lambda b, h, i: (b, i, h))
    prev = pl.BlockSpec((1, blk, pw), lambda b, h, i: (b, jnp.maximum(i - 1, 0), h))
    nxt = pl.BlockSpec((1, blk, pw), lambda b, h, i: (b, jnp.minimum(i + 1, nblk - 1), h))
    kcur = pl.BlockSpec((1, pw, blk), lambda b, h, i: (b, h, i))
    kprev = pl.BlockSpec((1, pw, blk), lambda b, h, i: (b, h, jnp.maximum(i - 1, 0)))
    knxt = pl.BlockSpec((1, pw, blk), lambda b, h, i: (b, h, jnp.minimum(i + 1, nblk - 1)))
    cspec = pl.BlockSpec((1, Lc, pw), lambda b, h, i: (b, 0, h))
    bspec = pl.BlockSpec((1,) + bias.shape[1:], lambda b, h, i: (h, 0, 0, 0))
    stacked = 2 * blk
    return pl.pallas_call(
        _na_kernel,
        out_shape=jax.ShapeDtypeStruct((B, L, W), BF16),
        grid=(B, W // pw, nblk),
        in_specs=[cur, kprev, kcur, knxt, prev, cur, nxt, cspec, cspec, bspec],
        out_specs=cur,
        scratch_shapes=[pltpu.VMEM((pw, 3 * blk), BF16), pltpu.VMEM((pw, 3 * blk), BF16),
                        pltpu.VMEM((3 * blk, pw), BF16),
                        pltpu.VMEM((stacked, Lc), F32), pltpu.VMEM((stacked, Lc), BF16),
                        pltpu.VMEM((stacked, pw), F32), pltpu.VMEM((stacked, pw), F32),
                        pltpu.VMEM((NA_ROWS_PER_STEP, 2 * GRID_W, NA_WIN_ROWS * GRID_W), F32),
                        pltpu.VMEM((NA_ROWS_PER_STEP, 2 * GRID_W, NA_WIN_ROWS * GRID_W), BF16)],
        compiler_params=pltpu.CompilerParams(vmem_limit_bytes=VMEM_LIMIT),
        name="nattn",
    )(q, k, k, k, v, v, v, ck, cv, bias)


def _ctx_attn_kernel(q_ref, k_ref, v_ref, o_ref):
    q = q_ref[0]
    k = k_ref[0]
    v = v_ref[0]
    first_head = lax.broadcasted_iota(jnp.int32, q.shape, 1) < NA_HEAD_DIM
    outs = []
    for h in range(2):
        qm = jnp.where(first_head if h == 0 else jnp.logical_not(first_head), q, jnp.zeros_like(q))
        s = lax.dot_general(qm, k, _NT, preferred_element_type=F32)
        p = jnp.exp2(s - jnp.max(s, axis=1, keepdims=True))
        o = jnp.dot(p.astype(BF16), v, preferred_element_type=F32)
        outs.append(o / jnp.sum(p, axis=1, keepdims=True))
    o_ref[0] = jnp.where(first_head, outs[0], outs[1]).astype(o_ref.dtype)


def _ctx_attn(q, k, v):
    B, Lc, W = q.shape
    pw = 2 * NA_HEAD_DIM
    spec = pl.BlockSpec((1, Lc, pw), lambda b, h: (b, 0, h))
    return pl.pallas_call(
        _ctx_attn_kernel,
        out_shape=jax.ShapeDtypeStruct((B, Lc, W), BF16),
        grid=(B, W // pw),
        in_specs=[spec, spec, spec],
        out_specs=spec,
        name="ctx_attn",
    )(q, k, v)


LRU_CHUNK = 2048


def _lru_gate_weights(wa, ba, wi, bi, lam):
    def bd(w):
        nb, c, _ = w.shape
        blk_id = np.arange(nb * c) // c
        tiled = jnp.tile(w.reshape(nb * c, c), (1, nb))
        return jnp.where(jnp.asarray(blk_id[:, None] == blk_id[None, :]), tiled, 0.0)
    wg = jnp.stack([jnp.concatenate([bd(wa[d]), bd(wi[d])], axis=1) for d in range(2)]).astype(BF16)
    bg = jnp.stack([jnp.concatenate([ba[d], bi[d]])[None, :] for d in range(2)])
    return wg, bg, lam[:, None, :]


def _lru_coeffs(u, wg, bg, lam):
    C = u.shape[1]
    g = jnp.dot(u.astype(BF16), wg, preferred_element_type=F32) + bg
    sig = 0.5 + 0.5 * jnp.tanh(0.5 * g)
    r, ig = sig[:, :C], sig[:, C:]
    nl = -lam
    softplus = jnp.maximum(nl, 0.0) + jnp.log(1.0 + jnp.exp(-jnp.abs(nl)))
    log_a = (-LRU_C * softplus) * r
    a = jnp.exp(log_a)
    t = jnp.tanh(log_a)
    b = jnp.sqrt(-2.0 * t / (1.0 - t)) * (ig * u)
    return a, b


def _lru_scan(a, b, h0, reverse, ac_ref, bc_ref, h_ref):
    T, C = a.shape
    row = lax.broadcasted_iota(jnp.int32, a.shape, 0) % SUBLANES
    for s in (1, 2, 4):
        shift = T - s if reverse else s
        keep = (row < SUBLANES - s) if reverse else (row >= s)
        b = jnp.where(keep, a * pltpu.roll(b, shift, 0) + b, b)
        a = jnp.where(keep, a * pltpu.roll(a, shift, 0), a)
    ac_ref[...] = a
    bc_ref[...] = b
    ng = T // SUBLANES

    def group(g, h):
        r0 = pl.multiple_of((ng - 1 - g if reverse else g) * SUBLANES, SUBLANES)
        hr = ac_ref[pl.ds(r0, SUBLANES), :] * h + bc_ref[pl.ds(r0, SUBLANES), :]
        h_ref[pl.ds(r0, SUBLANES), :] = hr
        edge = hr[0:1] if reverse else hr[SUBLANES - 1:SUBLANES]
        return jnp.broadcast_to(edge, (SUBLANES, C))

    return lax.fori_loop(0, ng, group, h0, unroll=4)


def _gelu_tanh(x):
    return 0.5 * x * (1.0 + jnp.tanh(math.sqrt(2.0 / math.pi) * (x + 0.044715 * (x * x * x))))


def _lru_ctx_kernel(u_ref, xg_ref, wg_ref, bg_ref, lam_ref, hend_ref, yc_ref, ac_ref, bc_ref, h_ref):
    u = u_ref[0]
    C = u.shape[1]
    total = jnp.zeros_like(u)
    for d, rev in enumerate((False, True)):
        a, b = _lru_coeffs(u, wg_ref[d], bg_ref[d], lam_ref[d])
        hl = _lru_scan(a, b, jnp.zeros((SUBLANES, C), F32), rev, ac_ref, bc_ref, h_ref)
        hend_ref[0, d:d + 1, :] = hl[0:1]
        total = total + h_ref[...]
    yc_ref[0] = (total * _gelu_tanh(xg_ref[0])).astype(yc_ref.dtype)


def _lru_ctx(u, xg, wg, bg, lam):
    B, Lc, C = u.shape
    tok = pl.BlockSpec((1, Lc, C), lambda b: (b, 0, 0))
    full = lambda a: pl.BlockSpec(a.shape, lambda b: (0,) * a.ndim)
    return pl.pallas_call(
        _lru_ctx_kernel,
        out_shape=[jax.ShapeDtypeStruct((B, 2, C), F32), jax.ShapeDtypeStruct((B, Lc, C), BF16)],
        grid=(B,),
        in_specs=[tok, tok, full(wg), full(bg), full(lam)],
        out_specs=[pl.BlockSpec((1, 2, C), lambda b: (b, 0, 0)), tok],
        scratch_shapes=[pltpu.VMEM((Lc, C), F32)] * 3,
        name="lru_ctx",
    )(u, xg, wg, bg, lam)


def _lru_dir_kernel(*refs, d, reverse):
    if reverse:
        u_ref, hend_ref, wg_ref, bg_ref, lam_ref, hf_ref, xg_ref, o_ref, ac_ref, bc_ref, h_ref, carry_ref = refs
    else:
        u_ref, hend_ref, wg_ref, bg_ref, lam_ref, o_ref, ac_ref, bc_ref, carry_ref = refs
        h_ref = o_ref.at[0]
    C = u_ref.shape[2]

    @pl.when(pl.program_id(1) == 0)
    def _():
        carry_ref[...] = jnp.broadcast_to(hend_ref[0, d:d + 1, :], (SUBLANES, C))

    a, b = _lru_coeffs(u_ref[0], wg_ref[d], bg_ref[d], lam_ref[d])
    carry_ref[...] = _lru_scan(a, b, carry_ref[...], reverse, ac_ref, bc_ref, h_ref)
    if reverse:
        o_ref[0] = ((hf_ref[0] + h_ref[...]) * _gelu_tanh(xg_ref[0])).astype(o_ref.dtype)


def _lru_dir(u, hend, wg, bg, lam, hf=None, xg=None):
    B, L, C = u.shape
    reverse = hf is not None
    T = LRU_CHUNK
    nb = L // T
    tok = pl.BlockSpec((1, T, C), (lambda b, i: (b, nb - 1 - i, 0)) if reverse else (lambda b, i: (b, i, 0)))
    full = lambda a: pl.BlockSpec(a.shape, lambda b, i: (0,) * a.ndim)
    ins = [u, hend, wg, bg, lam] + ([hf, xg] if reverse else [])
    specs = [tok, pl.BlockSpec((1, 2, C), lambda b, i: (b, 0, 0)), full(wg), full(bg), full(lam)]
    specs += [tok, tok] if reverse else []
    scratch = [pltpu.VMEM((T, C), F32)] * (3 if reverse else 2) + [pltpu.VMEM((SUBLANES, C), F32)]
    return pl.pallas_call(
        functools.partial(_lru_dir_kernel, d=int(reverse), reverse=reverse),
        out_shape=jax.ShapeDtypeStruct((B, L, C), BF16 if reverse else F32),
        grid=(B, nb),
        in_specs=specs,
        out_specs=tok,
        scratch_shapes=scratch,
        compiler_params=pltpu.CompilerParams(dimension_semantics=("arbitrary", "arbitrary")),
        name="lru_bwd" if reverse else "lru_fwd",
    )(*ins)


def _lru_both_kernel(u_ref, hend_ref, wg_ref, bg_ref, lam_ref, xg_ref, o_ref, ac_ref, bc_ref, hb_ref, carry_ref,
                     hf_ref, *, nb):
    i = pl.program_id(1)
    T, C = u_ref.shape[1], u_ref.shape[2]

    for d in range(2):
        @pl.when(i == d * nb)
        def _():
            carry_ref[...] = jnp.broadcast_to(hend_ref[0, d:d + 1, :], (SUBLANES, C))

    @pl.when(i < nb)
    def _():
        a, b = _lru_coeffs(u_ref[0], wg_ref[0], bg_ref[0], lam_ref[0])
        r0 = pl.multiple_of(i * T, T)
        carry_ref[...] = _lru_scan(a, b, carry_ref[...], False, ac_ref, bc_ref, hf_ref.at[pl.ds(r0, T)])

    @pl.when(i >= nb)
    def _():
        a, b = _lru_coeffs(u_ref[0], wg_ref[1], bg_ref[1], lam_ref[1])
        carry_ref[...] = _lru_scan(a, b, carry_ref[...], True, ac_ref, bc_ref, hb_ref)
        r0 = pl.multiple_of((2 * nb - 1 - i) * T, T)
        o_ref[0] = ((hf_ref[pl.ds(r0, T), :] + hb_ref[...]) * _gelu_tanh(xg_ref[0])).astype(o_ref.dtype)


def _lru_both(u, xg, hend, wg, bg, lam):
    B, L, C = u.shape
    T = LRU_CHUNK
    nb = L // T
    chunk = lambda b, i: (b, jnp.where(i < nb, i, 2 * nb - 1 - i), 0)
    late = lambda b, i: (b, jnp.where(i < nb, nb - 1, 2 * nb - 1 - i), 0)
    full = lambda a: pl.BlockSpec(a.shape, lambda b, i: (0,) * a.ndim)
    return pl.pallas_call(
        functools.partial(_lru_both_kernel, nb=nb),
        out_shape=jax.ShapeDtypeStruct((B, L, C), BF16),
        grid=(B, 2 * nb),
        in_specs=[pl.BlockSpec((1, T, C), chunk), pl.BlockSpec((1, 2, C), lambda b, i: (b, 0, 0)), full(wg), full(bg),
                  full(lam), pl.BlockSpec((1, T, C), late)],
        out_specs=pl.BlockSpec((1, T, C), late),
        scratch_shapes=[pltpu.VMEM((T, C), F32)] * 3 + [pltpu.VMEM((SUBLANES, C), F32), pltpu.VMEM((L, C), F32)],
        compiler_params=pltpu.CompilerParams(dimension_semantics=("arbitrary", "arbitrary"),
                                             vmem_limit_bytes=VMEM_LIMIT),
        name="lru_both",
    )(u, hend, wg, bg, lam, xg)


def _lru(u, xg, u_c, xg_c, wa, ba, wi, bi, lam):
    wg, bg, lam3 = _lru_gate_weights(wa, ba, wi, bi, lam)
    hend, yc = _lru_ctx(u_c, xg_c, wg, bg, lam3)
    return _lru_both(u, xg, hend, wg, bg, lam3), yc


def kernel(x, c, ctx, c_ctx, ada_w, ada_b, g_mix_pre, g_mix_post, g_ffn_pre, g_ffn_post, w_in, w_out, hy_conv_w,
           hy_conv_b, hy_f_w1, hy_f_b1, hy_f_w2, hy_f_b2, hy_f_w3, hy_f_freq, hy_bias, na_rpb, lru_conv_w,
           lru_conv_b, lru_wa, lru_ba, lru_wi, lru_bi, lru_lam, ffn_w_gu, ffn_w_down):
    B, L, D = x.shape
    Lc = ctx.shape[1]
    tm = 1024

    assert B + 1 <= SUBLANES
    cond_t = jnp.zeros((D, SUBLANES), F32).at[:, 0:B].set(c.T).at[:, B].set(c_ctx)
    mods = _modulation(cond_t, B + 1, ada_w, ada_b)

    xc = ctx
    for l in range(DEPTH):
        with_ctx_out = l < DEPTH - 1
        m = mods[l].reshape(8, 6, D)
        lat = [m[0:B, j][:, None, :] for j in range(6)]
        cx = [jnp.broadcast_to(m[B, j][None, None, :], (B, 1, D)) for j in range(6)]
        row = lambda a: a.reshape(1, -1)

        w_in_bf = w_in[l].astype(BF16)
        w_out_bf = w_out[l].astype(BF16)
        wgu = ffn_w_gu[l].astype(BF16)
        wd = ffn_w_down[l].astype(BF16)

        conv_args = (hy_conv_w[l], row(hy_conv_b[l]), lru_conv_w[l], row(lru_conv_b[l]))
        hv, hx1, hx2, q, k, v, lu, lg = _inproj(x, lat[1], lat[0], row(g_mix_pre[l]), w_in_bf, *conv_args, tm=tm,
                                                chan_major=True)
        cv, cx1, cx2, cq, ck, cvv, clu, clg = _inproj(xc, cx[1], cx[0], row(g_mix_pre[l]), w_in_bf, *conv_args,
                                                      tm=Lc, chan_major=False)

        filt = (hy_f_w1[l], hy_f_b1[l], hy_f_w2[l], hy_f_b2[l], hy_f_w3[l], hy_f_freq[l])
        y_hy = _hyena(hv, hx1, hx2, filt, hy_bias[l])
        y_na = _na(q, k, v, ck, cvv, _na_bias_table(na_rpb[l]))
        y_lru, yc_lru = _lru(lu, lg, clu, clg, lru_wa[l], lru_ba[l], lru_wi[l], lru_bi[l], lru_lam[l])

        ffn_args = (row(g_ffn_pre[l]), row(g_ffn_post[l]), wgu, wd)
        x = _out_ffn(y_hy, y_na, y_lru, w_out_bf, x, row(g_mix_post[l]), lat[2], lat[4], lat[3], lat[5], *ffn_args,
                     tm=tm, chan_major=True)

        if with_ctx_out:
            yc_hy = _hyena_ctx(cv, cx1, cx2, filt, hy_bias[l])
            yc_na = _ctx_attn(cq, ck, cvv)
            xc = _out_ffn(yc_hy, yc_na, yc_lru, w_out_bf, xc, row(g_mix_post[l]), cx[2], cx[4], cx[3], cx[5],
                          *ffn_args, tm=Lc, chan_major=False)
    return x
```
